```python
import math
import jax, jax.numpy as jnp
from jax import lax
import numpy as np

D_MODEL = 1024
BATCH = 8
SEQ = 4096
DEPTH = 1

CTX_LEN = 256
GRID_W = 64
MIX_WIDTH = D_MODEL
ATTN_WIDTH = MIX_WIDTH // 2
HYENA_WIDTH = MIX_WIDTH - ATTN_WIDTH
N_HEADS = 4
V_DIM = ATTN_WIDTH // N_HEADS
QK_DIM = V_DIM // 2
QK_COLS = N_HEADS * 2 * QK_DIM
V_COLS = N_HEADS * V_DIM
HYENA_ORDER = 2
HYENA_COLS = (HYENA_ORDER + 1) * HYENA_WIDTH
IN_COLS = 2 * QK_COLS + V_COLS + HYENA_COLS
SHORT_CONV = 3
N_BANDS = 8
FEAT_DIM = 1 + 2 * N_BANDS
FILTER_HIDDEN = 64
DECAY_TARGET = 1e-2
FAST_DECAY_PCT = 0.3
SLOW_DECAY_PCT = 1.5
N_EXPERTS = 32
TOP_K = 4
D_EXPERT = D_MODEL
SWIGLU_LIMIT = 7.0
SWIGLU_ALPHA = 1.702
EXPERT_BLOCK = 128
Q_BLOCK = 128
ROPE_BASE = 10000.0
EPS = 1e-6
N_MOD = 6

kernel_name = "hybrid_diffattn_hyena_moe_dit"

F32 = jnp.float32


def rms_norm(x, g):
    x32 = x.astype(F32)
    y = x32 * lax.rsqrt(jnp.mean(x32 * x32, axis=-1, keepdims=True) + EPS)
    return (y * g.astype(F32)).astype(x.dtype)


def modulate(h, shift, scale):
    return h * (1 + scale) + shift


def rope_2d(x, row, col):
    half = QK_DIM // 2
    nf = half // 2
    inv = ROPE_BASE ** (-jnp.arange(nf, dtype=F32) / nf)

    def rot(xp, pos):
        ang = pos.astype(F32)[:, None] * inv[None]
        cos = jnp.cos(ang)[None, :, None, None, :].astype(x.dtype)
        sin = jnp.sin(ang)[None, :, None, None, :].astype(x.dtype)
        x1, x2 = xp[..., :nf], xp[..., nf:]
        return jnp.concatenate([x1 * cos - x2 * sin, x2 * cos + x1 * sin], axis=-1)

    return jnp.concatenate([rot(x[..., :half], row), rot(x[..., half:], col)], axis=-1)


def short_conv(u, w, b):
    L = u.shape[1]
    pad = SHORT_CONV // 2
    up = jnp.pad(u, ((0, 0), (pad, pad), (0, 0)))
    y = b
    for j in range(SHORT_CONV):
        y = y + up[:, j:j + L] * w[j]
    return y


def hyena_filters(L, w1, b1, f1, w2, b2, f2, w3):
    pos = jnp.arange(L, dtype=F32)
    tn = pos / L
    bands = jnp.linspace(1e-4, N_BANDS - 1, N_BANDS, dtype=F32)
    ang = (2.0 * math.pi / L) * pos[:, None] * bands[None]
    feats = jnp.concatenate([tn[:, None], jnp.sin(ang), jnp.cos(ang)], axis=-1)
    h = jnp.sin(f1 * (feats @ w1 + b1))
    h = jnp.sin(f2 * (h @ w2 + b2))
    h = (h @ w3).astype(F32)
    deltas = jnp.abs(jnp.linspace(math.log(DECAY_TARGET) / SLOW_DECAY_PCT,
                                  math.log(DECAY_TARGET) / FAST_DECAY_PCT,
                                  HYENA_WIDTH, dtype=F32))
    window = jnp.exp(-tn[:, None] * deltas[None])
    return h.reshape(L, HYENA_ORDER, 2, HYENA_WIDTH) * window[:, None, None, :]


def long_conv_bidir(u, h_fwd, h_bwd, skip):
    L, C = u.shape[1], u.shape[2]
    h_circ = jnp.concatenate([h_fwd[:1] + h_bwd[:1], h_fwd[1:],
                              jnp.zeros((1, C), F32), h_bwd[:0:-1]], axis=0)
    hf = jnp.fft.rfft(h_circ, n=2 * L, axis=0)
    u32 = u.astype(F32)
    uf = jnp.fft.rfft(u32, n=2 * L, axis=1)
    y = jnp.fft.irfft(uf * hf[None], n=2 * L, axis=1)[:, :L]
    return (y + u32 * skip.astype(F32)).astype(u.dtype)


def hyena_mixer(u, conv_w, conv_b, filters, skip):
    u = short_conv(u, conv_w, conv_b)
    v, x1, x2 = jnp.split(u, 3, axis=-1)
    z = v
    for n, gate in enumerate((x1, x2)):
        z = gate * long_conv_bidir(z, filters[:, n, 0], filters[:, n, 1], skip[n])
    return z


def diff_attend(q, k, v, lam):
    s = jnp.einsum('bqhmd,bkhmd->bhmqk', q, k).astype(F32) * (QK_DIM ** -0.5)
    p = jax.nn.softmax(s, axis=-1)
    a = p[:, :, 0] - lam * p[:, :, 1]
    return jnp.einsum('bhqk,bkhd->bqhd', a.astype(v.dtype), v)


def diff_attention_blocked(q, k_all, v_all, lam):
    B, L = q.shape[:2]
    nb = L // Q_BLOCK
    qb = q.reshape(B, nb, Q_BLOCK, N_HEADS, 2, QK_DIM).swapaxes(0, 1)
    out = lax.map(lambda qi: diff_attend(qi, k_all, v_all, lam), qb)
    return out.swapaxes(0, 1).reshape(B, L, N_HEADS, V_DIM)


def moe(h, router_w, router_b, w1, b1, w2, b2):
    T, D = h.shape
    logits = (h @ router_w + router_b).astype(F32)
    top_val, top_idx = lax.top_k(logits, TOP_K)
    gates = jax.nn.softmax(top_val, axis=-1)
    n_assign = T * TOP_K
    flat_e = top_idx.reshape(-1)
    flat_tok = jnp.arange(n_assign) // TOP_K
    order = jnp.argsort(flat_e)
    e_sorted = flat_e[order]
    tok_sorted = flat_tok[order]
    gate_sorted = gates.reshape(-1)[order]
    counts = jnp.bincount(flat_e, length=N_EXPERTS)
    padded = ((counts + EXPERT_BLOCK - 1) // EXPERT_BLOCK) * EXPERT_BLOCK
    start = jnp.cumsum(counts) - counts
    pend = jnp.cumsum(padded)
    pstart = pend - padded
    dest = pstart[e_sorted] + (jnp.arange(n_assign) - start[e_sorted])
    n_blocks = -(-n_assign // EXPERT_BLOCK) + N_EXPERTS
    buf = jnp.zeros((n_blocks * EXPERT_BLOCK, D), h.dtype).at[dest].set(h[tok_sorted])
    block_e = jnp.clip(jnp.searchsorted(pend, jnp.arange(n_blocks) * EXPERT_BLOCK, side='right'),
                       0, N_EXPERTS - 1)

    def expert_block(args):
        xb, e = args
        gl = xb @ w1[e] + b1[e]
        g, lin = gl[:, :D_EXPERT], gl[:, D_EXPERT:]
        g = jnp.minimum(g, SWIGLU_LIMIT)
        lin = jnp.clip(lin, -SWIGLU_LIMIT, SWIGLU_LIMIT)
        glu = g * jax.nn.sigmoid(SWIGLU_ALPHA * g)
        return ((lin + 1) * glu) @ w2[e] + b2[e]

    out_buf = lax.map(expert_block, (buf.reshape(n_blocks, EXPERT_BLOCK, D), block_e))
    y = out_buf.reshape(-1, D)[dest] * gate_sorted[:, None].astype(h.dtype)
    return jax.ops.segment_sum(y, tok_sorted, num_segments=T)


def trunk_layer(x, ctx, c, c_ctx, p, lam_init, update_ctx):
    B, S, D = x.shape
    Lc = ctx.shape[1]
    ROWS = S // GRID_W
    row = jnp.repeat(jnp.arange(ROWS), GRID_W)
    col = jnp.tile(jnp.arange(GRID_W), ROWS)

    mod_x = (jax.nn.silu(c) @ p['w_mod'] + p['b_mod']).reshape(B, N_MOD, 1, D)
    mod_c = (jax.nn.silu(c_ctx) @ p['w_mod'] + p['b_mod']).reshape(1, N_MOD, 1, D)
    sh1, sc1, g1, sh2, sc2, g2 = [mod_x[:, i] for i in range(N_MOD)]
    csh1, csc1, cg1, csh2, csc2, cg2 = [mod_c[:, i] for i in range(N_MOD)]

    w_in = p['w_in']
    lam = (jnp.exp(jnp.sum(p['lam_q1'].astype(F32) * p['lam_k1'].astype(F32)))
           - jnp.exp(jnp.sum(p['lam_q2'].astype(F32) * p['lam_k2'].astype(F32))) + lam_init)

    h = modulate(rms_norm(x, p['norm1_g']), sh1, sc1)
    hc = modulate(rms_norm(ctx, p['norm1_g']), csh1, csc1)
    proj = h @ w_in
    q = proj[..., :QK_COLS].reshape(B, S, N_HEADS, 2, QK_DIM)
    k = proj[..., QK_COLS:2 * QK_COLS].reshape(B, S, N_HEADS, 2, QK_DIM)
    v = proj[..., 2 * QK_COLS:2 * QK_COLS + V_COLS].reshape(B, S, N_HEADS, V_DIM)
    u_hy = proj[..., 2 * QK_COLS + V_COLS:]
    q = rope_2d(rms_norm(q, p['q_norm_g']), row, col)
    k = rope_2d(rms_norm(k, p['k_norm_g']), row, col)

    kv_c = hc @ w_in[:, QK_COLS:2 * QK_COLS + V_COLS]
    k_c = rms_norm(kv_c[..., :QK_COLS].reshape(B, Lc, N_HEADS, 2, QK_DIM), p['k_norm_g'])
    v_c = kv_c[..., QK_COLS:].reshape(B, Lc, N_HEADS, V_DIM)

    k_all = jnp.concatenate([k_c, k], axis=1)
    v_all = jnp.concatenate([v_c, v], axis=1)
    attn = diff_attention_blocked(q, k_all, v_all, lam)
    attn = (rms_norm(attn, p['subln_g']) * (1.0 - lam_init)).reshape(B, S, ATTN_WIDTH)

    filt_x = hyena_filters(S, p['hy_w1'], p['hy_b1'], p['hy_f1'], p['hy_w2'], p['hy_b2'], p['hy_f2'], p['hy_w3'])
    hy = hyena_mixer(u_hy, p['hy_conv_w'], p['hy_conv_b'], filt_x, p['hy_skip'])
    mix = jnp.concatenate([attn, rms_norm(hy, p['hy_out_g'])], axis=-1) @ p['w_out']
    x_new = x + g1 * mix

    h2 = modulate(rms_norm(x_new, p['norm2_g']), sh2, sc2)
    ffn = moe(h2.reshape(B * S, D), p['router_w'], p['router_b'],
              p['exp_w1'], p['exp_b1'], p['exp_w2'], p['exp_b2']).reshape(B, S, D)
    x_new = x_new + g2 * ffn

    if update_ctx:
        q_c = rms_norm((hc @ w_in[:, :QK_COLS]).reshape(B, Lc, N_HEADS, 2, QK_DIM), p['q_norm_g'])
        attn_c = diff_attend(q_c, k_c, v_c, lam)
        attn_c = (rms_norm(attn_c, p['subln_g']) * (1.0 - lam_init)).reshape(B, Lc, ATTN_WIDTH)
        filt_c = hyena_filters(Lc, p['hy_w1'], p['hy_b1'], p['hy_f1'], p['hy_w2'], p['hy_b2'], p['hy_f2'], p['hy_w3'])
        hy_c = hyena_mixer(hc @ w_in[:, 2 * QK_COLS + V_COLS:], p['hy_conv_w'], p['hy_conv_b'], filt_c, p['hy_skip'])
        mix_c = jnp.concatenate([attn_c, rms_norm(hy_c, p['hy_out_g'])], axis=-1) @ p['w_out']
        ctx = ctx + cg1 * mix_c
        hc2 = modulate(rms_norm(ctx, p['norm2_g']), csh2, csc2)
        ffn_c = moe(hc2.reshape(B * Lc, D), p['router_w'], p['router_b'],
                    p['exp_w1'], p['exp_b1'], p['exp_w2'], p['exp_b2']).reshape(B, Lc, D)
        ctx = ctx + cg2 * ffn_c
    return x_new, ctx


def setup_inputs(seed: int = 0) -> dict:
    key = jax.random.key(seed)
    ks = iter(jax.random.split(key, 40))

    def nrm(shape, scale):
        return jax.random.normal(next(ks), shape, F32) * scale

    def gain(shape):
        return 1.0 + nrm(shape, 0.01)

    L = DEPTH
    return {
        'x': nrm((BATCH, SEQ, D_MODEL), 1.0),
        'c': nrm((BATCH, D_MODEL), 1.0),
        'ctx': nrm((BATCH, CTX_LEN, D_MODEL), 1.0),
        'c_ctx': nrm((D_MODEL,), 1.0),
        'w_mod': nrm((L, D_MODEL, N_MOD * D_MODEL), 0.5 * D_MODEL ** -0.5),
        'b_mod': nrm((L, N_MOD * D_MODEL), 0.01),
        'norm1_g': gain((L, D_MODEL)),
        'norm2_g': gain((L, D_MODEL)),
        'w_in': nrm((L, D_MODEL, IN_COLS), D_MODEL ** -0.5),
        'q_norm_g': gain((L, QK_DIM)),
        'k_norm_g': gain((L, QK_DIM)),
        'lam_q1': nrm((L, QK_DIM), 0.1),
        'lam_k1': nrm((L, QK_DIM), 0.1),
        'lam_q2': nrm((L, QK_DIM), 0.1),
        'lam_k2': nrm((L, QK_DIM), 0.1),
        'subln_g': gain((L, V_DIM)),
        'hy_conv_w': nrm((L, SHORT_CONV, HYENA_COLS), SHORT_CONV ** -0.5),
        'hy_conv_b': nrm((L, HYENA_COLS), 0.01),
        'hy_w1': nrm((L, FEAT_DIM, FILTER_HIDDEN), FEAT_DIM ** -0.5),
        'hy_b1': nrm((L, FILTER_HIDDEN), 0.01),
        'hy_f1': gain((L, FILTER_HIDDEN)),
        'hy_w2': nrm((L, FILTER_HIDDEN, FILTER_HIDDEN), FILTER_HIDDEN ** -0.5),
        'hy_b2': nrm((L, FILTER_HIDDEN), 0.01),
        'hy_f2': gain((L, FILTER_HIDDEN)),
        'hy_w3': nrm((L, FILTER_HIDDEN, HYENA_ORDER * 2 * HYENA_WIDTH), 0.07 * FILTER_HIDDEN ** -0.5),
        'hy_skip': nrm((L, HYENA_ORDER, HYENA_WIDTH), 0.1),
        'hy_out_g': gain((L, HYENA_WIDTH)),
        'w_out': nrm((L, MIX_WIDTH, D_MODEL), MIX_WIDTH ** -0.5),
        'router_w': nrm((L, D_MODEL, N_EXPERTS), D_MODEL ** -0.5),
        'router_b': nrm((L, N_EXPERTS), 0.01),
        'exp_w1': nrm((L, N_EXPERTS, D_MODEL, 2 * D_EXPERT), D_MODEL ** -0.5),
        'exp_b1': nrm((L, N_EXPERTS, 2 * D_EXPERT), 0.01),
        'exp_w2': nrm((L, N_EXPERTS, D_EXPERT, D_MODEL), D_EXPERT ** -0.5),
        'exp_b2': nrm((L, N_EXPERTS, D_MODEL), 0.01),
    }


def reference(x, c, ctx, c_ctx, w_mod, b_mod, norm1_g, norm2_g, w_in, q_norm_g, k_norm_g,
              lam_q1, lam_k1, lam_q2, lam_k2, subln_g, hy_conv_w, hy_conv_b, hy_w1, hy_b1,
              hy_f1, hy_w2, hy_b2, hy_f2, hy_w3, hy_skip, hy_out_g, w_out, router_w, router_b,
              exp_w1, exp_b1, exp_w2, exp_b2):
    for l in range(DEPTH):
        p = {
            'w_mod': w_mod[l], 'b_mod': b_mod[l], 'norm1_g': norm1_g[l], 'norm2_g': norm2_g[l],
            'w_in': w_in[l], 'q_norm_g': q_norm_g[l], 'k_norm_g': k_norm_g[l],
            'lam_q1': lam_q1[l], 'lam_k1': lam_k1[l], 'lam_q2': lam_q2[l], 'lam_k2': lam_k2[l],
            'subln_g': subln_g[l], 'hy_conv_w': hy_conv_w[l], 'hy_conv_b': hy_conv_b[l],
            'hy_w1': hy_w1[l], 'hy_b1': hy_b1[l], 'hy_f1': hy_f1[l], 'hy_w2': hy_w2[l],
            'hy_b2': hy_b2[l], 'hy_f2': hy_f2[l], 'hy_w3': hy_w3[l], 'hy_skip': hy_skip[l],
            'hy_out_g': hy_out_g[l], 'w_out': w_out[l], 'router_w': router_w[l],
            'router_b': router_b[l], 'exp_w1': exp_w1[l], 'exp_b1': exp_b1[l],
            'exp_w2': exp_w2[l], 'exp_b2': exp_b2[l],
        }
        lam_init = 0.8 - 0.6 * math.exp(-0.3 * l)
        x, ctx = trunk_layer(x, ctx, c, c_ctx, p, lam_init, update_ctx=(l < DEPTH - 1))
    return x
```

```python
import functools
import math

import jax
import jax.numpy as jnp
from jax import lax
from jax.experimental import pallas as pl
from jax.experimental.pallas import tpu as pltpu

F32 = jnp.float32
BF16 = jnp.bfloat16
I32 = jnp.int32

GRID_W = 64
N_HEADS = 4
N_MOD = 6
SHORT_CONV = 3
HYENA_ORDER = 2
N_BANDS = 8
FEAT_DIM = 1 + 2 * N_BANDS
FILTER_HIDDEN = 64
DECAY_TARGET = 1e-2
FAST_DECAY_PCT = 0.3
SLOW_DECAY_PCT = 1.5
N_EXPERTS = 32
TOP_K = 4
SWIGLU_LIMIT = 7.0
SWIGLU_ALPHA = 1.702
ROPE_BASE = 10000.0
EPS = 1e-6

LANES = 128
V7X_VMEM_LIMIT = 56 * 1024 * 1024

ROW_TILE = 512
ATT_Q_TILE = 256
DFT_TILE = 256
EXPERT_ROWS = 512
ROUTE_TILE = 512
DISPATCH_TILE = 1024
COMBINE_TILE = 128


def _log2(n):
    assert n > 0 and n & (n - 1) == 0, f"{n} must be a power of two"
    return n.bit_length() - 1


def _cparams(sem, vmem=V7X_VMEM_LIMIT):
    return pltpu.CompilerParams(dimension_semantics=sem, vmem_limit_bytes=vmem)


def _split_bf16(a):
    hi = a.astype(BF16)
    lo = (a - hi.astype(F32)).astype(BF16)
    return hi, lo


def _dot(a, b):
    return jnp.dot(a, b, preferred_element_type=F32)


def _dot_nt(a, b):
    return lax.dot_general(a, b, (((1,), (1,)), ((), ())), preferred_element_type=F32)


def _store_row_tiles(ref, val):
    for c in range(ref.shape[1]):
        ref[:, c, :] = val[:, c * LANES:(c + 1) * LANES]


def _load_row_tiles(ref):
    return jnp.concatenate([ref[:, c, :] for c in range(ref.shape[1])], axis=1)


def _dot3(a, b):
    ah, al = _split_bf16(a)
    bh, bl = _split_bf16(b)
    return _dot(ah, bh) + (_dot(ah, bl) + _dot(al, bh))


def _mod_kernel(c_ref, w_ref, b_ref, o_ref):
    c = c_ref[...]
    s = c * jax.nn.sigmoid(c)
    o_ref[...] = _dot3(s, w_ref[...]) + b_ref[...]


def _modulation(cc, w_mod, b_mod):
    rows, d = cc.shape
    n = w_mod.shape[1]
    tn = min(n, 1536)
    return pl.pallas_call(
        _mod_kernel,
        grid=(n // tn,),
        in_specs=[pl.BlockSpec((rows, d), lambda j: (0, 0)),
                  pl.BlockSpec((d, tn), lambda j: (0, j)),
                  pl.BlockSpec((1, tn), lambda j: (0, j))],
        out_specs=pl.BlockSpec((rows, tn), lambda j: (0, j)),
        out_shape=jax.ShapeDtypeStruct((rows, n), F32),
        compiler_params=_cparams(("parallel",)),
        name="modulation",
    )(cc, w_mod, b_mod.reshape(1, n))


def _rope_table_kernel(cos_ref, sin_ref, *, qk_dim):
    s, w = cos_ref.shape
    half = qk_dim // 2
    nf = half // 2
    t = lax.broadcasted_iota(I32, (s, w), 0)
    lane = lax.broadcasted_iota(I32, (s, w), 1)
    d = lane & (qk_dim - 1)
    j = d & (nf - 1)
    row = t >> _log2(GRID_W)
    col = t & (GRID_W - 1)
    pos = jnp.where(d < half, row, col).astype(F32)
    inv = jnp.exp(j.astype(F32) * (-math.log(ROPE_BASE) / nf))
    ang = pos * inv
    first = (d & (half - 1)) < nf
    cos_ref[...] = jnp.cos(ang)
    sn = jnp.sin(ang)
    sin_ref[...] = jnp.where(first, -sn, sn)


def _rope_tables(s, qk_dim):
    return pl.pallas_call(
        functools.partial(_rope_table_kernel, qk_dim=qk_dim),
        out_shape=(jax.ShapeDtypeStruct((s, LANES), F32), jax.ShapeDtypeStruct((s, LANES), F32)),
        name="rope_tables",
    )()


def _group_rms(t, gain, qk_dim):
    w = t.shape[1]
    r = lax.broadcasted_iota(I32, (w, w), 0) >> _log2(qk_dim)
    c = lax.broadcasted_iota(I32, (w, w), 1) >> _log2(qk_dim)
    bd = jnp.where(r == c, 1.0 / qk_dim, 0.0).astype(BF16)
    hi, lo = _split_bf16(t * t)
    ms = _dot(hi, bd) + _dot(lo, bd)
    return t * lax.rsqrt(ms + EPS) * gain


def _rope(t, cos, sin_signed, qk_dim):
    w = t.shape[1]
    nf = qk_dim // 4
    lane = lax.broadcasted_iota(I32, t.shape, 1)
    first = (lane & (2 * nf - 1)) < nf
    partner = jnp.where(first, pltpu.roll(t, w - nf, axis=1), pltpu.roll(t, nf, axis=1))
    return t * cos + partner * sin_signed


def _proj_kernel(*refs, latent, qk_cols, v_cols, qk_dim):
    if latent:
        (x_ref, sh_ref, sc_ref, g_ref, w_ref, qg_ref, kg_ref, cos_ref, sin_ref,
         q_out, k_out, v_out, u_out) = refs
    else:
        x_ref, sh_ref, sc_ref, g_ref, w_ref, kg_ref, k_out, v_out = refs
    x = x_ref[0]
    ms = jnp.mean(x * x, axis=-1, keepdims=True)
    h = (x * lax.rsqrt(ms + EPS) * g_ref[...]) * (1.0 + sc_ref[0]) + sh_ref[0]
    proj = _dot(h.astype(BF16), w_ref[...])
    if latent:
        reps = qk_cols // LANES
        cos = jnp.concatenate([cos_ref[...]] * reps, axis=1)
        sin = jnp.concatenate([sin_ref[...]] * reps, axis=1)
        q = _rope(_group_rms(proj[:, :qk_cols], qg_ref[...], qk_dim), cos, sin, qk_dim)
        q_out[0] = (q * (qk_dim ** -0.5)).astype(BF16)
        k = _rope(_group_rms(proj[:, qk_cols:2 * qk_cols], kg_ref[...], qk_dim), cos, sin, qk_dim)
        k_out[0] = k.astype(BF16)
        v_out[0] = proj[:, 2 * qk_cols:2 * qk_cols + v_cols].astype(BF16)
        u_out[0] = proj[:, 2 * qk_cols + v_cols:]
    else:
        k = _group_rms(proj[:, :qk_cols], kg_ref[...], qk_dim)
        k_out[0] = k.astype(BF16)
        v_out[0] = proj[:, qk_cols:qk_cols + v_cols].astype(BF16)


def _project_latent(x, sh, sc, g, w_bf, qg, kg, cos_t, sin_t, qk_cols, v_cols, qk_dim):
    b, s, d = x.shape
    n = w_bf.shape[1]
    hy_cols = n - 2 * qk_cols - v_cols
    tm = min(ROW_TILE, s)
    row = lambda bi, i: (bi, i, 0)
    per_b = lambda bi, i: (bi, 0, 0)
    const = lambda bi, i: (0, 0)
    return pl.pallas_call(
        functools.partial(_proj_kernel, latent=True, qk_cols=qk_cols, v_cols=v_cols, qk_dim=qk_dim),
        grid=(b, s // tm),
        in_specs=[pl.BlockSpec((1, tm, d), row),
                  pl.BlockSpec((1, 1, d), per_b), pl.BlockSpec((1, 1, d), per_b),
                  pl.BlockSpec((1, d), const), pl.BlockSpec((d, n), const),
                  pl.BlockSpec((1, qk_cols), const), pl.BlockSpec((1, qk_cols), const),
                  pl.BlockSpec((tm, LANES), lambda bi, i: (i, 0)),
                  pl.BlockSpec((tm, LANES), lambda bi, i: (i, 0))],
        out_specs=[pl.BlockSpec((1, tm, qk_cols), row), pl.BlockSpec((1, tm, qk_cols), row),
                   pl.BlockSpec((1, tm, v_cols), row), pl.BlockSpec((1, tm, hy_cols), row)],
        out_shape=[jax.ShapeDtypeStruct((b, s, qk_cols), BF16), jax.ShapeDtypeStruct((b, s, qk_cols), BF16),
                   jax.ShapeDtypeStruct((b, s, v_cols), BF16), jax.ShapeDtypeStruct((b, s, hy_cols), F32)],
        compiler_params=_cparams(("parallel", "parallel")),
        name="project_latent",
    )(x, sh, sc, g, w_bf, qg, kg, cos_t, sin_t)


def _project_context(ctx, sh, sc, g, w_bf, kg, qk_cols, v_cols, qk_dim):
    b, lc, d = ctx.shape
    n = w_bf.shape[1]
    tm = min(ROW_TILE, lc)
    row = lambda bi, i: (bi, i, 0)
    shared = lambda bi, i: (0, 0, 0)
    const = lambda bi, i: (0, 0)
    return pl.pallas_call(
        functools.partial(_proj_kernel, latent=False, qk_cols=qk_cols, v_cols=v_cols, qk_dim=qk_dim),
        grid=(b, lc // tm),
        in_specs=[pl.BlockSpec((1, tm, d), row),
                  pl.BlockSpec((1, 1, d), shared), pl.BlockSpec((1, 1, d), shared),
                  pl.BlockSpec((1, d), const), pl.BlockSpec((d, n), const),
                  pl.BlockSpec((1, qk_cols), const)],
        out_specs=[pl.BlockSpec((1, tm, qk_cols), row), pl.BlockSpec((1, tm, v_cols), row)],
        out_shape=[jax.ShapeDtypeStruct((b, lc, qk_cols), BF16), jax.ShapeDtypeStruct((b, lc, v_cols), BF16)],
        compiler_params=_cparams(("parallel", "parallel")),
        name="project_context",
    )(ctx, sh, sc, g, w_bf, kg)


def _attn_kernel(q_ref, k_ref, v_ref, lq1, lk1, lq2, lk2, sg_ref, o_ref, *, lam_init, qk_dim):
    lam = (jnp.exp(jnp.sum(lq1[...] * lk1[...], axis=-1, keepdims=True))
           - jnp.exp(jnp.sum(lq2[...] * lk2[...], axis=-1, keepdims=True)) + lam_init)
    q = q_ref[0]
    k = k_ref[0]
    lane = lax.broadcasted_iota(I32, q.shape, 1)
    zero = jnp.zeros_like(q)

    def probs(qm):
        s = _dot_nt(qm, k)
        e = jnp.exp(s - jnp.max(s, axis=-1, keepdims=True))
        return e, jnp.sum(e, axis=-1, keepdims=True)

    e1, l1 = probs(jnp.where(lane < qk_dim, q, zero))
    e2, l2 = probs(jnp.where(lane >= qk_dim, q, zero))
    a = e1 * (1.0 / l1) - e2 * (lam / l2)
    o = _dot(a.astype(BF16), v_ref[0])
    ms = jnp.mean(o * o, axis=-1, keepdims=True)
    o_ref[0] = ((o * lax.rsqrt(ms + EPS) * sg_ref[...]) * (1.0 - lam_init)).astype(BF16)


def _diff_attention(q, k_all, v_all, lq1, lk1, lq2, lk2, subln_g, lam_init, qk_dim):
    b, s, w = q.shape
    kk = k_all.shape[1]
    tq = min(ATT_Q_TILE, s)
    hb = lambda bi, h, i: (bi, i, h)
    kv = lambda bi, h, i: (bi, 0, h)
    const = lambda bi, h, i: (0, 0)
    vec = pl.BlockSpec((1, qk_dim), const)
    return pl.pallas_call(
        functools.partial(_attn_kernel, lam_init=lam_init, qk_dim=qk_dim),
        grid=(b, N_HEADS, s // tq),
        in_specs=[pl.BlockSpec((1, tq, LANES), hb), pl.BlockSpec((1, kk, LANES), kv),
                  pl.BlockSpec((1, kk, LANES), kv), vec, vec, vec, vec,
                  pl.BlockSpec((1, LANES), const)],
        out_specs=pl.BlockSpec((1, tq, LANES), hb),
        out_shape=jax.ShapeDtypeStruct((b, s, w), BF16),
        compiler_params=_cparams(("parallel", "parallel", "parallel")),
        name="diff_attention",
    )(q, k_all, v_all, lq1, lk1, lq2, lk2, subln_g)


def _filter_kernel(w1_ref, b1_ref, f1_ref, w2_ref, b2_ref, f2_ref, w3_ref, o_ref, *, seq, hw):
    tl, n = o_ref.shape
    base = pl.program_id(0) * tl
    pos = (lax.broadcasted_iota(I32, (tl, LANES), 0) + base).astype(F32)
    lane = lax.broadcasted_iota(I32, (tl, LANES), 1)
    tn = pos / seq
    band_idx = jnp.where(lane <= N_BANDS, lane - 1, lane - 1 - N_BANDS).astype(F32)
    band = 1e-4 + band_idx * ((N_BANDS - 1 - 1e-4) / (N_BANDS - 1))
    ang = (2.0 * math.pi / seq) * pos * band
    feats = jnp.where(lane == 0, tn,
                      jnp.where(lane <= N_BANDS, jnp.sin(ang),
                                jnp.where(lane < FEAT_DIM, jnp.cos(ang), 0.0)))
    h = jnp.sin(f1_ref[...] * (_dot3(feats, w1_ref[...]) + b1_ref[...]))
    h = jnp.sin(f2_ref[...] * (_dot3(h, w2_ref[...]) + b2_ref[...]))
    h = _dot3(h, w3_ref[...])
    ch = (lax.broadcasted_iota(I32, (tl, n), 1) & ((1 << _log2(hw)) - 1)).astype(F32)
    lo = abs(math.log(DECAY_TARGET) / SLOW_DECAY_PCT)
    hi = abs(math.log(DECAY_TARGET) / FAST_DECAY_PCT)
    delta = lo + ch * ((hi - lo) / (hw - 1))
    tn_n = (lax.broadcasted_iota(I32, (tl, n), 0) + base).astype(F32) / seq
    o_ref[...] = (h * jnp.exp(-tn_n * delta)).astype(BF16)


def _hyena_filters(seq, hw, w1, b1, f1, w2, b2, f2, w3):
    fh = w2.shape[0]
    n = w3.shape[1]
    w1p = jnp.zeros((LANES, fh), F32).at[:FEAT_DIM].set(w1)
    tl = min(ROW_TILE, seq)
    const = lambda i: (0, 0)
    return pl.pallas_call(
        functools.partial(_filter_kernel, seq=seq, hw=hw),
        grid=(seq // tl,),
        in_specs=[pl.BlockSpec((LANES, fh), const), pl.BlockSpec((1, fh), const), pl.BlockSpec((1, fh), const),
                  pl.BlockSpec((fh, fh), const), pl.BlockSpec((1, fh), const), pl.BlockSpec((1, fh), const),
                  pl.BlockSpec((fh, n), const)],
        out_specs=pl.BlockSpec((tl, n), lambda i: (i, 0)),
        out_shape=jax.ShapeDtypeStruct((seq, n), BF16),
        compiler_params=_cparams(("parallel",)),
        name="hyena_filters",
    )(w1p, b1.reshape(1, fh), f1.reshape(1, fh), w2, b2.reshape(1, fh), f2.reshape(1, fh), w3)


def _dft_kernel(mf_ref, mi_ref, tfc, tfs, tic, tis, *, seq):
    r = tfc.shape[0]
    mask = (1 << _log2(4 * seq)) - 1
    unit = math.pi / (2 * seq)
    i_row = lax.broadcasted_iota(I32, (r, seq), 0)
    col = lax.broadcasted_iota(I32, (r, seq), 1)

    @pl.when(pl.program_id(0) == 0)
    def _():
        af = ((2 * i_row * col) & mask).astype(F32) * unit
        tfc[...] = jnp.cos(af)
        tfs[...] = jnp.sin(af)
        ai = (((2 * col + 1) * i_row) & mask).astype(F32) * unit
        tic[...] = jnp.cos(ai)
        tis[...] = jnp.sin(ai)

    r0 = pl.program_id(0) * r
    c1 = lax.broadcasted_iota(I32, (1, seq), 1)
    bf = (((2 * r0 + 1) * c1) & mask).astype(F32) * unit
    bi = (((2 * c1 + 1) * r0) & mask).astype(F32) * unit
    cbf, sbf = jnp.cos(bf), jnp.sin(bf)
    cbi, sbi = jnp.cos(bi), jnp.sin(bi)
    mf_ref[0] = (cbf * tfc[...] - sbf * tfs[...]).astype(BF16)
    mf_ref[1] = (sbf * tfc[...] + cbf * tfs[...]).astype(BF16)
    mi_ref[:, :seq] = (cbi * tic[...] - sbi * tis[...]).astype(BF16)
    mi_ref[:, seq:] = (sbi * tic[...] + cbi * tis[...]).astype(BF16)


def _dft_matrices(seq):
    r = min(DFT_TILE, seq)
    tbl = pltpu.VMEM((r, seq), F32)
    return pl.pallas_call(
        functools.partial(_dft_kernel, seq=seq),
        grid=(seq // r,),
        out_specs=[pl.BlockSpec((2, r, seq), lambda i: (0, i, 0)),
                   pl.BlockSpec((r, 2 * seq), lambda i: (i, 0))],
        out_shape=[jax.ShapeDtypeStruct((2, seq, seq), BF16), jax.ShapeDtypeStruct((seq, 2 * seq), BF16)],
        scratch_shapes=[tbl, tbl, tbl, tbl],
        compiler_params=_cparams(("arbitrary",)),
        name="dft_matrices",
    )()


def _spectrum_kernel(mf_ref, h_ref, g_ref, *, seq, hw):
    hb = h_ref[...]
    hc = _dot(mf_ref[0], hb)
    hs = _dot(mf_ref[1], hb)
    scale = 1.0 / seq
    g_ref[0, 0] = (hc[:, :hw] + hc[:, hw:]) * scale
    g_ref[0, 1] = (hs[:, :hw] - hs[:, hw:]) * scale


def _filter_spectra(mf, hfilt, hw):
    seq = mf.shape[1]
    r = min(DFT_TILE, seq)
    return pl.pallas_call(
        functools.partial(_spectrum_kernel, seq=seq, hw=hw),
        grid=(HYENA_ORDER, seq // r),
        in_specs=[pl.BlockSpec((2, r, seq), lambda n, i: (0, i, 0)),
                  pl.BlockSpec((seq, 2 * hw), lambda n, i: (0, n))],
        out_specs=pl.BlockSpec((1, 2, r, hw), lambda n, i: (n, 0, i, 0)),
        out_shape=jax.ShapeDtypeStruct((HYENA_ORDER, 2, seq, hw), F32),
        compiler_params=_cparams(("parallel", "parallel")),
        name="filter_spectra",
    )(mf, hfilt)


def _short_conv_kernel(u_ref, w_ref, b_ref, o_ref):
    u = u_ref[0]
    s = u.shape[0]
    t = lax.broadcasted_iota(I32, u.shape, 0)
    prev = jnp.where(t == 0, 0.0, pltpu.roll(u, 1, axis=0))
    nxt = jnp.where(t == s - 1, 0.0, pltpu.roll(u, s - 1, axis=0))
    o_ref[0] = b_ref[...] + prev * w_ref[0:1, :] + u * w_ref[1:2, :] + nxt * w_ref[2:3, :]


def _short_conv(u, w, bias):
    b, s, c = u.shape
    tc = min(256, c)
    return pl.pallas_call(
        _short_conv_kernel,
        grid=(b, c // tc),
        in_specs=[pl.BlockSpec((1, s, tc), lambda bi, j: (bi, 0, j)),
                  pl.BlockSpec((SHORT_CONV, tc), lambda bi, j: (0, j)),
                  pl.BlockSpec((1, tc), lambda bi, j: (0, j))],
        out_specs=pl.BlockSpec((1, s, tc), lambda bi, j: (bi, 0, j)),
        out_shape=jax.ShapeDtypeStruct((b, s, c), F32),
        compiler_params=_cparams(("parallel", "parallel")),
        name="short_conv",
    )(u, w, bias.reshape(1, c))


def _fwd_dft_kernel(mf_ref, z_ref, g_ref, y_ref, zb):
    @pl.when(pl.program_id(1) == 0)
    def _():
        zb[...] = z_ref[0].astype(BF16)

    uc = _dot(mf_ref[0], zb[...])
    us = _dot(mf_ref[1], zb[...])
    gc = g_ref[0, 0]
    gs = g_ref[0, 1]
    y_ref[0, 0] = (uc * gc - us * gs).astype(BF16)
    y_ref[0, 1] = (uc * gs + us * gc).astype(BF16)


def _fwd_dft(mf, z, z_col, g, order, hw):
    b, seq = z.shape[0], z.shape[1]
    r = min(DFT_TILE, seq)
    return pl.pallas_call(
        _fwd_dft_kernel,
        grid=(b, seq // r),
        in_specs=[pl.BlockSpec((2, r, seq), lambda bi, i: (0, i, 0)),
                  pl.BlockSpec((1, seq, hw), lambda bi, i: (bi, 0, z_col)),
                  pl.BlockSpec((1, 2, r, hw), lambda bi, i: (order, 0, i, 0))],
        out_specs=pl.BlockSpec((1, 2, r, hw), lambda bi, i: (bi, 0, i, 0)),
        out_shape=jax.ShapeDtypeStruct((b, 2, seq, hw), BF16),
        scratch_shapes=[pltpu.VMEM((seq, hw), BF16)],
        compiler_params=_cparams(("parallel", "arbitrary")),
        name="hyena_fwd_dft",
    )(mf, z, g)


def _inv_dft_kernel(mi_ref, y_ref, z_ref, gate_ref, skip_ref, *rest, final):
    if final:
        og_ref, o_ref = rest
    else:
        (o_ref,) = rest
    conv = _dot(mi_ref[...], y_ref[0])
    z = gate_ref[0] * (conv + z_ref[0] * skip_ref[0])
    if final:
        ms = jnp.mean(z * z, axis=-1, keepdims=True)
        o_ref[0] = (z * lax.rsqrt(ms + EPS) * og_ref[...]).astype(BF16)
    else:
        o_ref[0] = z


def _inv_dft(mi, y, z, z_col, gates, gate_col, skip, order, out_g, hw):
    b, seq = z.shape[0], z.shape[1]
    r = min(DFT_TILE, seq)
    final = out_g is not None
    in_specs = [pl.BlockSpec((r, 2 * seq), lambda bi, i: (i, 0)),
                pl.BlockSpec((1, 2 * seq, hw), lambda bi, i: (bi, 0, 0)),
                pl.BlockSpec((1, r, hw), lambda bi, i: (bi, i, z_col)),
                pl.BlockSpec((1, r, hw), lambda bi, i: (bi, i, gate_col)),
                pl.BlockSpec((1, 1, hw), lambda bi, i: (order, 0, 0))]
    args = [mi, y.reshape(b, 2 * seq, hw), z, gates, skip.reshape(HYENA_ORDER, 1, hw)]
    if final:
        in_specs.append(pl.BlockSpec((1, hw), lambda bi, i: (0, 0)))
        args.append(out_g.reshape(1, hw))
    return pl.pallas_call(
        functools.partial(_inv_dft_kernel, final=final),
        grid=(b, seq // r),
        in_specs=in_specs,
        out_specs=pl.BlockSpec((1, r, hw), lambda bi, i: (bi, i, 0)),
        out_shape=jax.ShapeDtypeStruct((b, seq, hw), BF16 if final else F32),
        compiler_params=_cparams(("parallel", "parallel")),
        name="hyena_inv_dft",
    )(*args)


def _out_kernel(a_ref, hy_ref, x_ref, wo_ref, g1_ref, sh_ref, sc_ref, n2_ref, rw_ref, rb_ref,
                xn_ref, h2_ref, lg_ref):
    aw = a_ref.shape[2]
    mix = _dot(a_ref[0], wo_ref[:aw, :]) + _dot(hy_ref[0], wo_ref[aw:, :])
    xn = x_ref[0] + g1_ref[0] * mix
    xn_ref[0] = xn
    ms = jnp.mean(xn * xn, axis=-1, keepdims=True)
    h2 = (xn * lax.rsqrt(ms + EPS) * n2_ref[...]) * (1.0 + sc_ref[0]) + sh_ref[0]
    _store_row_tiles(h2_ref, h2)
    hh, hl = _split_bf16(h2)
    wh, wl = _split_bf16(rw_ref[...])
    lg_ref[...] = _dot_nt(wh, hh) + (_dot_nt(wh, hl) + _dot_nt(wl, hh)) + rb_ref[...]


def _out_project(attn, hyn, x, wo_bf, g1, sh2, sc2, n2g, rw_t, rb):
    b, s, d = x.shape
    aw, hw = attn.shape[2], hyn.shape[2]
    ne = rw_t.shape[0]
    tm = min(ROW_TILE, s)
    nt = s // tm
    row = lambda bi, i: (bi, i, 0)
    per_b = lambda bi, i: (bi, 0, 0)
    const = lambda bi, i: (0, 0)
    return pl.pallas_call(
        _out_kernel,
        grid=(b, nt),
        in_specs=[pl.BlockSpec((1, tm, aw), row), pl.BlockSpec((1, tm, hw), row), pl.BlockSpec((1, tm, d), row),
                  pl.BlockSpec((aw + hw, d), const),
                  pl.BlockSpec((1, 1, d), per_b), pl.BlockSpec((1, 1, d), per_b), pl.BlockSpec((1, 1, d), per_b),
                  pl.BlockSpec((1, d), const), pl.BlockSpec((ne, d), const), pl.BlockSpec((ne, 1), const)],
        out_specs=[pl.BlockSpec((1, tm, d), row),
                   pl.BlockSpec((tm, d // LANES, LANES), lambda bi, i: (bi * nt + i, 0, 0)),
                   pl.BlockSpec((ne, tm), lambda bi, i: (0, bi * nt + i))],
        out_shape=[jax.ShapeDtypeStruct((b, s, d), F32), jax.ShapeDtypeStruct((b * s, d // LANES, LANES), F32),
                   jax.ShapeDtypeStruct((ne, b * s), F32)],
        compiler_params=_cparams(("parallel", "parallel")),
        name="out_project",
    )(attn, hyn, x, wo_bf, g1, sh2, sc2, n2g, rw_t, rb)


def _route_kernel(lg_ref, idx_ref, gate_ref, rank_ref, cnt_ref, carry):
    ne, tl = lg_ref.shape

    @pl.when(pl.program_id(0) == 0)
    def _():
        carry[...] = jnp.zeros_like(carry)

    l = lg_ref[...]
    rows = lax.broadcasted_iota(I32, (ne, tl), 0).astype(F32)
    vals, sels = [], []
    for k in range(TOP_K):
        m = jnp.max(l, axis=0, keepdims=True)
        ik = jnp.min(jnp.where(l == m, rows, float(ne)), axis=0, keepdims=True)
        sel = rows == ik
        idx_ref[k:k + 1, :] = ik.astype(I32)
        vals.append(m)
        sels.append(sel)
        l = jnp.where(sel, -jnp.inf, l)
    exps = [jnp.exp(v - vals[0]) for v in vals]
    denom = exps[0] + exps[1] + exps[2] + exps[3]
    for k in range(TOP_K):
        gate_ref[k:k + 1, :] = exps[k] / denom
    oh = jnp.zeros((ne, tl), F32)
    for sel in sels:
        oh = oh + jnp.where(sel, 1.0, 0.0)
    r = lax.broadcasted_iota(I32, (tl, tl), 0)
    c = lax.broadcasted_iota(I32, (tl, tl), 1)
    tri = jnp.where(r <= c, 1.0, 0.0).astype(BF16)
    cum = _dot(oh.astype(BF16), tri)
    excl = cum - oh + carry[:, 0:1]
    for k in range(TOP_K):
        rank_ref[k:k + 1, :] = jnp.sum(jnp.where(sels[k], excl, 0.0), axis=0, keepdims=True).astype(I32)
    carry[...] = carry[...] + jnp.sum(oh, axis=1, keepdims=True)
    cnt_ref[...] = carry[...]


def _route(logits_t):
    ne, t = logits_t.shape
    tl = min(ROUTE_TILE, t)
    blk = lambda i: (0, i)
    return pl.pallas_call(
        _route_kernel,
        grid=(t // tl,),
        in_specs=[pl.BlockSpec((ne, tl), blk)],
        out_specs=[pl.BlockSpec((TOP_K, tl), blk), pl.BlockSpec((TOP_K, tl), blk), pl.BlockSpec((TOP_K, tl), blk),
                   pl.BlockSpec((ne, LANES), lambda i: (0, 0))],
        out_shape=[jax.ShapeDtypeStruct((TOP_K, t), I32), jax.ShapeDtypeStruct((TOP_K, t), F32),
                   jax.ShapeDtypeStruct((TOP_K, t), I32), jax.ShapeDtypeStruct((ne, LANES), F32)],
        scratch_shapes=[pltpu.VMEM((ne, LANES), F32)],
        compiler_params=_cparams(("arbitrary",)),
        name="moe_route",
    )(logits_t)


def _slots_kernel(cnt_ref, idx_ref, rank_ref, dest_ref, blk_ref, meta_ref, *, rows_per_block):
    ne = cnt_ref.shape[0]
    t = idx_ref.shape[1]
    shift = _log2(rows_per_block)
    cnt = cnt_ref[...].astype(I32)
    padded = ((cnt + (rows_per_block - 1)) >> shift) << shift
    r = lax.broadcasted_iota(I32, (ne, LANES), 0)
    c = lax.broadcasted_iota(I32, (ne, LANES), 1)
    padded_lane = jnp.sum(jnp.where(r == c, padded, 0), axis=0, keepdims=True)
    cnt_lane = jnp.sum(jnp.where(r == c, cnt, 0), axis=0, keepdims=True)
    pend_lane = jnp.sum(jnp.where(r <= c, padded, 0), axis=0, keepdims=True)
    pend_col = jnp.sum(jnp.where(c <= r, jnp.broadcast_to(padded_lane, (ne, LANES)), 0),
                       axis=1, keepdims=True)
    pstart_col = pend_col - padded[:, 0:1]
    rows_t = lax.broadcasted_iota(I32, (ne, t), 0)
    for k in range(TOP_K):
        sel = rows_t == idx_ref[k:k + 1, :]
        dest_ref[k:k + 1, :] = (jnp.sum(jnp.where(sel, jnp.broadcast_to(pstart_col, (ne, t)), 0),
                                        axis=0, keepdims=True) + rank_ref[k:k + 1, :])
    nbp = blk_ref.shape[1]
    j0 = lax.broadcasted_iota(I32, (ne, nbp), 1) * rows_per_block
    be = jnp.sum(jnp.where(jnp.broadcast_to(pend_col, (ne, nbp)) <= j0, 1, 0), axis=0, keepdims=True)
    blk_ref[...] = jnp.minimum(be, ne - 1)
    total = jnp.max(pend_col, axis=0, keepdims=True)
    meta_ref[0:1, :] = pend_lane - padded_lane + cnt_lane
    meta_ref[1:2, :] = padded_lane - cnt_lane
    meta_ref[2:3, :] = jnp.broadcast_to(total >> shift, (1, LANES))
    meta_ref[3:8, :] = jnp.zeros((5, LANES), I32)


def _slots(cnt, idx, rank, n_blocks, rows_per_block):
    t = idx.shape[1]
    nbp = -(-n_blocks // LANES) * LANES
    return pl.pallas_call(
        functools.partial(_slots_kernel, rows_per_block=rows_per_block),
        out_shape=[jax.ShapeDtypeStruct((TOP_K, t), I32), jax.ShapeDtypeStruct((1, nbp), I32),
                   jax.ShapeDtypeStruct((8, LANES), I32)],
        compiler_params=pltpu.CompilerParams(vmem_limit_bytes=V7X_VMEM_LIMIT),
        name="moe_slots",
    )(cnt, idx, rank)


def _pad_chunks(rows_per_block):
    sizes, s = [], rows_per_block // 2
    while s >= 1:
        sizes.append(s)
        s //= 2
    return sizes


def _dispatch_kernel(padlo_ref, npad_ref, nused_ref, dest_ref, h_ref, buf_ref, zeros, sem, zsem, *, rows_per_block):
    tl = dest_ref.shape[1]
    ne = padlo_ref.shape[0]
    base = pl.program_id(0) * tl
    sizes = _pad_chunks(rows_per_block)
    half = rows_per_block // 2

    def pad_copy(start, size):
        return pltpu.make_async_copy(zeros.at[pl.ds(0, size)], buf_ref.at[pl.ds(start, size)], zsem)

    @pl.when(pl.program_id(0) == 0)
    def _():
        zeros[...] = jnp.zeros_like(zeros)

        first, last = 2 * nused_ref[0], buf_ref.shape[0] // half
        lax.fori_loop(first, last, lambda j, c: (pad_copy(j * half, half).start(), c)[1], 0)
        lax.fori_loop(first, last, lambda j, c: (pad_copy(j * half, half).wait(), c)[1], 0)

        def each(e, wait):
            start = padlo_ref[e]
            left = npad_ref[e]
            for size in sizes:
                hit = (left & size) != 0

                @pl.when(hit)
                def _():
                    cp = pad_copy(start, size)
                    if wait:
                        cp.wait()
                    else:
                        cp.start()

                start = start + jnp.where(hit, size, 0)

        lax.fori_loop(0, ne, lambda e, c: (each(e, False), c)[1], 0)
        lax.fori_loop(0, ne, lambda e, c: (each(e, True), c)[1], 0)

    def row_copy(t, k):
        return pltpu.make_async_copy(h_ref.at[base + t], buf_ref.at[dest_ref[k, t]], sem)

    def issue(t, c):
        for k in range(TOP_K):
            row_copy(t, k).start()
        return c

    lax.fori_loop(0, tl, issue, 0)
    for k in range(TOP_K):
        pltpu.make_async_copy(h_ref.at[pl.ds(0, tl)], buf_ref.at[pl.ds(0, tl)], sem).wait()


def _dispatch(h2r, dest, pad_lo, n_pad, n_used, n_rows, rows_per_block):
    t = h2r.shape[0]
    tl = min(DISPATCH_TILE, t)
    return pl.pallas_call(
        functools.partial(_dispatch_kernel, rows_per_block=rows_per_block),
        grid_spec=pltpu.PrefetchScalarGridSpec(
            num_scalar_prefetch=3,
            grid=(t // tl,),
            in_specs=[pl.BlockSpec((TOP_K, tl), lambda i, *_: (0, i), memory_space=pltpu.SMEM),
                      pl.BlockSpec(memory_space=pl.ANY)],
            out_specs=pl.BlockSpec(memory_space=pl.ANY),
            scratch_shapes=[pltpu.VMEM((rows_per_block // 2,) + h2r.shape[1:], F32),
                            pltpu.SemaphoreType.DMA(()), pltpu.SemaphoreType.DMA(())]),
        out_shape=jax.ShapeDtypeStruct((n_rows,) + h2r.shape[1:], F32),
        compiler_params=_cparams(("arbitrary",)),
        name="moe_dispatch",
    )(pad_lo, n_pad, n_used, dest, h2r)


def _expert_kernel(be_ref, nu_ref, x_ref, w1_ref, b1_ref, w2_ref, b2_ref, o_ref):
    @pl.when(pl.program_id(0) < nu_ref[0])
    def _():
        de = w2_ref.shape[1]
        gl = _dot(_load_row_tiles(x_ref).astype(BF16), w1_ref[0]) + b1_ref[0]
        g = jnp.minimum(gl[:, :de], SWIGLU_LIMIT)
        lin = jnp.clip(gl[:, de:], -SWIGLU_LIMIT, SWIGLU_LIMIT)
        glu = g * jax.nn.sigmoid(SWIGLU_ALPHA * g)
        _store_row_tiles(o_ref, _dot(((lin + 1.0) * glu).astype(BF16), w2_ref[0]) + b2_ref[0])

    @pl.when(pl.program_id(0) >= nu_ref[0])
    def _():
        o_ref[...] = jnp.zeros_like(o_ref)


def _experts(buf, block_e, n_used, w1_bf, b1, w2_bf, b2, rows_per_block):
    n_rows = buf.shape[0]
    ne, d, d2 = w1_bf.shape
    de = w2_bf.shape[1]
    nb = n_rows // rows_per_block
    rowblk = lambda j, be, nu: (jnp.minimum(j, nu[0] - 1), 0, 0)
    by_e = lambda j, be, nu: (be[j], 0, 0)
    blk_shape = (rows_per_block,) + buf.shape[1:]
    return pl.pallas_call(
        _expert_kernel,
        grid_spec=pltpu.PrefetchScalarGridSpec(
            num_scalar_prefetch=2,
            grid=(nb,),
            in_specs=[pl.BlockSpec(blk_shape, rowblk),
                      pl.BlockSpec((1, d, d2), by_e), pl.BlockSpec((1, 1, d2), by_e),
                      pl.BlockSpec((1, de, d), by_e), pl.BlockSpec((1, 1, d), by_e)],
            out_specs=pl.BlockSpec(blk_shape, lambda j, be, nu: (j, 0, 0))),
        out_shape=jax.ShapeDtypeStruct(buf.shape, F32),
        compiler_params=_cparams(("arbitrary",)),
        name="moe_experts",
    )(block_e, n_used, buf, w1_bf, b1.reshape(ne, 1, d2), w2_bf, b2.reshape(ne, 1, d))


def _combine_kernel(dest_ref, gate_ref, xn_ref, g2_ref, ob_ref, o_ref, rows, sem):
    tl = gate_ref.shape[0]

    def row_copy(t, k):
        return pltpu.make_async_copy(ob_ref.at[dest_ref[k, t]], rows.at[k, t], sem)

    def issue(t, c):
        for k in range(TOP_K):
            row_copy(t, k).start()
        return c

    lax.fori_loop(0, tl, issue, 0)
    for k in range(TOP_K):
        pltpu.make_async_copy(ob_ref.at[pl.ds(0, tl)], rows.at[k], sem).wait()
    gates = gate_ref[...]
    ffn = gates[:, 0:1] * _load_row_tiles(rows.at[0])
    for k in range(1, TOP_K):
        ffn = ffn + gates[:, k:k + 1] * _load_row_tiles(rows.at[k])
    o_ref[0] = xn_ref[0] + g2_ref[0] * ffn


def _combine(out_buf, dest, gates_t, xn, g2):
    b, s, d = xn.shape
    tl = min(COMBINE_TILE, s)
    nt = s // tl
    return pl.pallas_call(
        _combine_kernel,
        grid=(b, nt),
        in_specs=[pl.BlockSpec((TOP_K, tl), lambda bi, i: (0, bi * nt + i), memory_space=pltpu.SMEM),
                  pl.BlockSpec((tl, TOP_K), lambda bi, i: (bi * nt + i, 0)),
                  pl.BlockSpec((1, tl, d), lambda bi, i: (bi, i, 0)),
                  pl.BlockSpec((1, 1, d), lambda bi, i: (bi, 0, 0)),
                  pl.BlockSpec(memory_space=pl.ANY)],
        out_specs=pl.BlockSpec((1, tl, d), lambda bi, i: (bi, i, 0)),
        out_shape=jax.ShapeDtypeStruct((b, s, d), F32),
        scratch_shapes=[pltpu.VMEM((TOP_K, tl) + out_buf.shape[1:], F32), pltpu.SemaphoreType.DMA(())],
        compiler_params=_cparams(("arbitrary", "arbitrary")),
        name="moe_combine",
    )(dest, gates_t, xn, g2, out_buf)


def _moe(h2, logits_t, xn, g2, w1, b1, w2, b2):
    b, s, d = xn.shape
    t = b * s
    n_blocks = (t * TOP_K) // EXPERT_ROWS + N_EXPERTS
    n_rows = n_blocks * EXPERT_ROWS
    idx, gates, rank, cnt = _route(logits_t)
    dest, blk, meta = _slots(cnt, idx, rank, n_blocks, EXPERT_ROWS)
    buf = _dispatch(h2, dest, meta[0, :N_EXPERTS], meta[1, :N_EXPERTS], meta[2, :1], n_rows, EXPERT_ROWS)
    out_buf = _experts(buf, blk[0, :n_blocks], meta[2, :1], w1.astype(BF16), b1, w2.astype(BF16), b2, EXPERT_ROWS)
    return _combine(out_buf, dest, gates.T, xn, g2)


def _layer(x, ctx, c, c_ctx, p, lam_init):
    b, s, d = x.shape
    attn_w = d // 2
    hw = d - attn_w
    v_dim = attn_w // N_HEADS
    qk_dim = v_dim // 2
    qk_cols = N_HEADS * 2 * qk_dim
    v_cols = N_HEADS * v_dim
    assert 2 * qk_dim == LANES and v_dim == LANES and s % GRID_W == 0

    rows = -(-(b + 1) // 8) * 8
    cc = jnp.zeros((rows, d), F32).at[:b].set(c).at[b].set(c_ctx)
    mod = _modulation(cc, p['w_mod'], p['b_mod'])
    mod_x = mod[:b].reshape(b, N_MOD, 1, d)
    sh1, sc1, g1, sh2, sc2, g2 = [mod_x[:, i] for i in range(N_MOD)]
    mod_c = mod[b:b + 1].reshape(1, N_MOD, 1, d)
    csh1, csc1 = mod_c[:, 0], mod_c[:, 1]

    w_in_bf = p['w_in'].astype(BF16)
    qg = jnp.tile(p['q_norm_g'], qk_cols // qk_dim).reshape(1, qk_cols)
    kg = jnp.tile(p['k_norm_g'], qk_cols // qk_dim).reshape(1, qk_cols)
    n1g = p['norm1_g'].reshape(1, d)
    cos_t, sin_t = _rope_tables(s, qk_dim)
    q, k, v, u_hy = _project_latent(x, sh1, sc1, n1g, w_in_bf, qg, kg, cos_t, sin_t, qk_cols, v_cols, qk_dim)
    k_c, v_c = _project_context(ctx, csh1, csc1, n1g, w_in_bf[:, qk_cols:2 * qk_cols + v_cols], kg,
                                qk_cols, v_cols, qk_dim)
    k_all = jnp.concatenate([k_c, k], axis=1)
    v_all = jnp.concatenate([v_c, v], axis=1)
    vec = lambda a: a.reshape(1, qk_dim)
    attn = _diff_attention(q, k_all, v_all, vec(p['lam_q1']), vec(p['lam_k1']), vec(p['lam_q2']),
                           vec(p['lam_k2']), p['subln_g'].reshape(1, v_dim), lam_init, qk_dim)

    hfilt = _hyena_filters(s, hw, p['hy_w1'], p['hy_b1'], p['hy_f1'], p['hy_w2'], p['hy_b2'], p['hy_f2'], p['hy_w3'])
    mf, mi = _dft_matrices(s)
    g_spec = _filter_spectra(mf, hfilt, hw)
    uc = _short_conv(u_hy, p['hy_conv_w'], p['hy_conv_b'])
    y1 = _fwd_dft(mf, uc, 0, g_spec, 0, hw)
    z1 = _inv_dft(mi, y1, uc, 0, uc, 1, p['hy_skip'], 0, None, hw)
    y2 = _fwd_dft(mf, z1, 0, g_spec, 1, hw)
    hyn = _inv_dft(mi, y2, z1, 0, uc, 2, p['hy_skip'], 1, p['hy_out_g'], hw)

    xn, h2, logits_t = _out_project(attn, hyn, x, p['w_out'].astype(BF16), g1, sh2, sc2,
                                    p['norm2_g'].reshape(1, d), p['router_w'].T,
                                    p['router_b'].reshape(N_EXPERTS, 1))
    return _moe(h2, logits_t, xn, g2, p['exp_w1'], p['exp_b1'], p['exp_w2'], p['exp_b2'])


def kernel(x, c, ctx, c_ctx, w_mod, b_mod, norm1_g, norm2_g, w_in, q_norm_g, k_norm_g, lam_q1, lam_k1, lam_q2, lam_k2, subln_g, hy_conv_w, hy_conv_b, hy_w1, hy_b1, hy_f1, hy_w2, hy_b2, hy_f2, hy_w3, hy_skip, hy_out_g, w_out, router_w, router_b, exp_w1, exp_b1, exp_w2, exp_b2):
    depth = w_mod.shape[0]
    assert depth == 1, "context-token update between layers is not implemented"
    p = {
        'w_mod': w_mod[0], 'b_mod': b_mod[0], 'norm1_g': norm1_g[0], 'norm2_g': norm2_g[0],
        'w_in': w_in[0], 'q_norm_g': q_norm_g[0], 'k_norm_g': k_norm_g[0],
        'lam_q1': lam_q1[0], 'lam_k1': lam_k1[0], 'lam_q2': lam_q2[0], 'lam_k2': lam_k2[0],
        'subln_g': subln_g[0], 'hy_conv_w': hy_conv_w[0], 'hy_conv_b': hy_conv_b[0],
        'hy_w1': hy_w1[0], 'hy_b1': hy_b1[0], 'hy_f1': hy_f1[0], 'hy_w2': hy_w2[0],
        'hy_b2': hy_b2[0], 'hy_f2': hy_f2[0], 'hy_w3': hy_w3[0], 'hy_skip': hy_skip[0],
        'hy_out_g': hy_out_g[0], 'w_out': w_out[0], 'router_w': router_w[0],
        'router_b': router_b[0], 'exp_w1': exp_w1[0], 'exp_b1': exp_b1[0],
        'exp_w2': exp_w2[0], 'exp_b2': exp_b2[0],
    }
    lam_init = 0.8 - 0.6 * math.exp(-0.3 * 0)
    return _layer(x, ctx, c, c_ctx, p, lam_init)
```

```python
import functools
import math

import jax
import jax.numpy as jnp
from jax import lax
from jax.experimental import pallas as pl
from jax.experimental.pallas import tpu as pltpu

F32 = jnp.float32
BF16 = jnp.bfloat16
I32 = jnp.int32

GRID_W = 64
N_HEADS = 4
N_MOD = 6
SHORT_CONV = 3
HYENA_ORDER = 2
N_BANDS = 8
FEAT_DIM = 1 + 2 * N_BANDS
FILTER_HIDDEN = 64
DECAY_TARGET = 1e-2
FAST_DECAY_PCT = 0.3
SLOW_DECAY_PCT = 1.5
N_EXPERTS = 32
TOP_K = 4
SWIGLU_LIMIT = 7.0
SWIGLU_ALPHA = 1.702
ROPE_BASE = 10000.0
EPS = 1e-6

LANES = 128
V7X_VMEM_LIMIT = 56 * 1024 * 1024

ROW_TILE = 512
ATT_Q_TILE = 256
DFT_TILE = 256
EXPERT_ROWS = 512
ROUTE_TILE = 512
DISPATCH_TILE = 1024
COMBINE_TILE = 128


def _log2(n):
    assert n > 0 and n & (n - 1) == 0, f"{n} must be a power of two"
    return n.bit_length() - 1


def _cparams(sem, vmem=V7X_VMEM_LIMIT):
    return pltpu.CompilerParams(dimension_semantics=sem, vmem_limit_bytes=vmem)


def _split_bf16(a):
    hi = a.astype(BF16)
    lo = (a - hi.astype(F32)).astype(BF16)
    return hi, lo


def _dot(a, b):
    return jnp.dot(a, b, preferred_element_type=F32)


def _dot_nt(a, b):
    return lax.dot_general(a, b, (((1,), (1,)), ((), ())), preferred_element_type=F32)


def _store_row_tiles(ref, val):
    for c in range(ref.shape[1]):
        ref[:, c, :] = val[:, c * LANES:(c + 1) * LANES]


def _load_row_tiles(ref):
    return jnp.concatenate([ref[:, c, :] for c in range(ref.shape[1])], axis=1)


def _dot3(a, b):
    ah, al = _split_bf16(a)
    bh, bl = _split_bf16(b)
    return _dot(ah, bh) + (_dot(ah, bl) + _dot(al, bh))


def _mod_kernel(c_ref, w_ref, b_ref, o_ref):
    c = c_ref[...]
    s = c * jax.nn.sigmoid(c)
    o_ref[...] = _dot3(s, w_ref[...]) + b_ref[...]


def _modulation(cc, w_mod, b_mod):
    rows, d = cc.shape
    n = w_mod.shape[1]
    tn = min(n, 1536)
    return pl.pallas_call(
        _mod_kernel,
        grid=(n // tn,),
        in_specs=[pl.BlockSpec((rows, d), lambda j: (0, 0)),
                  pl.BlockSpec((d, tn), lambda j: (0, j)),
                  pl.BlockSpec((1, tn), lambda j: (0, j))],
        out_specs=pl.BlockSpec((rows, tn), lambda j: (0, j)),
        out_shape=jax.ShapeDtypeStruct((rows, n), F32),
        compiler_params=_cparams(("parallel",)),
        name="modulation",
    )(cc, w_mod, b_mod.reshape(1, n))


def _rope_table_kernel(cos_ref, sin_ref, *, qk_dim):
    s, w = cos_ref.shape
    half = qk_dim // 2
    nf = half // 2
    t = lax.broadcasted_iota(I32, (s, w), 0)
    lane = lax.broadcasted_iota(I32, (s, w), 1)
    d = lane & (qk_dim - 1)
    j = d & (nf - 1)
    row = t >> _log2(GRID_W)
    col = t & (GRID_W - 1)
    pos = jnp.where(d < half, row, col).astype(F32)
    inv = jnp.exp(j.astype(F32) * (-math.log(ROPE_BASE) / nf))
    ang = pos * inv
    first = (d & (half - 1)) < nf
    cos_ref[...] = jnp.cos(ang)
    sn = jnp.sin(ang)
    sin_ref[...] = jnp.where(first, -sn, sn)


def _rope_tables(s, qk_dim):
    return pl.pallas_call(
        functools.partial(_rope_table_kernel, qk_dim=qk_dim),
        out_shape=(jax.ShapeDtypeStruct((s, LANES), F32), jax.ShapeDtypeStruct((s, LANES), F32)),
        name="rope_tables",
    )()


def _group_rms(t, gain, qk_dim):
    w = t.shape[1]
    r = lax.broadcasted_iota(I32, (w, w), 0) >> _log2(qk_dim)
    c = lax.broadcasted_iota(I32, (w, w), 1) >> _log2(qk_dim)
    bd = jnp.where(r == c, 1.0 / qk_dim, 0.0).astype(BF16)
    hi, lo = _split_bf16(t * t)
    ms = _dot(hi, bd) + _dot(lo, bd)
    return t * lax.rsqrt(ms + EPS) * gain


def _rope(t, cos, sin_signed, qk_dim):
    w = t.shape[1]
    nf = qk_dim // 4
    lane = lax.broadcasted_iota(I32, t.shape, 1)
    first = (lane & (2 * nf - 1)) < nf
    partner = jnp.where(first, pltpu.roll(t, w - nf, axis=1), pltpu.roll(t, nf, axis=1))
    return t * cos + partner * sin_signed


def _proj_kernel(*refs, latent, qk_cols, v_cols, qk_dim):
    if latent:
        (x_ref, sh_ref, sc_ref, g_ref, w_ref, qg_ref, kg_ref, cos_ref, sin_ref,
         q_out, k_out, v_out, u_out) = refs
    else:
        x_ref, sh_ref, sc_ref, g_ref, w_ref, kg_ref, k_out, v_out = refs
    x = x_ref[0]
    ms = jnp.mean(x * x, axis=-1, keepdims=True)
    h = (x * lax.rsqrt(ms + EPS) * g_ref[...]) * (1.0 + sc_ref[0]) + sh_ref[0]
    proj = _dot(h.astype(BF16), w_ref[...])
    if latent:
        reps = qk_cols // LANES
        cos = jnp.concatenate([cos_ref[...]] * reps, axis=1)
        sin = jnp.concatenate([sin_ref[...]] * reps, axis=1)
        q = _rope(_group_rms(proj[:, :qk_cols], qg_ref[...], qk_dim), cos, sin, qk_dim)
        q_out[0] = (q * (qk_dim ** -0.5)).astype(BF16)
        k = _rope(_group_rms(proj[:, qk_cols:2 * qk_cols], kg_ref[...], qk_dim), cos, sin, qk_dim)
        k_out[0] = k.astype(BF16)
        v_out[0] = proj[:, 2 * qk_cols:2 * qk_cols + v_cols].astype(BF16)
        u_out[0] = proj[:, 2 * qk_cols + v_cols:]
    else:
        k = _group_rms(proj[:, :qk_cols], kg_ref[...], qk_dim)
        k_out[0] = k.astype(BF16)
        v_out[0] = proj[:, qk_cols:qk_cols + v_cols].astype(BF16)


def _project_latent(x, sh, sc, g, w_bf, qg, kg, cos_t, sin_t, qk_cols, v_cols, qk_dim):
    b, s, d = x.shape
    n = w_bf.shape[1]
    hy_cols = n - 2 * qk_cols - v_cols
    tm = min(ROW_TILE, s)
    row = lambda bi, i: (bi, i, 0)
    per_b = lambda bi, i: (bi, 0, 0)
    const = lambda bi, i: (0, 0)
    return pl.pallas_call(
        functools.partial(_proj_kernel, latent=True, qk_cols=qk_cols, v_cols=v_cols, qk_dim=qk_dim),
        grid=(b, s // tm),
        in_specs=[pl.BlockSpec((1, tm, d), row),
                  pl.BlockSpec((1, 1, d), per_b), pl.BlockSpec((1, 1, d), per_b),
                  pl.BlockSpec((1, d), const), pl.BlockSpec((d, n), const),
                  pl.BlockSpec((1, qk_cols), const), pl.BlockSpec((1, qk_cols), const),
                  pl.BlockSpec((tm, LANES), lambda bi, i: (i, 0)),
                  pl.BlockSpec((tm, LANES), lambda bi, i: (i, 0))],
        out_specs=[pl.BlockSpec((1, tm, qk_cols), row), pl.BlockSpec((1, tm, qk_cols), row),
                   pl.BlockSpec((1, tm, v_cols), row), pl.BlockSpec((1, tm, hy_cols), row)],
        out_shape=[jax.ShapeDtypeStruct((b, s, qk_cols), BF16), jax.ShapeDtypeStruct((b, s, qk_cols), BF16),
                   jax.ShapeDtypeStruct((b, s, v_cols), BF16), jax.ShapeDtypeStruct((b, s, hy_cols), F32)],
        compiler_params=_cparams(("parallel", "parallel")),
        name="project_latent",
    )(x, sh, sc, g, w_bf, qg, kg, cos_t, sin_t)


def _project_context(ctx, sh, sc, g, w_bf, kg, qk_cols, v_cols, qk_dim):
    b, lc, d = ctx.shape
    n = w_bf.shape[1]
    tm = min(ROW_TILE, lc)
    row = lambda bi, i: (bi, i, 0)
    shared = lambda bi, i: (0, 0, 0)
    const = lambda bi, i: (0, 0)
    return pl.pallas_call(
        functools.partial(_proj_kernel, latent=False, qk_cols=qk_cols, v_cols=v_cols, qk_dim=qk_dim),
        grid=(b, lc // tm),
        in_specs=[pl.BlockSpec((1, tm, d), row),
                  pl.BlockSpec((1, 1, d), shared), pl.BlockSpec((1, 1, d), shared),
                  pl.BlockSpec((1, d), const), pl.BlockSpec((d, n), const),
                  pl.BlockSpec((1, qk_cols), const)],
        out_specs=[pl.BlockSpec((1, tm, qk_cols), row), pl.BlockSpec((1, tm, v_cols), row)],
        out_shape=[jax.ShapeDtypeStruct((b, lc, qk_cols), BF16), jax.ShapeDtypeStruct((b, lc, v_cols), BF16)],
        compiler_params=_cparams(("parallel", "parallel")),
        name="project_context",
    )(ctx, sh, sc, g, w_bf, kg)


def _attn_kernel(q_ref, k_ref, v_ref, lq1, lk1, lq2, lk2, sg_ref, o_ref, *, lam_init, qk_dim):
    lam = (jnp.exp(jnp.sum(lq1[...] * lk1[...], axis=-1, keepdims=True))
           - jnp.exp(jnp.sum(lq2[...] * lk2[...], axis=-1, keepdims=True)) + lam_init)
    q = q_ref[0]
    k = k_ref[0]
    lane = lax.broadcasted_iota(I32, q.shape, 1)
    zero = jnp.zeros_like(q)

    def probs(qm):
        s = _dot_nt(qm, k)
        e = jnp.exp(s - jnp.max(s, axis=-1, keepdims=True))
        return e, jnp.sum(e, axis=-1, keepdims=True)

    e1, l1 = probs(jnp.where(lane < qk_dim, q, zero))
    e2, l2 = probs(jnp.where(lane >= qk_dim, q, zero))
    a = e1 * (1.0 / l1) - e2 * (lam / l2)
    o = _dot(a.astype(BF16), v_ref[0])
    ms = jnp.mean(o * o, axis=-1, keepdims=True)
    o_ref[0] = ((o * lax.rsqrt(ms + EPS) * sg_ref[...]) * (1.0 - lam_init)).astype(BF16)


def _diff_attention(q, k_all, v_all, lq1, lk1, lq2, lk2, subln_g, lam_init, qk_dim):
    b, s, w = q.shape
    kk = k_all.shape[1]
    tq = min(ATT_Q_TILE, s)
    hb = lambda bi, h, i: (bi, i, h)
    kv = lambda bi, h, i: (bi, 0, h)
    const = lambda bi, h, i: (0, 0)
    vec = pl.BlockSpec((1, qk_dim), const)
    return pl.pallas_call(
        functools.partial(_attn_kernel, lam_init=lam_init, qk_dim=qk_dim),
        grid=(b, N_HEADS, s // tq),
        in_specs=[pl.BlockSpec((1, tq, LANES), hb), pl.BlockSpec((1, kk, LANES), kv),
                  pl.BlockSpec((1, kk, LANES), kv), vec, vec, vec, vec,
                  pl.BlockSpec((1, LANES), const)],
        out_specs=pl.BlockSpec((1, tq, LANES), hb),
        out_shape=jax.ShapeDtypeStruct((b, s, w), BF16),
        compiler_params=_cparams(("parallel", "parallel", "parallel")),
        name="diff_attention",
    )(q, k_all, v_all, lq1, lk1, lq2, lk2, subln_g)


def _filter_kernel(w1_ref, b1_ref, f1_ref, w2_ref, b2_ref, f2_ref, w3_ref, o_ref, *, seq, hw):
    tl, n = o_ref.shape
    base = pl.program_id(0) * tl
    pos = (lax.broadcasted_iota(I32, (tl, LANES), 0) + base).astype(F32)
    lane = lax.broadcasted_iota(I32, (tl, LANES), 1)
    tn = pos / seq
    band_idx = jnp.where(lane <= N_BANDS, lane - 1, lane - 1 - N_BANDS).astype(F32)
    band = 1e-4 + band_idx * ((N_BANDS - 1 - 1e-4) / (N_BANDS - 1))
    ang = (2.0 * math.pi / seq) * pos * band
    feats = jnp.where(lane == 0, tn,
                      jnp.where(lane <= N_BANDS, jnp.sin(ang),
                                jnp.where(lane < FEAT_DIM, jnp.cos(ang), 0.0)))
    h = jnp.sin(f1_ref[...] * (_dot3(feats, w1_ref[...]) + b1_ref[...]))
    h = jnp.sin(f2_ref[...] * (_dot3(h, w2_ref[...]) + b2_ref[...]))
    h = _dot3(h, w3_ref[...])
    ch = (lax.broadcasted_iota(I32, (tl, n), 1) & ((1 << _log2(hw)) - 1)).astype(F32)
    lo = abs(math.log(DECAY_TARGET) / SLOW_DECAY_PCT)
    hi = abs(math.log(DECAY_TARGET) / FAST_DECAY_PCT)
    delta = lo + ch * ((hi - lo) / (hw - 1))
    tn_n = (lax.broadcasted_iota(I32, (tl, n), 0) + base).astype(F32) / seq
    o_ref[...] = (h * jnp.exp(-tn_n * delta)).astype(BF16)


def _hyena_filters(seq, hw, w1, b1, f1, w2, b2, f2, w3):
    fh = w2.shape[0]
    n = w3.shape[1]
    w1p = jnp.zeros((LANES, fh), F32).at[:FEAT_DIM].set(w1)
    tl = min(ROW_TILE, seq)
    const = lambda i: (0, 0)
    return pl.pallas_call(
        functools.partial(_filter_kernel, seq=seq, hw=hw),
        grid=(seq // tl,),
        in_specs=[pl.BlockSpec((LANES, fh), const), pl.BlockSpec((1, fh), const), pl.BlockSpec((1, fh), const),
                  pl.BlockSpec((fh, fh), const), pl.BlockSpec((1, fh), const), pl.BlockSpec((1, fh), const),
                  pl.BlockSpec((fh, n), const)],
        out_specs=pl.BlockSpec((tl, n), lambda i: (i, 0)),
        out_shape=jax.ShapeDtypeStruct((seq, n), BF16),
        compiler_params=_cparams(("parallel",)),
        name="hyena_filters",
    )(w1p, b1.reshape(1, fh), f1.reshape(1, fh), w2, b2.reshape(1, fh), f2.reshape(1, fh), w3)


def _dft_kernel(mf_ref, mi_ref, tfc, tfs, tic, tis, *, seq):
    r = tfc.shape[0]
    mask = (1 << _log2(4 * seq)) - 1
    unit = math.pi / (2 * seq)
    i_row = lax.broadcasted_iota(I32, (r, seq), 0)
    col = lax.broadcasted_iota(I32, (r, seq), 1)

    @pl.when(pl.program_id(0) == 0)
    def _():
        af = ((2 * i_row * col) & mask).astype(F32) * unit
        tfc[...] = jnp.cos(af)
        tfs[...] = jnp.sin(af)
        ai = (((2 * col + 1) * i_row) & mask).astype(F32) * unit
        tic[...] = jnp.cos(ai)
        tis[...] = jnp.sin(ai)

    r0 = pl.program_id(0) * r
    c1 = lax.broadcasted_iota(I32, (1, seq), 1)
    bf = (((2 * r0 + 1) * c1) & mask).astype(F32) * unit
    bi = (((2 * c1 + 1) * r0) & mask).astype(F32) * unit
    cbf, sbf = jnp.cos(bf), jnp.sin(bf)
    cbi, sbi = jnp.cos(bi), jnp.sin(bi)
    mf_ref[0] = (cbf * tfc[...] - sbf * tfs[...]).astype(BF16)
    mf_ref[1] = (sbf * tfc[...] + cbf * tfs[...]).astype(BF16)
    mi_ref[:, :seq] = (cbi * tic[...] - sbi * tis[...]).astype(BF16)
    mi_ref[:, seq:] = (sbi * tic[...] + cbi * tis[...]).astype(BF16)


def _dft_matrices(seq):
    r = min(DFT_TILE, seq)
    tbl = pltpu.VMEM((r, seq), F32)
    return pl.pallas_call(
        functools.partial(_dft_kernel, seq=seq),
        grid=(seq // r,),
        out_specs=[pl.BlockSpec((2, r, seq), lambda i: (0, i, 0)),
                   pl.BlockSpec((r, 2 * seq), lambda i: (i, 0))],
        out_shape=[jax.ShapeDtypeStruct((2, seq, seq), BF16), jax.ShapeDtypeStruct((seq, 2 * seq), BF16)],
        scratch_shapes=[tbl, tbl, tbl, tbl],
        compiler_params=_cparams(("arbitrary",)),
        name="dft_matrices",
    )()


def _spectrum_kernel(mf_ref, h_ref, g_ref, *, seq, hw):
    hb = h_ref[...]
    hc = _dot(mf_ref[0], hb)
    hs = _dot(mf_ref[1], hb)
    scale = 1.0 / seq
    g_ref[0, 0] = (hc[:, :hw] + hc[:, hw:]) * scale
    g_ref[0, 1] = (hs[:, :hw] - hs[:, hw:]) * scale


def _filter_spectra(mf, hfilt, hw):
    seq = mf.shape[1]
    r = min(DFT_TILE, seq)
    return pl.pallas_call(
        functools.partial(_spectrum_kernel, seq=seq, hw=hw),
        grid=(HYENA_ORDER, seq // r),
        in_specs=[pl.BlockSpec((2, r, seq), lambda n, i: (0, i, 0)),
                  pl.BlockSpec((seq, 2 * hw), lambda n, i: (0, n))],
        out_specs=pl.BlockSpec((1, 2, r, hw), lambda n, i: (n, 0, i, 0)),
        out_shape=jax.ShapeDtypeStruct((HYENA_ORDER, 2, seq, hw), F32),
        compiler_params=_cparams(("parallel", "parallel")),
        name="filter_spectra",
    )(mf, hfilt)


def _short_conv_kernel(u_ref, w_ref, b_ref, o_ref):
    u = u_ref[0]
    s = u.shape[0]
    t = lax.broadcasted_iota(I32, u.shape, 0)
    prev = jnp.where(t == 0, 0.0, pltpu.roll(u, 1, axis=0))
    nxt = jnp.where(t == s - 1, 0.0, pltpu.roll(u, s - 1, axis=0))
    o_ref[0] = b_ref[...] + prev * w_ref[0:1, :] + u * w_ref[1:2, :] + nxt * w_ref[2:3, :]


def _short_conv(u, w, bias):
    b, s, c = u.shape
    tc = min(256, c)
    return pl.pallas_call(
        _short_conv_kernel,
        grid=(b, c // tc),
        in_specs=[pl.BlockSpec((1, s, tc), lambda bi, j: (bi, 0, j)),
                  pl.BlockSpec((SHORT_CONV, tc), lambda bi, j: (0, j)),
                  pl.BlockSpec((1, tc), lambda bi, j: (0, j))],
        out_specs=pl.BlockSpec((1, s, tc), lambda bi, j: (bi, 0, j)),
        out_shape=jax.ShapeDtypeStruct((b, s, c), F32),
        compiler_params=_cparams(("parallel", "parallel")),
        name="short_conv",
    )(u, w, bias.reshape(1, c))


def _fwd_dft_kernel(mf_ref, z_ref, g_ref, y_ref, zb):
    @pl.when(pl.program_id(1) == 0)
    def _():
        zb[...] = z_ref[0].astype(BF16)

    uc = _dot(mf_ref[0], zb[...])
    us = _dot(mf_ref[1], zb[...])
    gc = g_ref[0, 0]
    gs = g_ref[0, 1]
    y_ref[0, 0] = (uc * gc - us * gs).astype(BF16)
    y_ref[0, 1] = (uc * gs + us * gc).astype(BF16)


def _fwd_dft(mf, z, z_col, g, order, hw):
    b, seq = z.shape[0], z.shape[1]
    r = min(DFT_TILE, seq)
    return pl.pallas_call(
        _fwd_dft_kernel,
        grid=(b, seq // r),
        in_specs=[pl.BlockSpec((2, r, seq), lambda bi, i: (0, i, 0)),
                  pl.BlockSpec((1, seq, hw), lambda bi, i: (bi, 0, z_col)),
                  pl.BlockSpec((1, 2, r, hw), lambda bi, i: (order, 0, i, 0))],
        out_specs=pl.BlockSpec((1, 2, r, hw), lambda bi, i: (bi, 0, i, 0)),
        out_shape=jax.ShapeDtypeStruct((b, 2, seq, hw), BF16),
        scratch_shapes=[pltpu.VMEM((seq, hw), BF16)],
        compiler_params=_cparams(("parallel", "arbitrary")),
        name="hyena_fwd_dft",
    )(mf, z, g)


def _inv_dft_kernel(mi_ref, y_ref, z_ref, gate_ref, skip_ref, *rest, final):
    if final:
        og_ref, o_ref = rest
    else:
        (o_ref,) = rest
    conv = _dot(mi_ref[...], y_ref[0])
    z = gate_ref[0] * (conv + z_ref[0] * skip_ref[0])
    if final:
        ms = jnp.mean(z * z, axis=-1, keepdims=True)
        o_ref[0] = (z * lax.rsqrt(ms + EPS) * og_ref[...]).astype(BF16)
    else:
        o_ref[0] = z


def _inv_dft(mi, y, z, z_col, gates, gate_col, skip, order, out_g, hw):
    b, seq = z.shape[0], z.shape[1]
    r = min(DFT_TILE, seq)
    final = out_g is not None
    in_specs = [pl.BlockSpec((r, 2 * seq), lambda bi, i: (i, 0)),
                pl.BlockSpec((1, 2 * seq, hw), lambda bi, i: (bi, 0, 0)),
                pl.BlockSpec((1, r, hw), lambda bi, i: (bi, i, z_col)),
                pl.BlockSpec((1, r, hw), lambda bi, i: (bi, i, gate_col)),
                pl.BlockSpec((1, 1, hw), lambda bi, i: (order, 0, 0))]
    args = [mi, y.reshape(b, 2 * seq, hw), z, gates, skip.reshape(HYENA_ORDER, 1, hw)]
    if final:
        in_specs.append(pl.BlockSpec((1, hw), lambda bi, i: (0, 0)))
        args.append(out_g.reshape(1, hw))
    return pl.pallas_call(
        functools.partial(_inv_dft_kernel, final=final),
        grid=(b, seq // r),
        in_specs=in_specs,
        out_specs=pl.BlockSpec((1, r, hw), lambda bi, i: (bi, i, 0)),
        out_shape=jax.ShapeDtypeStruct((b, seq, hw), BF16 if final else F32),
        compiler_params=_cparams(("parallel", "parallel")),
        name="hyena_inv_dft",
    )(*args)


def _out_kernel(a_ref, hy_ref, x_ref, wo_ref, g1_ref, sh_ref, sc_ref, n2_ref, rw_ref, rb_ref,
                xn_ref, h2_ref, lg_ref):
    aw = a_ref.shape[2]
    mix = _dot(a_ref[0], wo_ref[:aw, :]) + _dot(hy_ref[0], wo_ref[aw:, :])
    xn = x_ref[0] + g1_ref[0] * mix
    xn_ref[0] = xn
    ms = jnp.mean(xn * xn, axis=-1, keepdims=True)
    h2 = (xn * lax.rsqrt(ms + EPS) * n2_ref[...]) * (1.0 + sc_ref[0]) + sh_ref[0]
    _store_row_tiles(h2_ref, h2)
    hh, hl = _split_bf16(h2)
    wh, wl = _split_bf16(rw_ref[...])
    lg_ref[...] = _dot_nt(wh, hh) + (_dot_nt(wh, hl) + _dot_nt(wl, hh)) + rb_ref[...]


def _out_project(attn, hyn, x, wo_bf, g1, sh2, sc2, n2g, rw_t, rb):
    b, s, d = x.shape
    aw, hw = attn.shape[2], hyn.shape[2]
    ne = rw_t.shape[0]
    tm = min(ROW_TILE, s)
    nt = s // tm
    row = lambda bi, i: (bi, i, 0)
    per_b = lambda bi, i: (bi, 0, 0)
    const = lambda bi, i: (0, 0)
    return pl.pallas_call(
        _out_kernel,
        grid=(b, nt),
        in_specs=[pl.BlockSpec((1, tm, aw), row), pl.BlockSpec((1, tm, hw), row), pl.BlockSpec((1, tm, d), row),
                  pl.BlockSpec((aw + hw, d), const),
                  pl.BlockSpec((1, 1, d), per_b), pl.BlockSpec((1, 1, d), per_b), pl.BlockSpec((1, 1, d), per_b),
                  pl.BlockSpec((1, d), const), pl.BlockSpec((ne, d), const), pl.BlockSpec((ne, 1), const)],
        out_specs=[pl.BlockSpec((1, tm, d), row),
                   pl.BlockSpec((tm, d // LANES, LANES), lambda bi, i: (bi * nt + i, 0, 0)),
                   pl.BlockSpec((ne, tm), lambda bi, i: (0, bi * nt + i))],
        out_shape=[jax.ShapeDtypeStruct((b, s, d), F32), jax.ShapeDtypeStruct((b * s, d // LANES, LANES), F32),
                   jax.ShapeDtypeStruct((ne, b * s), F32)],
        compiler_params=_cparams(("parallel", "parallel")),
        name="out_project",
    )(attn, hyn, x, wo_bf, g1, sh2, sc2, n2g, rw_t, rb)


def _route_kernel(lg_ref, idx_ref, gate_ref, rank_ref, cnt_ref, carry):
    ne, tl = lg_ref.shape

    @pl.when(pl.program_id(0) == 0)
    def _():
        carry[...] = jnp.zeros_like(carry)

    l = lg_ref[...]
    rows = lax.broadcasted_iota(I32, (ne, tl), 0).astype(F32)
    vals, sels = [], []
    for k in range(TOP_K):
        m = jnp.max(l, axis=0, keepdims=True)
        ik = jnp.min(jnp.where(l == m, rows, float(ne)), axis=0, keepdims=True)
        sel = rows == ik
        idx_ref[k:k + 1, :] = ik.astype(I32)
        vals.append(m)
        sels.append(sel)
        l = jnp.where(sel, -jnp.inf, l)
    exps = [jnp.exp(v - vals[0]) for v in vals]
    denom = exps[0] + exps[1] + exps[2] + exps[3]
    for k in range(TOP_K):
        gate_ref[k:k + 1, :] = exps[k] / denom
    oh = jnp.zeros((ne, tl), F32)
    for sel in sels:
        oh = oh + jnp.where(sel, 1.0, 0.0)
    r = lax.broadcasted_iota(I32, (tl, tl), 0)
    c = lax.broadcasted_iota(I32, (tl, tl), 1)
    tri = jnp.where(r <= c, 1.0, 0.0).astype(BF16)
    cum = _dot(oh.astype(BF16), tri)
    excl = cum - oh + carry[:, 0:1]
    for k in range(TOP_K):
        rank_ref[k:k + 1, :] = jnp.sum(jnp.where(sels[k], excl, 0.0), axis=0, keepdims=True).astype(I32)
    carry[...] = carry[...] + jnp.sum(oh, axis=1, keepdims=True)
    cnt_ref[...] = carry[...]


def _route(logits_t):
    ne, t = logits_t.shape
    tl = min(ROUTE_TILE, t)
    blk = lambda i: (0, i)
    return pl.pallas_call(
        _route_kernel,
        grid=(t // tl,),
        in_specs=[pl.BlockSpec((ne, tl), blk)],
        out_specs=[pl.BlockSpec((TOP_K, tl), blk), pl.BlockSpec((TOP_K, tl), blk), pl.BlockSpec((TOP_K, tl), blk),
                   pl.BlockSpec((ne, LANES), lambda i: (0, 0))],
        out_shape=[jax.ShapeDtypeStruct((TOP_K, t), I32), jax.ShapeDtypeStruct((TOP_K, t), F32),
                   jax.ShapeDtypeStruct((TOP_K, t), I32), jax.ShapeDtypeStruct((ne, LANES), F32)],
        scratch_shapes=[pltpu.VMEM((ne, LANES), F32)],
        compiler_params=_cparams(("arbitrary",)),
        name="moe_route",
    )(logits_t)


def _slots_kernel(cnt_ref, idx_ref, rank_ref, dest_ref, blk_ref, meta_ref, *, rows_per_block):
    ne = cnt_ref.shape[0]
    t = idx_ref.shape[1]
    shift = _log2(rows_per_block)
    cnt = cnt_ref[...].astype(I32)
    padded = ((cnt + (rows_per_block - 1)) >> shift) << shift
    r = lax.broadcasted_iota(I32, (ne, LANES), 0)
    c = lax.broadcasted_iota(I32, (ne, LANES), 1)
    padded_lane = jnp.sum(jnp.where(r == c, padded, 0), axis=0, keepdims=True)
    cnt_lane = jnp.sum(jnp.where(r == c, cnt, 0), axis=0, keepdims=True)
    pend_lane = jnp.sum(jnp.where(r <= c, padded, 0), axis=0, keepdims=True)
    pend_col = jnp.sum(jnp.where(c <= r, jnp.broadcast_to(padded_lane, (ne, LANES)), 0),
                       axis=1, keepdims=True)
    pstart_col = pend_col - padded[:, 0:1]
    rows_t = lax.broadcasted_iota(I32, (ne, t), 0)
    for k in range(TOP_K):
        sel = rows_t == idx_ref[k:k + 1, :]
        dest_ref[k:k + 1, :] = (jnp.sum(jnp.where(sel, jnp.broadcast_to(pstart_col, (ne, t)), 0),
                                        axis=0, keepdims=True) + rank_ref[k:k + 1, :])
    nbp = blk_ref.shape[1]
    j0 = lax.broadcasted_iota(I32, (ne, nbp), 1) * rows_per_block
    be = jnp.sum(jnp.where(jnp.broadcast_to(pend_col, (ne, nbp)) <= j0, 1, 0), axis=0, keepdims=True)
    blk_ref[...] = jnp.minimum(be, ne - 1)
    total = jnp.max(pend_col, axis=0, keepdims=True)
    meta_ref[0:1, :] = pend_lane - padded_lane + cnt_lane
    meta_ref[1:2, :] = padded_lane - cnt_lane
    meta_ref[2:3, :] = jnp.broadcast_to(total >> shift, (1, LANES))
    meta_ref[3:8, :] = jnp.zeros((5, LANES), I32)


def _slots(cnt, idx, rank, n_blocks, rows_per_block):
    t = idx.shape[1]
    nbp = -(-n_blocks // LANES) * LANES
    return pl.pallas_call(
        functools.partial(_slots_kernel, rows_per_block=rows_per_block),
        out_shape=[jax.ShapeDtypeStruct((TOP_K, t), I32), jax.ShapeDtypeStruct((1, nbp), I32),
                   jax.ShapeDtypeStruct((8, LANES), I32)],
        compiler_params=pltpu.CompilerParams(vmem_limit_bytes=V7X_VMEM_LIMIT),
        name="moe_slots",
    )(cnt, idx, rank)


def _pad_chunks(rows_per_block):
    sizes, s = [], rows_per_block // 2
    while s >= 1:
        sizes.append(s)
        s //= 2
    return sizes


def _dispatch_kernel(padlo_ref, npad_ref, nused_ref, dest_ref, h_ref, buf_ref, zeros, sem, zsem, *, rows_per_block):
    tl = dest_ref.shape[1]
    ne = padlo_ref.shape[0]
    sizes = _pad_chunks(rows_per_block)
    half = rows_per_block // 2

    def pad_copy(start, size):
        return pltpu.make_async_copy(zeros.at[pl.ds(0, size)], buf_ref.at[pl.ds(start, size)], zsem)

    @pl.when(pl.program_id(0) == 0)
    def _():
        zeros[...] = jnp.zeros_like(zeros)

        first, last = 2 * nused_ref[0], buf_ref.shape[0] // half
        lax.fori_loop(first, last, lambda j, c: (pad_copy(j * half, half).start(), c)[1], 0)
        lax.fori_loop(first, last, lambda j, c: (pad_copy(j * half, half).wait(), c)[1], 0)

        def each(e, wait):
            start = padlo_ref[e]
            left = npad_ref[e]
            for size in sizes:
                hit = (left & size) != 0

                @pl.when(hit)
                def _():
                    cp = pad_copy(start, size)
                    if wait:
                        cp.wait()
                    else:
                        cp.start()

                start = start + jnp.where(hit, size, 0)

        lax.fori_loop(0, ne, lambda e, c: (each(e, False), c)[1], 0)
        lax.fori_loop(0, ne, lambda e, c: (each(e, True), c)[1], 0)

    def row_copy(t, k):
        return pltpu.make_async_copy(h_ref.at[t], buf_ref.at[dest_ref[k, t]], sem)

    def issue(t, c):
        for k in range(TOP_K):
            row_copy(t, k).start()
        return c

    lax.fori_loop(0, tl, issue, 0)
    for k in range(TOP_K):
        pltpu.make_async_copy(h_ref.at[pl.ds(0, tl)], buf_ref.at[pl.ds(0, tl)], sem).wait()


def _dispatch(h2r, dest, pad_lo, n_pad, n_used, n_rows, rows_per_block):
    t = h2r.shape[0]
    tl = min(DISPATCH_TILE, t)
    return pl.pallas_call(
        functools.partial(_dispatch_kernel, rows_per_block=rows_per_block),
        grid_spec=pltpu.PrefetchScalarGridSpec(
            num_scalar_prefetch=3,
            grid=(t // tl,),
            in_specs=[pl.BlockSpec((TOP_K, tl), lambda i, *_: (0, i), memory_space=pltpu.SMEM),
                      pl.BlockSpec((tl,) + h2r.shape[1:], lambda i, *_: (i, 0, 0))],
            out_specs=pl.BlockSpec(memory_space=pl.ANY),
            scratch_shapes=[pltpu.VMEM((rows_per_block // 2,) + h2r.shape[1:], F32),
                            pltpu.SemaphoreType.DMA(()), pltpu.SemaphoreType.DMA(())]),
        out_shape=jax.ShapeDtypeStruct((n_rows,) + h2r.shape[1:], F32),
        compiler_params=_cparams(("arbitrary",)),
        name="moe_dispatch",
    )(pad_lo, n_pad, n_used, dest, h2r)


def _expert_kernel(be_ref, nu_ref, x_ref, w1_ref, b1_ref, w2_ref, b2_ref, o_ref):
    @pl.when(pl.program_id(0) < nu_ref[0])
    def _():
        de = w2_ref.shape[1]
        gl = _dot(_load_row_tiles(x_ref).astype(BF16), w1_ref[0]) + b1_ref[0]
        g = jnp.minimum(gl[:, :de], SWIGLU_LIMIT)
        lin = jnp.clip(gl[:, de:], -SWIGLU_LIMIT, SWIGLU_LIMIT)
        glu = g * jax.nn.sigmoid(SWIGLU_ALPHA * g)
        _store_row_tiles(o_ref, _dot(((lin + 1.0) * glu).astype(BF16), w2_ref[0]) + b2_ref[0])

    @pl.when(pl.program_id(0) >= nu_ref[0])
    def _():
        o_ref[...] = jnp.zeros_like(o_ref)


def _experts(buf, block_e, n_used, w1_bf, b1, w2_bf, b2, rows_per_block):
    n_rows = buf.shape[0]
    ne, d, d2 = w1_bf.shape
    de = w2_bf.shape[1]
    nb = n_rows // rows_per_block
    rowblk = lambda j, be, nu: (jnp.minimum(j, nu[0] - 1), 0, 0)
    by_e = lambda j, be, nu: (be[j], 0, 0)
    blk_shape = (rows_per_block,) + buf.shape[1:]
    return pl.pallas_call(
        _expert_kernel,
        grid_spec=pltpu.PrefetchScalarGridSpec(
            num_scalar_prefetch=2,
            grid=(nb,),
            in_specs=[pl.BlockSpec(blk_shape, rowblk),
                      pl.BlockSpec((1, d, d2), by_e), pl.BlockSpec((1, 1, d2), by_e),
                      pl.BlockSpec((1, de, d), by_e), pl.BlockSpec((1, 1, d), by_e)],
            out_specs=pl.BlockSpec(blk_shape, lambda j, be, nu: (j, 0, 0))),
        out_shape=jax.ShapeDtypeStruct(buf.shape, F32),
        compiler_params=_cparams(("arbitrary",)),
        name="moe_experts",
    )(block_e, n_used, buf, w1_bf, b1.reshape(ne, 1, d2), w2_bf, b2.reshape(ne, 1, d))


def _combine_kernel(dest_ref, gate_ref, xn_ref, g2_ref, ob_ref, o_ref, rows, sem):
    tl = gate_ref.shape[0]

    def row_copy(t, k):
        return pltpu.make_async_copy(ob_ref.at[dest_ref[k, t]], rows.at[k, t], sem)

    def issue(t, c):
        for k in range(TOP_K):
            row_copy(t, k).start()
        return c

    lax.fori_loop(0, tl, issue, 0)
    for k in range(TOP_K):
        pltpu.make_async_copy(ob_ref.at[pl.ds(0, tl)], rows.at[k], sem).wait()
    gates = gate_ref[...]
    ffn = gates[:, 0:1] * _load_row_tiles(rows.at[0])
    for k in range(1, TOP_K):
        ffn = ffn + gates[:, k:k + 1] * _load_row_tiles(rows.at[k])
    o_ref[0] = xn_ref[0] + g2_ref[0] * ffn


def _combine(out_buf, dest, gates_t, xn, g2):
    b, s, d = xn.shape
    tl = min(COMBINE_TILE, s)
    nt = s // tl
    return pl.pallas_call(
        _combine_kernel,
        grid=(b, nt),
        in_specs=[pl.BlockSpec((TOP_K, tl), lambda bi, i: (0, bi * nt + i), memory_space=pltpu.SMEM),
                  pl.BlockSpec((tl, TOP_K), lambda bi, i: (bi * nt + i, 0)),
                  pl.BlockSpec((1, tl, d), lambda bi, i: (bi, i, 0)),
                  pl.BlockSpec((1, 1, d), lambda bi, i: (bi, 0, 0)),
                  pl.BlockSpec(memory_space=pl.ANY)],
        out_specs=pl.BlockSpec((1, tl, d), lambda bi, i: (bi, i, 0)),
        out_shape=jax.ShapeDtypeStruct((b, s, d), F32),
        scratch_shapes=[pltpu.VMEM((TOP_K, tl) + out_buf.shape[1:], F32), pltpu.SemaphoreType.DMA(())],
        compiler_params=_cparams(("arbitrary", "arbitrary")),
        name="moe_combine",
    )(dest, gates_t, xn, g2, out_buf)


def _moe(h2, logits_t, xn, g2, w1, b1, w2, b2):
    b, s, d = xn.shape
    t = b * s
    n_blocks = (t * TOP_K) // EXPERT_ROWS + N_EXPERTS
    n_rows = n_blocks * EXPERT_ROWS
    idx, gates, rank, cnt = _route(logits_t)
    dest, blk, meta = _slots(cnt, idx, rank, n_blocks, EXPERT_ROWS)
    buf = _dispatch(h2, dest, meta[0, :N_EXPERTS], meta[1, :N_EXPERTS], meta[2, :1], n_rows, EXPERT_ROWS)
    out_buf = _experts(buf, blk[0, :n_blocks], meta[2, :1], w1.astype(BF16), b1, w2.astype(BF16), b2, EXPERT_ROWS)
    return _combine(out_buf, dest, gates.T, xn, g2)


def _layer(x, ctx, c, c_ctx, p, lam_init):
    b, s, d = x.shape
    attn_w = d // 2
    hw = d - attn_w
    v_dim = attn_w // N_HEADS
    qk_dim = v_dim // 2
    qk_cols = N_HEADS * 2 * qk_dim
    v_cols = N_HEADS * v_dim
    assert 2 * qk_dim == LANES and v_dim == LANES and s % GRID_W == 0

    rows = -(-(b + 1) // 8) * 8
    cc = jnp.zeros((rows, d), F32).at[:b].set(c).at[b].set(c_ctx)
    mod = _modulation(cc, p['w_mod'], p['b_mod'])
    mod_x = mod[:b].reshape(b, N_MOD, 1, d)
    sh1, sc1, g1, sh2, sc2, g2 = [mod_x[:, i] for i in range(N_MOD)]
    mod_c = mod[b:b + 1].reshape(1, N_MOD, 1, d)
    csh1, csc1 = mod_c[:, 0], mod_c[:, 1]

    w_in_bf = p['w_in'].astype(BF16)
    qg = jnp.tile(p['q_norm_g'], qk_cols // qk_dim).reshape(1, qk_cols)
    kg = jnp.tile(p['k_norm_g'], qk_cols // qk_dim).reshape(1, qk_cols)
    n1g = p['norm1_g'].reshape(1, d)
    cos_t, sin_t = _rope_tables(s, qk_dim)
    q, k, v, u_hy = _project_latent(x, sh1, sc1, n1g, w_in_bf, qg, kg, cos_t, sin_t, qk_cols, v_cols, qk_dim)
    k_c, v_c = _project_context(ctx, csh1, csc1, n1g, w_in_bf[:, qk_cols:2 * qk_cols + v_cols], kg,
                                qk_cols, v_cols, qk_dim)
    k_all = jnp.concatenate([k_c, k], axis=1)
    v_all = jnp.concatenate([v_c, v], axis=1)
    vec = lambda a: a.reshape(1, qk_dim)
    attn = _diff_attention(q, k_all, v_all, vec(p['lam_q1']), vec(p['lam_k1']), vec(p['lam_q2']),
                           vec(p['lam_k2']), p['subln_g'].reshape(1, v_dim), lam_init, qk_dim)

    hfilt = _hyena_filters(s, hw, p['hy_w1'], p['hy_b1'], p['hy_f1'], p['hy_w2'], p['hy_b2'], p['hy_f2'], p['hy_w3'])
    mf, mi = _dft_matrices(s)
    g_spec = _filter_spectra(mf, hfilt, hw)
    uc = _short_conv(u_hy, p['hy_conv_w'], p['hy_conv_b'])
    y1 = _fwd_dft(mf, uc, 0, g_spec, 0, hw)
    z1 = _inv_dft(mi, y1, uc, 0, uc, 1, p['hy_skip'], 0, None, hw)
    y2 = _fwd_dft(mf, z1, 0, g_spec, 1, hw)
    hyn = _inv_dft(mi, y2, z1, 0, uc, 2, p['hy_skip'], 1, p['hy_out_g'], hw)

    xn, h2, logits_t = _out_project(attn, hyn, x, p['w_out'].astype(BF16), g1, sh2, sc2,
                                    p['norm2_g'].reshape(1, d), p['router_w'].T,
                                    p['router_b'].reshape(N_EXPERTS, 1))
    return _moe(h2, logits_t, xn, g2, p['exp_w1'], p['exp_b1'], p['exp_w2'], p['exp_b2'])


def kernel(x, c, ctx, c_ctx, w_mod, b_mod, norm1_g, norm2_g, w_in, q_norm_g, k_norm_g, lam_q1, lam_k1, lam_q2, lam_k2, subln_g, hy_conv_w, hy_conv_b, hy_w1, hy_b1, hy_f1, hy_w2, hy_b2, hy_f2, hy_w3, hy_skip, hy_out_g, w_out, router_w, router_b, exp_w1, exp_b1, exp_w2, exp_b2):
    depth = w_mod.shape[0]
    assert depth == 1, "context-token update between layers is not implemented"
    p = {
        'w_mod': w_mod[0], 'b_mod': b_mod[0], 'norm1_g': norm1_g[0], 'norm2_g': norm2_g[0],
        'w_in': w_in[0], 'q_norm_g': q_norm_g[0], 'k_norm_g': k_norm_g[0],
        'lam_q1': lam_q1[0], 'lam_k1': lam_k1[0], 'lam_q2': lam_q2[0], 'lam_k2': lam_k2[0],
        'subln_g': subln_g[0], 'hy_conv_w': hy_conv_w[0], 'hy_conv_b': hy_conv_b[0],
        'hy_w1': hy_w1[0], 'hy_b1': hy_b1[0], 'hy_f1': hy_f1[0], 'hy_w2': hy_w2[0],
        'hy_b2': hy_b2[0], 'hy_f2': hy_f2[0], 'hy_w3': hy_w3[0], 'hy_skip': hy_skip[0],
        'hy_out_g': hy_out_g[0], 'w_out': w_out[0], 'router_w': router_w[0],
        'router_b': router_b[0], 'exp_w1': exp_w1[0], 'exp_b1': exp_b1[0],
        'exp_w2': exp_w2[0], 'exp_b2': exp_b2[0],
    }
    lam_init = 0.8 - 0.6 * math.exp(-0.3 * 0)
    return _layer(x, ctx, c, c_ctx, p, lam_init)
```

```python
import functools
import math

import jax
import jax.numpy as jnp
from jax import lax
from jax.experimental import pallas as pl
from jax.experimental.pallas import tpu as pltpu

F32 = jnp.float32
BF16 = jnp.bfloat16
I32 = jnp.int32

GRID_W = 64
N_HEADS = 4
N_MOD = 6
SHORT_CONV = 3
HYENA_ORDER = 2
N_BANDS = 8
FEAT_DIM = 1 + 2 * N_BANDS
FILTER_HIDDEN = 64
DECAY_TARGET = 1e-2
FAST_DECAY_PCT = 0.3
SLOW_DECAY_PCT = 1.5
N_EXPERTS = 32
TOP_K = 4
SWIGLU_LIMIT = 7.0
SWIGLU_ALPHA = 1.702
ROPE_BASE = 10000.0
EPS = 1e-6

LANES = 128
V7X_VMEM_LIMIT = 56 * 1024 * 1024

ROW_TILE = 512
ATT_Q_TILE = 256
DFT_TILE = 256
EXPERT_ROWS = 512
TOKEN_TILE = 512


def _log2(n):
    assert n > 0 and n & (n - 1) == 0, f"{n} must be a power of two"
    return n.bit_length() - 1


def _cparams(sem, vmem=V7X_VMEM_LIMIT):
    return pltpu.CompilerParams(dimension_semantics=sem, vmem_limit_bytes=vmem)


def _split_bf16(a):
    hi = a.astype(BF16)
    lo = (a - hi.astype(F32)).astype(BF16)
    return hi, lo


def _dot(a, b):
    return jnp.dot(a, b, preferred_element_type=F32)


def _dot_nt(a, b):
    return lax.dot_general(a, b, (((1,), (1,)), ((), ())), preferred_element_type=F32)


def _store_row_tiles(ref, val):
    rows, d = val.shape
    rt = d // LANES
    for c in range(rt):
        ref[pl.ds(c, rows, stride=rt), :] = val[:, c * LANES:(c + 1) * LANES]


def _load_row_tiles(ref, rt):
    rows = ref.shape[0] // rt
    return jnp.concatenate([ref[pl.ds(c, rows, stride=rt), :] for c in range(rt)], axis=1)


def _dot3(a, b):
    ah, al = _split_bf16(a)
    bh, bl = _split_bf16(b)
    return _dot(ah, bh) + (_dot(ah, bl) + _dot(al, bh))


def _mod_kernel(c_ref, w_ref, b_ref, o_ref):
    c = c_ref[...]
    s = c * jax.nn.sigmoid(c)
    o_ref[...] = _dot3(s, w_ref[...]) + b_ref[...]


def _modulation(cc, w_mod, b_mod):
    rows, d = cc.shape
    n = w_mod.shape[1]
    tn = min(n, 1536)
    return pl.pallas_call(
        _mod_kernel,
        grid=(n // tn,),
        in_specs=[pl.BlockSpec((rows, d), lambda j: (0, 0)),
                  pl.BlockSpec((d, tn), lambda j: (0, j)),
                  pl.BlockSpec((1, tn), lambda j: (0, j))],
        out_specs=pl.BlockSpec((rows, tn), lambda j: (0, j)),
        out_shape=jax.ShapeDtypeStruct((rows, n), F32),
        compiler_params=_cparams(("parallel",)),
        name="modulation",
    )(cc, w_mod, b_mod.reshape(1, n))


def _rope_table_kernel(cos_ref, sin_ref, *, qk_dim):
    s, w = cos_ref.shape
    half = qk_dim // 2
    nf = half // 2
    t = lax.broadcasted_iota(I32, (s, w), 0)
    lane = lax.broadcasted_iota(I32, (s, w), 1)
    d = lane & (qk_dim - 1)
    j = d & (nf - 1)
    row = t >> _log2(GRID_W)
    col = t & (GRID_W - 1)
    pos = jnp.where(d < half, row, col).astype(F32)
    inv = jnp.exp(j.astype(F32) * (-math.log(ROPE_BASE) / nf))
    ang = pos * inv
    first = (d & (half - 1)) < nf
    cos_ref[...] = jnp.cos(ang)
    sn = jnp.sin(ang)
    sin_ref[...] = jnp.where(first, -sn, sn)


def _rope_tables(s, qk_dim):
    return pl.pallas_call(
        functools.partial(_rope_table_kernel, qk_dim=qk_dim),
        out_shape=(jax.ShapeDtypeStruct((s, LANES), F32), jax.ShapeDtypeStruct((s, LANES), F32)),
        name="rope_tables",
    )()


def _group_rms(t, gain, qk_dim):
    w = t.shape[1]
    r = lax.broadcasted_iota(I32, (w, w), 0) >> _log2(qk_dim)
    c = lax.broadcasted_iota(I32, (w, w), 1) >> _log2(qk_dim)
    bd = jnp.where(r == c, 1.0 / qk_dim, 0.0).astype(BF16)
    hi, lo = _split_bf16(t * t)
    ms = _dot(hi, bd) + _dot(lo, bd)
    return t * lax.rsqrt(ms + EPS) * gain


def _rope(t, cos, sin_signed, qk_dim):
    w = t.shape[1]
    nf = qk_dim // 4
    lane = lax.broadcasted_iota(I32, t.shape, 1)
    first = (lane & (2 * nf - 1)) < nf
    partner = jnp.where(first, pltpu.roll(t, w - nf, axis=1), pltpu.roll(t, nf, axis=1))
    return t * cos + partner * sin_signed


def _proj_kernel(*refs, latent, qk_cols, v_cols, qk_dim):
    if latent:
        (x_ref, sh_ref, sc_ref, g_ref, w_ref, qg_ref, kg_ref, cos_ref, sin_ref,
         q_out, k_out, v_out, u_out) = refs
    else:
        x_ref, sh_ref, sc_ref, g_ref, w_ref, kg_ref, k_out, v_out = refs
    x = x_ref[0]
    ms = jnp.mean(x * x, axis=-1, keepdims=True)
    h = (x * lax.rsqrt(ms + EPS) * g_ref[...]) * (1.0 + sc_ref[0]) + sh_ref[0]
    proj = _dot(h.astype(BF16), w_ref[...])
    if latent:
        reps = qk_cols // LANES
        cos = jnp.concatenate([cos_ref[...]] * reps, axis=1)
        sin = jnp.concatenate([sin_ref[...]] * reps, axis=1)
        q = _rope(_group_rms(proj[:, :qk_cols], qg_ref[...], qk_dim), cos, sin, qk_dim)
        q_out[0] = (q * (qk_dim ** -0.5 * math.log2(math.e))).astype(BF16)
        k = _rope(_group_rms(proj[:, qk_cols:2 * qk_cols], kg_ref[...], qk_dim), cos, sin, qk_dim)
        k_out[0] = k.astype(BF16)
        v_out[0] = proj[:, 2 * qk_cols:2 * qk_cols + v_cols].astype(BF16)
        u_out[0] = proj[:, 2 * qk_cols + v_cols:]
    else:
        k = _group_rms(proj[:, :qk_cols], kg_ref[...], qk_dim)
        k_out[0] = k.astype(BF16)
        v_out[0] = proj[:, qk_cols:qk_cols + v_cols].astype(BF16)


def _project_latent(x, sh, sc, g, w_bf, qg, kg, cos_t, sin_t, qk_cols, v_cols, qk_dim):
    b, s, d = x.shape
    n = w_bf.shape[1]
    hy_cols = n - 2 * qk_cols - v_cols
    tm = min(ROW_TILE, s)
    row = lambda bi, i: (bi, i, 0)
    per_b = lambda bi, i: (bi, 0, 0)
    const = lambda bi, i: (0, 0)
    return pl.pallas_call(
        functools.partial(_proj_kernel, latent=True, qk_cols=qk_cols, v_cols=v_cols, qk_dim=qk_dim),
        grid=(b, s // tm),
        in_specs=[pl.BlockSpec((1, tm, d), row),
                  pl.BlockSpec((1, 1, d), per_b), pl.BlockSpec((1, 1, d), per_b),
                  pl.BlockSpec((1, d), const), pl.BlockSpec((d, n), const),
                  pl.BlockSpec((1, qk_cols), const), pl.BlockSpec((1, qk_cols), const),
                  pl.BlockSpec((tm, LANES), lambda bi, i: (i, 0)),
                  pl.BlockSpec((tm, LANES), lambda bi, i: (i, 0))],
        out_specs=[pl.BlockSpec((1, tm, qk_cols), row), pl.BlockSpec((1, tm, qk_cols), row),
                   pl.BlockSpec((1, tm, v_cols), row), pl.BlockSpec((1, tm, hy_cols), row)],
        out_shape=[jax.ShapeDtypeStruct((b, s, qk_cols), BF16), jax.ShapeDtypeStruct((b, s, qk_cols), BF16),
                   jax.ShapeDtypeStruct((b, s, v_cols), BF16), jax.ShapeDtypeStruct((b, s, hy_cols), F32)],
        compiler_params=_cparams(("parallel", "parallel")),
        name="project_latent",
    )(x, sh, sc, g, w_bf, qg, kg, cos_t, sin_t)


def _project_context(ctx, sh, sc, g, w_bf, kg, qk_cols, v_cols, qk_dim):
    b, lc, d = ctx.shape
    n = w_bf.shape[1]
    tm = min(ROW_TILE, lc)
    row = lambda bi, i: (bi, i, 0)
    shared = lambda bi, i: (0, 0, 0)
    const = lambda bi, i: (0, 0)
    return pl.pallas_call(
        functools.partial(_proj_kernel, latent=False, qk_cols=qk_cols, v_cols=v_cols, qk_dim=qk_dim),
        grid=(b, lc // tm),
        in_specs=[pl.BlockSpec((1, tm, d), row),
                  pl.BlockSpec((1, 1, d), shared), pl.BlockSpec((1, 1, d), shared),
                  pl.BlockSpec((1, d), const), pl.BlockSpec((d, n), const),
                  pl.BlockSpec((1, qk_cols), const)],
        out_specs=[pl.BlockSpec((1, tm, qk_cols), row), pl.BlockSpec((1, tm, v_cols), row)],
        out_shape=[jax.ShapeDtypeStruct((b, lc, qk_cols), BF16), jax.ShapeDtypeStruct((b, lc, v_cols), BF16)],
        compiler_params=_cparams(("parallel", "parallel")),
        name="project_context",
    )(ctx, sh, sc, g, w_bf, kg)


def _attn_kernel(q_ref, k_ref, v_ref, lq1, lk1, lq2, lk2, sg_ref, o_ref, kt_ref, *, lam_init, qk_dim):
    lam = (jnp.exp(jnp.sum(lq1[...] * lk1[...], axis=-1, keepdims=True))
           - jnp.exp(jnp.sum(lq2[...] * lk2[...], axis=-1, keepdims=True)) + lam_init)

    @pl.when(pl.program_id(2) == 0)
    def _():
        kt_ref[...] = k_ref[0].T

    q = q_ref[0]
    lane = lax.broadcasted_iota(I32, q.shape, 1)
    zero = jnp.zeros_like(q)

    def probs(qm):
        s = _dot(qm, kt_ref[...])
        e = jnp.exp2(s - jnp.max(s, axis=-1, keepdims=True))
        return e, jnp.sum(e, axis=-1, keepdims=True)

    e1, l1 = probs(jnp.where(lane < qk_dim, q, zero))
    e2, l2 = probs(jnp.where(lane >= qk_dim, q, zero))
    a = e1 - e2 * (lam * l1 / l2)
    o = _dot(a.astype(BF16), v_ref[0]) * (1.0 / l1)
    ms = jnp.mean(o * o, axis=-1, keepdims=True)
    o_ref[0] = ((o * lax.rsqrt(ms + EPS) * sg_ref[...]) * (1.0 - lam_init)).astype(BF16)


def _diff_attention(q, k_all, v_all, lq1, lk1, lq2, lk2, subln_g, lam_init, qk_dim):
    b, s, w = q.shape
    kk = k_all.shape[1]
    tq = min(ATT_Q_TILE, s)
    hb = lambda bi, h, i: (bi, i, h)
    kv = lambda bi, h, i: (bi, 0, h)
    const = lambda bi, h, i: (0, 0)
    vec = pl.BlockSpec((1, qk_dim), const)
    return pl.pallas_call(
        functools.partial(_attn_kernel, lam_init=lam_init, qk_dim=qk_dim),
        grid=(b, N_HEADS, s // tq),
        in_specs=[pl.BlockSpec((1, tq, LANES), hb), pl.BlockSpec((1, kk, LANES), kv),
                  pl.BlockSpec((1, kk, LANES), kv), vec, vec, vec, vec,
                  pl.BlockSpec((1, LANES), const)],
        out_specs=pl.BlockSpec((1, tq, LANES), hb),
        out_shape=jax.ShapeDtypeStruct((b, s, w), BF16),
        scratch_shapes=[pltpu.VMEM((LANES, kk), BF16)],
        compiler_params=_cparams(("parallel", "parallel", "arbitrary")),
        name="diff_attention",
    )(q, k_all, v_all, lq1, lk1, lq2, lk2, subln_g)


def _filter_kernel(w1_ref, b1_ref, f1_ref, w2_ref, b2_ref, f2_ref, w3_ref, o_ref, *, seq, hw):
    tl, n = o_ref.shape
    base = pl.program_id(0) * tl
    pos = (lax.broadcasted_iota(I32, (tl, LANES), 0) + base).astype(F32)
    lane = lax.broadcasted_iota(I32, (tl, LANES), 1)
    tn = pos / seq
    band_idx = jnp.where(lane <= N_BANDS, lane - 1, lane - 1 - N_BANDS).astype(F32)
    band = 1e-4 + band_idx * ((N_BANDS - 1 - 1e-4) / (N_BANDS - 1))
    ang = (2.0 * math.pi / seq) * pos * band
    feats = jnp.where(lane == 0, tn,
                      jnp.where(lane <= N_BANDS, jnp.sin(ang),
                                jnp.where(lane < FEAT_DIM, jnp.cos(ang), 0.0)))
    h = jnp.sin(f1_ref[...] * (_dot3(feats, w1_ref[...]) + b1_ref[...]))
    h = jnp.sin(f2_ref[...] * (_dot3(h, w2_ref[...]) + b2_ref[...]))
    h = _dot3(h, w3_ref[...])
    ch = (lax.broadcasted_iota(I32, (tl, n), 1) & ((1 << _log2(hw)) - 1)).astype(F32)
    lo = abs(math.log(DECAY_TARGET) / SLOW_DECAY_PCT)
    hi = abs(math.log(DECAY_TARGET) / FAST_DECAY_PCT)
    delta = lo + ch * ((hi - lo) / (hw - 1))
    tn_n = (lax.broadcasted_iota(I32, (tl, n), 0) + base).astype(F32) / seq
    o_ref[...] = (h * jnp.exp(-tn_n * delta)).astype(BF16)


def _hyena_filters(seq, hw, w1, b1, f1, w2, b2, f2, w3):
    fh = w2.shape[0]
    n = w3.shape[1]
    w1p = jnp.zeros((LANES, fh), F32).at[:FEAT_DIM].set(w1)
    tl = min(ROW_TILE, seq)
    const = lambda i: (0, 0)
    return pl.pallas_call(
        functools.partial(_filter_kernel, seq=seq, hw=hw),
        grid=(seq // tl,),
        in_specs=[pl.BlockSpec((LANES, fh), const), pl.BlockSpec((1, fh), const), pl.BlockSpec((1, fh), const),
                  pl.BlockSpec((fh, fh), const), pl.BlockSpec((1, fh), const), pl.BlockSpec((1, fh), const),
                  pl.BlockSpec((fh, n), const)],
        out_specs=pl.BlockSpec((tl, n), lambda i: (i, 0)),
        out_shape=jax.ShapeDtypeStruct((seq, n), BF16),
        compiler_params=_cparams(("parallel",)),
        name="hyena_filters",
    )(w1p, b1.reshape(1, fh), f1.reshape(1, fh), w2, b2.reshape(1, fh), f2.reshape(1, fh), w3)


def _dft_kernel(mf_ref, mi_ref, tfc, tfs, tic, tis, *, seq):
    r = tfc.shape[0]
    mask = (1 << _log2(4 * seq)) - 1
    unit = math.pi / (2 * seq)
    i_row = lax.broadcasted_iota(I32, (r, seq), 0)
    col = lax.broadcasted_iota(I32, (r, seq), 1)

    @pl.when(pl.program_id(0) == 0)
    def _():
        af = ((2 * i_row * col) & mask).astype(F32) * unit
        tfc[...] = jnp.cos(af)
        tfs[...] = jnp.sin(af)
        ai = (((2 * col + 1) * i_row) & mask).astype(F32) * unit
        tic[...] = jnp.cos(ai)
        tis[...] = jnp.sin(ai)

    r0 = pl.program_id(0) * r
    c1 = lax.broadcasted_iota(I32, (1, seq), 1)
    bf = (((2 * r0 + 1) * c1) & mask).astype(F32) * unit
    bi = (((2 * c1 + 1) * r0) & mask).astype(F32) * unit
    cbf, sbf = jnp.cos(bf), jnp.sin(bf)
    cbi, sbi = jnp.cos(bi), jnp.sin(bi)
    mf_ref[0] = (cbf * tfc[...] - sbf * tfs[...]).astype(BF16)
    mf_ref[1] = (sbf * tfc[...] + cbf * tfs[...]).astype(BF16)
    mi_ref[:, :seq] = (cbi * tic[...] - sbi * tis[...]).astype(BF16)
    mi_ref[:, seq:] = (sbi * tic[...] + cbi * tis[...]).astype(BF16)


def _dft_matrices(seq):
    r = min(DFT_TILE, seq)
    tbl = pltpu.VMEM((r, seq), F32)
    return pl.pallas_call(
        functools.partial(_dft_kernel, seq=seq),
        grid=(seq // r,),
        out_specs=[pl.BlockSpec((2, r, seq), lambda i: (0, i, 0)),
                   pl.BlockSpec((r, 2 * seq), lambda i: (i, 0))],
        out_shape=[jax.ShapeDtypeStruct((2, seq, seq), BF16), jax.ShapeDtypeStruct((seq, 2 * seq), BF16)],
        scratch_shapes=[tbl, tbl, tbl, tbl],
        compiler_params=_cparams(("arbitrary",)),
        name="dft_matrices",
    )()


def _spectrum_kernel(mf_ref, h_ref, g_ref, *, seq, hw):
    hb = h_ref[...]
    hc = _dot(mf_ref[0], hb)
    hs = _dot(mf_ref[1], hb)
    scale = 1.0 / seq
    g_ref[0, 0] = (hc[:, :hw] + hc[:, hw:]) * scale
    g_ref[0, 1] = (hs[:, :hw] - hs[:, hw:]) * scale


def _filter_spectra(mf, hfilt, hw):
    seq = mf.shape[1]
    r = min(DFT_TILE, seq)
    return pl.pallas_call(
        functools.partial(_spectrum_kernel, seq=seq, hw=hw),
        grid=(HYENA_ORDER, seq // r),
        in_specs=[pl.BlockSpec((2, r, seq), lambda n, i: (0, i, 0)),
                  pl.BlockSpec((seq, 2 * hw), lambda n, i: (0, n))],
        out_specs=pl.BlockSpec((1, 2, r, hw), lambda n, i: (n, 0, i, 0)),
        out_shape=jax.ShapeDtypeStruct((HYENA_ORDER, 2, seq, hw), F32),
        compiler_params=_cparams(("parallel", "parallel")),
        name="filter_spectra",
    )(mf, hfilt)


def _short_conv_kernel(u_ref, w_ref, b_ref, o_ref):
    u = u_ref[0]
    s = u.shape[0]
    t = lax.broadcasted_iota(I32, u.shape, 0)
    prev = jnp.where(t == 0, 0.0, pltpu.roll(u, 1, axis=0))
    nxt = jnp.where(t == s - 1, 0.0, pltpu.roll(u, s - 1, axis=0))
    o_ref[0] = b_ref[...] + prev * w_ref[0:1, :] + u * w_ref[1:2, :] + nxt * w_ref[2:3, :]


def _short_conv(u, w, bias):
    b, s, c = u.shape
    tc = min(256, c)
    return pl.pallas_call(
        _short_conv_kernel,
        grid=(b, c // tc),
        in_specs=[pl.BlockSpec((1, s, tc), lambda bi, j: (bi, 0, j)),
                  pl.BlockSpec((SHORT_CONV, tc), lambda bi, j: (0, j)),
                  pl.BlockSpec((1, tc), lambda bi, j: (0, j))],
        out_specs=pl.BlockSpec((1, s, tc), lambda bi, j: (bi, 0, j)),
        out_shape=jax.ShapeDtypeStruct((b, s, c), F32),
        compiler_params=_cparams(("parallel", "parallel")),
        name="short_conv",
    )(u, w, bias.reshape(1, c))


def _fwd_dft_kernel(mf_ref, z_ref, g_ref, y_ref, zb):
    @pl.when(pl.program_id(1) == 0)
    def _():
        zb[...] = z_ref[0].astype(BF16)

    uc = _dot(mf_ref[0], zb[...])
    us = _dot(mf_ref[1], zb[...])
    gc = g_ref[0, 0]
    gs = g_ref[0, 1]
    y_ref[0, 0] = (uc * gc - us * gs).astype(BF16)
    y_ref[0, 1] = (uc * gs + us * gc).astype(BF16)


def _fwd_dft(mf, z, z_col, g, order, hw):
    b, seq = z.shape[0], z.shape[1]
    r = min(DFT_TILE, seq)
    return pl.pallas_call(
        _fwd_dft_kernel,
        grid=(b, seq // r),
        in_specs=[pl.BlockSpec((2, r, seq), lambda bi, i: (0, i, 0)),
                  pl.BlockSpec((1, seq, hw), lambda bi, i: (bi, 0, z_col)),
                  pl.BlockSpec((1, 2, r, hw), lambda bi, i: (order, 0, i, 0))],
        out_specs=pl.BlockSpec((1, 2, r, hw), lambda bi, i: (bi, 0, i, 0)),
        out_shape=jax.ShapeDtypeStruct((b, 2, seq, hw), BF16),
        scratch_shapes=[pltpu.VMEM((seq, hw), BF16)],
        compiler_params=_cparams(("parallel", "arbitrary")),
        name="hyena_fwd_dft",
    )(mf, z, g)


def _inv_dft_kernel(mi_ref, y_ref, z_ref, gate_ref, skip_ref, *rest, final):
    if final:
        og_ref, o_ref = rest
    else:
        (o_ref,) = rest
    conv = _dot(mi_ref[...], y_ref[0])
    z = gate_ref[0] * (conv + z_ref[0] * skip_ref[0])
    if final:
        ms = jnp.mean(z * z, axis=-1, keepdims=True)
        o_ref[0] = (z * lax.rsqrt(ms + EPS) * og_ref[...]).astype(BF16)
    else:
        o_ref[0] = z


def _inv_dft(mi, y, z, z_col, gates, gate_col, skip, order, out_g, hw):
    b, seq = z.shape[0], z.shape[1]
    r = min(DFT_TILE, seq)
    final = out_g is not None
    in_specs = [pl.BlockSpec((r, 2 * seq), lambda bi, i: (i, 0)),
                pl.BlockSpec((1, 2 * seq, hw), lambda bi, i: (bi, 0, 0)),
                pl.BlockSpec((1, r, hw), lambda bi, i: (bi, i, z_col)),
                pl.BlockSpec((1, r, hw), lambda bi, i: (bi, i, gate_col)),
                pl.BlockSpec((1, 1, hw), lambda bi, i: (order, 0, 0))]
    args = [mi, y.reshape(b, 2 * seq, hw), z, gates, skip.reshape(HYENA_ORDER, 1, hw)]
    if final:
        in_specs.append(pl.BlockSpec((1, hw), lambda bi, i: (0, 0)))
        args.append(out_g.reshape(1, hw))
    return pl.pallas_call(
        functools.partial(_inv_dft_kernel, final=final),
        grid=(b, seq // r),
        in_specs=in_specs,
        out_specs=pl.BlockSpec((1, r, hw), lambda bi, i: (bi, i, 0)),
        out_shape=jax.ShapeDtypeStruct((b, seq, hw), BF16 if final else F32),
        compiler_params=_cparams(("parallel", "parallel")),
        name="hyena_inv_dft",
    )(*args)


def _out_kernel(a_ref, hy_ref, x_ref, wo_ref, g1_ref, sh_ref, sc_ref, n2_ref, rw_ref, rb_ref,
                xn_ref, h2_ref, lg_ref):
    aw = a_ref.shape[2]
    mix = _dot(a_ref[0], wo_ref[:aw, :]) + _dot(hy_ref[0], wo_ref[aw:, :])
    xn = x_ref[0] + g1_ref[0] * mix
    xn_ref[0] = xn
    ms = jnp.mean(xn * xn, axis=-1, keepdims=True)
    h2 = (xn * lax.rsqrt(ms + EPS) * n2_ref[...]) * (1.0 + sc_ref[0]) + sh_ref[0]
    _store_row_tiles(h2_ref, h2)
    hh, hl = _split_bf16(h2)
    wh, wl = _split_bf16(rw_ref[...])
    lg_ref[...] = _dot_nt(wh, hh) + (_dot_nt(wh, hl) + _dot_nt(wl, hh)) + rb_ref[...]


def _out_project(attn, hyn, x, wo_bf, g1, sh2, sc2, n2g, rw_t, rb):
    b, s, d = x.shape
    aw, hw = attn.shape[2], hyn.shape[2]
    ne = rw_t.shape[0]
    tm = min(ROW_TILE, s)
    nt = s // tm
    row = lambda bi, i: (bi, i, 0)
    per_b = lambda bi, i: (bi, 0, 0)
    const = lambda bi, i: (0, 0)
    return pl.pallas_call(
        _out_kernel,
        grid=(b, nt),
        in_specs=[pl.BlockSpec((1, tm, aw), row), pl.BlockSpec((1, tm, hw), row), pl.BlockSpec((1, tm, d), row),
                  pl.BlockSpec((aw + hw, d), const),
                  pl.BlockSpec((1, 1, d), per_b), pl.BlockSpec((1, 1, d), per_b), pl.BlockSpec((1, 1, d), per_b),
                  pl.BlockSpec((1, d), const), pl.BlockSpec((ne, d), const), pl.BlockSpec((ne, 1), const)],
        out_specs=[pl.BlockSpec((1, tm, d), row),
                   pl.BlockSpec((tm * (d // LANES), LANES), lambda bi, i: (bi * nt + i, 0)),
                   pl.BlockSpec((ne, tm), lambda bi, i: (0, bi * nt + i))],
        out_shape=[jax.ShapeDtypeStruct((b, s, d), F32), jax.ShapeDtypeStruct((b * s * (d // LANES), LANES), F32),
                   jax.ShapeDtypeStruct((ne, b * s), F32)],
        compiler_params=_cparams(("parallel", "parallel")),
        name="out_project",
    )(attn, hyn, x, wo_bf, g1, sh2, sc2, n2g, rw_t, rb)


def _route_kernel(lg_ref, gate_ref, lpos_ref, tstart_ref, tcnt_ref, cnt_ref, carry):
    ne, tl = lg_ref.shape

    @pl.when(pl.program_id(0) == 0)
    def _():
        carry[...] = jnp.zeros_like(carry)

    l = lg_ref[...]
    rows = lax.broadcasted_iota(I32, (ne, tl), 0).astype(F32)
    vals, sels = [], []
    for k in range(TOP_K):
        m = jnp.max(l, axis=0, keepdims=True)
        ik = jnp.min(jnp.where(l == m, rows, float(ne)), axis=0, keepdims=True)
        sel = rows == ik
        vals.append(m)
        sels.append(sel)
        l = jnp.where(sel, -jnp.inf, l)
    exps = [jnp.exp(v - vals[0]) for v in vals]
    denom = exps[0] + exps[1] + exps[2] + exps[3]
    for k in range(TOP_K):
        gate_ref[k:k + 1, :] = exps[k] / denom
    oh = jnp.zeros((ne, tl), F32)
    for sel in sels:
        oh = oh + jnp.where(sel, 1.0, 0.0)
    r = lax.broadcasted_iota(I32, (tl, tl), 0)
    c = lax.broadcasted_iota(I32, (tl, tl), 1)
    tri = jnp.where(r <= c, 1.0, 0.0).astype(BF16)
    cum = _dot(oh.astype(BF16), tri)
    n_col = jnp.sum(oh, axis=1, keepdims=True)
    er = lax.broadcasted_iota(I32, (ne, LANES), 0)
    ec = lax.broadcasted_iota(I32, (ne, LANES), 1)
    to_lane = lambda col: jnp.sum(jnp.where(er == ec, jnp.broadcast_to(col, (ne, LANES)), 0.0),
                                  axis=0, keepdims=True)
    n_lane = to_lane(n_col)
    off_col = jnp.sum(jnp.where(ec < er, jnp.broadcast_to(n_lane, (ne, LANES)), 0.0), axis=1, keepdims=True)
    slab_pos = cum - oh + off_col
    for k in range(TOP_K):
        lpos_ref[k:k + 1, :] = jnp.sum(jnp.where(sels[k], slab_pos, 0.0), axis=0, keepdims=True).astype(I32)
    tstart_ref[0] = to_lane(carry[:, 0:1]).astype(I32)
    tcnt_ref[0] = n_lane.astype(I32)
    carry[...] = carry[...] + n_col
    cnt_ref[...] = carry[...]


def _route(logits_t, tl):
    ne, t = logits_t.shape
    nt = t // tl
    blk = lambda i: (0, i)
    per_tile = pl.BlockSpec((1, 1, LANES), lambda i: (i, 0, 0))
    return pl.pallas_call(
        _route_kernel,
        grid=(nt,),
        in_specs=[pl.BlockSpec((ne, tl), blk)],
        out_specs=[pl.BlockSpec((TOP_K, tl), blk), pl.BlockSpec((TOP_K, tl), blk), per_tile, per_tile,
                   pl.BlockSpec((ne, LANES), lambda i: (0, 0))],
        out_shape=[jax.ShapeDtypeStruct((TOP_K, t), F32), jax.ShapeDtypeStruct((TOP_K, t), I32),
                   jax.ShapeDtypeStruct((nt, 1, LANES), I32), jax.ShapeDtypeStruct((nt, 1, LANES), I32),
                   jax.ShapeDtypeStruct((ne, LANES), F32)],
        scratch_shapes=[pltpu.VMEM((ne, LANES), F32)],
        compiler_params=_cparams(("arbitrary",)),
        name="moe_route",
    )(logits_t)


def _slots_kernel(cnt_ref, tstart_ref, run_ref, blk_ref, meta_ref, *, rows_per_block):
    ne = cnt_ref.shape[0]
    shift = _log2(rows_per_block)
    cnt = cnt_ref[...].astype(I32)
    padded = ((cnt + (rows_per_block - 1)) >> shift) << shift
    r = lax.broadcasted_iota(I32, (ne, LANES), 0)
    c = lax.broadcasted_iota(I32, (ne, LANES), 1)
    padded_lane = jnp.sum(jnp.where(r == c, padded, 0), axis=0, keepdims=True)
    cnt_lane = jnp.sum(jnp.where(r == c, cnt, 0), axis=0, keepdims=True)
    pend_lane = jnp.sum(jnp.where(r <= c, padded, 0), axis=0, keepdims=True)
    pend_col = jnp.sum(jnp.where(c <= r, jnp.broadcast_to(padded_lane, (ne, LANES)), 0),
                       axis=1, keepdims=True)
    run_ref[...] = tstart_ref[...] + (pend_lane - padded_lane)
    nbp = blk_ref.shape[1]
    j0 = lax.broadcasted_iota(I32, (ne, nbp), 1) * rows_per_block
    be = jnp.sum(jnp.where(jnp.broadcast_to(pend_col, (ne, nbp)) <= j0, 1, 0), axis=0, keepdims=True)
    blk_ref[...] = jnp.minimum(be, ne - 1)
    total = jnp.max(pend_col, axis=0, keepdims=True)
    meta_ref[0:1, :] = pend_lane - padded_lane + cnt_lane
    meta_ref[1:2, :] = padded_lane - cnt_lane
    meta_ref[2:3, :] = jnp.broadcast_to(total >> shift, (1, LANES))
    meta_ref[3:8, :] = jnp.zeros((5, LANES), I32)


def _slots(cnt, tstart, n_blocks, rows_per_block):
    nbp = -(-n_blocks // LANES) * LANES
    return pl.pallas_call(
        functools.partial(_slots_kernel, rows_per_block=rows_per_block),
        out_shape=[jax.ShapeDtypeStruct(tstart.shape, I32), jax.ShapeDtypeStruct((1, nbp), I32),
                   jax.ShapeDtypeStruct((8, LANES), I32)],
        compiler_params=pltpu.CompilerParams(vmem_limit_bytes=V7X_VMEM_LIMIT),
        name="moe_slots",
    )(cnt, tstart)


def _pad_chunks(rows_per_block):
    sizes, s = [], rows_per_block // 2
    while s >= 1:
        sizes.append(s)
        s //= 2
    return sizes


def _rows(start, size, rt):
    return pl.ds(pl.multiple_of(start * rt, rt), size * rt)


def _for_each_run_chunk(run_ref, cnt_ref, tile, ne, max_rows, act):
    sizes = _pad_chunks(2 * max_rows)

    def each(e, off):
        left = cnt_ref[tile * ne + e]
        pos, start = off, run_ref[tile * ne + e]
        for size in sizes:
            hit = (left & size) != 0

            @pl.when(hit)
            def _():
                act(pos, start, size)

            inc = jnp.where(hit, size, 0)
            pos, start = pos + inc, start + inc
        return off + left

    lax.fori_loop(0, ne, each, 0)


def _dispatch_kernel(run_ref, cnt_ref, padlo_ref, npad_ref, nused_ref, lpos_ref, h_ref, buf_ref,
                     slab, zeros, sems, zsem, *, rows_per_block, rt, n_tiles):
    tl = lpos_ref.shape[1]
    ne = padlo_ref.shape[0]
    sizes = _pad_chunks(rows_per_block)
    half = rows_per_block // 2
    i = pl.program_id(0)
    slot = i & 1

    def pad_copy(start, size):
        return pltpu.make_async_copy(zeros.at[_rows(0, size, rt)], buf_ref.at[_rows(start, size, rt)], zsem)

    def run_copies(tile, sl, wait):
        def act(pos, start, size):
            cp = pltpu.make_async_copy(slab.at[sl, _rows(pos, size, rt)], buf_ref.at[_rows(start, size, rt)],
                                       sems.at[sl])
            if wait:
                cp.wait()
            else:
                cp.start()

        _for_each_run_chunk(run_ref, cnt_ref, tile, ne, tl, act)

    @pl.when(i == 0)
    def _():
        zeros[...] = jnp.zeros_like(zeros)

        first, last = 2 * nused_ref[0], buf_ref.shape[0] // (half * rt)
        lax.fori_loop(first, last, lambda j, c: (pad_copy(j * half, half).start(), c)[1], 0)
        lax.fori_loop(first, last, lambda j, c: (pad_copy(j * half, half).wait(), c)[1], 0)

        def each(e, wait):
            start = padlo_ref[e]
            left = npad_ref[e]
            for size in sizes:
                hit = (left & size) != 0

                @pl.when(hit)
                def _():
                    cp = pad_copy(start, size)
                    if wait:
                        cp.wait()
                    else:
                        cp.start()

                start = start + jnp.where(hit, size, 0)

        lax.fori_loop(0, ne, lambda e, c: (each(e, False), c)[1], 0)
        lax.fori_loop(0, ne, lambda e, c: (each(e, True), c)[1], 0)

    @pl.when(i >= 2)
    def _():
        run_copies(i - 2, slot, True)

    def fill(t, c):
        row = h_ref[_rows(t, 1, rt), :]
        for k in range(TOP_K):
            slab[slot, _rows(lpos_ref[k, t], 1, rt), :] = row
        return c

    lax.fori_loop(0, tl, fill, 0, unroll=8)
    run_copies(i, slot, False)

    @pl.when(i == n_tiles - 1)
    def _():
        if n_tiles >= 2:
            run_copies(i - 1, 1 - slot, True)
        run_copies(i, slot, True)


def _dispatch(h2r, lpos, run_start, run_cnt, pad_lo, n_pad, n_used, n_rows, rows_per_block, rt, tl):
    n_tiles = lpos.shape[1] // tl
    return pl.pallas_call(
        functools.partial(_dispatch_kernel, rows_per_block=rows_per_block, rt=rt, n_tiles=n_tiles),
        grid_spec=pltpu.PrefetchScalarGridSpec(
            num_scalar_prefetch=5,
            grid=(n_tiles,),
            in_specs=[pl.BlockSpec((TOP_K, tl), lambda i, *_: (0, i), memory_space=pltpu.SMEM),
                      pl.BlockSpec((tl * rt, LANES), lambda i, *_: (i, 0))],
            out_specs=pl.BlockSpec(memory_space=pl.ANY),
            scratch_shapes=[pltpu.VMEM((2, TOP_K * tl * rt, LANES), F32),
                            pltpu.VMEM((rows_per_block // 2 * rt, LANES), F32),
                            pltpu.SemaphoreType.DMA((2,)), pltpu.SemaphoreType.DMA(())]),
        out_shape=jax.ShapeDtypeStruct((n_rows * rt, LANES), F32),
        compiler_params=_cparams(("arbitrary",)),
        name="moe_dispatch",
    )(run_start, run_cnt, pad_lo, n_pad, n_used, lpos, h2r)


def _expert_kernel(be_ref, nu_ref, x_ref, w1_ref, b1_ref, w2_ref, b2_ref, o_ref):
    @pl.when(pl.program_id(0) < nu_ref[0])
    def _():
        de = w2_ref.shape[1]
        gl = _dot(_load_row_tiles(x_ref, w1_ref.shape[1] // LANES).astype(BF16), w1_ref[0]) + b1_ref[0]
        g = jnp.minimum(gl[:, :de], SWIGLU_LIMIT)
        lin = jnp.clip(gl[:, de:], -SWIGLU_LIMIT, SWIGLU_LIMIT)
        glu = g * jax.nn.sigmoid(SWIGLU_ALPHA * g)
        _store_row_tiles(o_ref, _dot(((lin + 1.0) * glu).astype(BF16), w2_ref[0]) + b2_ref[0])

    @pl.when(pl.program_id(0) >= nu_ref[0])
    def _():
        o_ref[...] = jnp.zeros_like(o_ref)


def _experts(buf, block_e, n_used, w1_bf, b1, w2_bf, b2, rows_per_block):
    ne, d, d2 = w1_bf.shape
    de = w2_bf.shape[1]
    blk_shape = (rows_per_block * (d // LANES), LANES)
    nb = buf.shape[0] // blk_shape[0]
    rowblk = lambda j, be, nu: (jnp.minimum(j, nu[0] - 1), 0)
    by_e = lambda j, be, nu: (be[j], 0, 0)
    return pl.pallas_call(
        _expert_kernel,
        grid_spec=pltpu.PrefetchScalarGridSpec(
            num_scalar_prefetch=2,
            grid=(nb,),
            in_specs=[pl.BlockSpec(blk_shape, rowblk),
                      pl.BlockSpec((1, d, d2), by_e), pl.BlockSpec((1, 1, d2), by_e),
                      pl.BlockSpec((1, de, d), by_e), pl.BlockSpec((1, 1, d), by_e)],
            out_specs=pl.BlockSpec(blk_shape, lambda j, be, nu: (j, 0))),
        out_shape=jax.ShapeDtypeStruct(buf.shape, F32),
        compiler_params=_cparams(("arbitrary",)),
        name="moe_experts",
    )(block_e, n_used, buf, w1_bf, b1.reshape(ne, 1, d2), w2_bf, b2.reshape(ne, 1, d))


def _combine_kernel(run_ref, cnt_ref, lpos_ref, gate_ref, xn_ref, g2_ref, ob_ref, o_ref, slab, acc, sems,
                    *, rt, ne, n_tiles):
    tl = lpos_ref.shape[1]
    i = pl.program_id(0) * pl.num_programs(1) + pl.program_id(1)
    slot = i & 1

    def run_copies(tile, sl, wait):
        def act(pos, start, size):
            cp = pltpu.make_async_copy(ob_ref.at[_rows(start, size, rt)], slab.at[sl, _rows(pos, size, rt)],
                                       sems.at[sl])
            if wait:
                cp.wait()
            else:
                cp.start()

        _for_each_run_chunk(run_ref, cnt_ref, tile, ne, tl, act)

    @pl.when(i == 0)
    def _():
        run_copies(i, slot, False)

    @pl.when(i + 1 < n_tiles)
    def _():
        run_copies(i + 1, 1 - slot, False)

    run_copies(i, slot, True)

    def token(t, c):
        a = gate_ref[0, t] * slab[slot, _rows(lpos_ref[0, t], 1, rt), :]
        for k in range(1, TOP_K):
            a = a + gate_ref[k, t] * slab[slot, _rows(lpos_ref[k, t], 1, rt), :]
        acc[_rows(t, 1, rt), :] = a
        return c

    lax.fori_loop(0, tl, token, 0, unroll=8)
    o_ref[0] = xn_ref[0] + g2_ref[0] * _load_row_tiles(acc, rt)


def _combine(out_buf, lpos, gates, run_start, run_cnt, xn, g2, rt, tl):
    b, s, d = xn.shape
    nt = s // tl
    tok = lambda bi, i, *_: (0, bi * nt + i)
    return pl.pallas_call(
        functools.partial(_combine_kernel, rt=rt, ne=N_EXPERTS, n_tiles=b * nt),
        grid_spec=pltpu.PrefetchScalarGridSpec(
            num_scalar_prefetch=2,
            grid=(b, nt),
            in_specs=[pl.BlockSpec((TOP_K, tl), tok, memory_space=pltpu.SMEM),
                      pl.BlockSpec((TOP_K, tl), tok, memory_space=pltpu.SMEM),
                      pl.BlockSpec((1, tl, d), lambda bi, i, *_: (bi, i, 0)),
                      pl.BlockSpec((1, 1, d), lambda bi, i, *_: (bi, 0, 0)),
                      pl.BlockSpec(memory_space=pl.ANY)],
            out_specs=pl.BlockSpec((1, tl, d), lambda bi, i, *_: (bi, i, 0)),
            scratch_shapes=[pltpu.VMEM((2, TOP_K * tl * rt, LANES), F32), pltpu.VMEM((tl * rt, LANES), F32),
                            pltpu.SemaphoreType.DMA((2,))]),
        out_shape=jax.ShapeDtypeStruct((b, s, d), F32),
        compiler_params=_cparams(("arbitrary", "arbitrary")),
        name="moe_combine",
    )(run_start, run_cnt, lpos, gates, xn, g2, out_buf)


def _moe(h2r, logits_t, xn, g2, w1, b1, w2, b2):
    b, s, d = xn.shape
    t = b * s
    rt = d // LANES
    tl = min(TOKEN_TILE, s)
    assert s % tl == 0
    n_blocks = (t * TOP_K) // EXPERT_ROWS + N_EXPERTS
    n_rows = n_blocks * EXPERT_ROWS
    gates, lpos, tstart, tcnt, cnt = _route(logits_t, tl)
    run, blk, meta = _slots(cnt, tstart, n_blocks, EXPERT_ROWS)
    run_start = run[:, 0, :N_EXPERTS].reshape(-1)
    run_cnt = tcnt[:, 0, :N_EXPERTS].reshape(-1)
    buf = _dispatch(h2r, lpos, run_start, run_cnt, meta[0, :N_EXPERTS], meta[1, :N_EXPERTS], meta[2, :1],
                    n_rows, EXPERT_ROWS, rt, tl)
    out_buf = _experts(buf, blk[0, :n_blocks], meta[2, :1], w1.astype(BF16), b1, w2.astype(BF16), b2, EXPERT_ROWS)
    return _combine(out_buf, lpos, gates, run_start, run_cnt, xn, g2, rt, tl)


def _layer(x, ctx, c, c_ctx, p, lam_init):
    b, s, d = x.shape
    attn_w = d // 2
    hw = d - attn_w
    v_dim = attn_w // N_HEADS
    qk_dim = v_dim // 2
    qk_cols = N_HEADS * 2 * qk_dim
    v_cols = N_HEADS * v_dim
    assert 2 * qk_dim == LANES and v_dim == LANES and s % GRID_W == 0

    rows = -(-(b + 1) // 8) * 8
    cc = jnp.zeros((rows, d), F32).at[:b].set(c).at[b].set(c_ctx)
    mod = _modulation(cc, p['w_mod'], p['b_mod'])
    mod_x = mod[:b].reshape(b, N_MOD, 1, d)
    sh1, sc1, g1, sh2, sc2, g2 = [mod_x[:, i] for i in range(N_MOD)]
    mod_c = mod[b:b + 1].reshape(1, N_MOD, 1, d)
    csh1, csc1 = mod_c[:, 0], mod_c[:, 1]

    w_in_bf = p['w_in'].astype(BF16)
    qg = jnp.tile(p['q_norm_g'], qk_cols // qk_dim).reshape(1, qk_cols)
    kg = jnp.tile(p['k_norm_g'], qk_cols // qk_dim).reshape(1, qk_cols)
    n1g = p['norm1_g'].reshape(1, d)
    cos_t, sin_t = _rope_tables(s, qk_dim)
    q, k, v, u_hy = _project_latent(x, sh1, sc1, n1g, w_in_bf, qg, kg, cos_t, sin_t, qk_cols, v_cols, qk_dim)
    k_c, v_c = _project_context(ctx, csh1, csc1, n1g, w_in_bf[:, qk_cols:2 * qk_cols + v_cols], kg,
                                qk_cols, v_cols, qk_dim)
    k_all = jnp.concatenate([k_c, k], axis=1)
    v_all = jnp.concatenate([v_c, v], axis=1)
    vec = lambda a: a.reshape(1, qk_dim)
    attn = _diff_attention(q, k_all, v_all, vec(p['lam_q1']), vec(p['lam_k1']), vec(p['lam_q2']),
                           vec(p['lam_k2']), p['subln_g'].reshape(1, v_dim), lam_init, qk_dim)

    hfilt = _hyena_filters(s, hw, p['hy_w1'], p['hy_b1'], p['hy_f1'], p['hy_w2'], p['hy_b2'], p['hy_f2'], p['hy_w3'])
    mf, mi = _dft_matrices(s)
    g_spec = _filter_spectra(mf, hfilt, hw)
    uc = _short_conv(u_hy, p['hy_conv_w'], p['hy_conv_b'])
    y1 = _fwd_dft(mf, uc, 0, g_spec, 0, hw)
    z1 = _inv_dft(mi, y1, uc, 0, uc, 1, p['hy_skip'], 0, None, hw)
    y2 = _fwd_dft(mf, z1, 0, g_spec, 1, hw)
    hyn = _inv_dft(mi, y2, z1, 0, uc, 2, p['hy_skip'], 1, p['hy_out_g'], hw)

    xn, h2, logits_t = _out_project(attn, hyn, x, p['w_out'].astype(BF16), g1, sh2, sc2,
                                    p['norm2_g'].reshape(1, d), p['router_w'].T,
                                    p['router_b'].reshape(N_EXPERTS, 1))
    return _moe(h2, logits_t, xn, g2, p['exp_w1'], p['exp_b1'], p['exp_w2'], p['exp_b2'])


def kernel(x, c, ctx, c_ctx, w_mod, b_mod, norm1_g, norm2_g, w_in, q_norm_g, k_norm_g, lam_q1, lam_k1, lam_q2, lam_k2, subln_g, hy_conv_w, hy_conv_b, hy_w1, hy_b1, hy_f1, hy_w2, hy_b2, hy_f2, hy_w3, hy_skip, hy_out_g, w_out, router_w, router_b, exp_w1, exp_b1, exp_w2, exp_b2):
    depth = w_mod.shape[0]
    assert depth == 1, "context-token update between layers is not implemented"
    p = {
        'w_mod': w_mod[0], 'b_mod': b_mod[0], 'norm1_g': norm1_g[0], 'norm2_g': norm2_g[0],
        'w_in': w_in[0], 'q_norm_g': q_norm_g[0], 'k_norm_g': k_norm_g[0],
        'lam_q1': lam_q1[0], 'lam_k1': lam_k1[0], 'lam_q2': lam_q2[0], 'lam_k2': lam_k2[0],
        'subln_g': subln_g[0], 'hy_conv_w': hy_conv_w[0], 'hy_conv_b': hy_conv_b[0],
        'hy_w1': hy_w1[0], 'hy_b1': hy_b1[0], 'hy_f1': hy_f1[0], 'hy_w2': hy_w2[0],
        'hy_b2': hy_b2[0], 'hy_f2': hy_f2[0], 'hy_w3': hy_w3[0], 'hy_skip': hy_skip[0],
        'hy_out_g': hy_out_g[0], 'w_out': w_out[0], 'router_w': router_w[0],
        'router_b': router_b[0], 'exp_w1': exp_w1[0], 'exp_b1': exp_b1[0],
        'exp_w2': exp_w2[0], 'exp_b2': exp_b2[0],
    }
    lam_init = 0.8 - 0.6 * math.exp(-0.3 * 0)
    return _layer(x, ctx, c, c_ctx, p, lam_init)
```

```python
import functools
import math

import jax
import jax.numpy as jnp
from jax import lax
from jax.experimental import pallas as pl
from jax.experimental.pallas import tpu as pltpu

F32 = jnp.float32
BF16 = jnp.bfloat16
I32 = jnp.int32

GRID_W = 64
N_HEADS = 4
N_MOD = 6
SHORT_CONV = 3
HYENA_ORDER = 2
N_BANDS = 8
FEAT_DIM = 1 + 2 * N_BANDS
FILTER_HIDDEN = 64
DECAY_TARGET = 1e-2
FAST_DECAY_PCT = 0.3
SLOW_DECAY_PCT = 1.5
N_EXPERTS = 32
TOP_K = 4
SWIGLU_LIMIT = 7.0
SWIGLU_ALPHA = 1.702
ROPE_BASE = 10000.0
EPS = 1e-6

LANES = 128
V7X_VMEM_LIMIT = 56 * 1024 * 1024

ROW_TILE = 512
ATT_Q_TILE = 256
ATT_KEY_CHUNKS = 17
DFT_TILE = 256
EXPERT_ROWS = 512
TOKEN_TILE = 512


def _log2(n):
    assert n > 0 and n & (n - 1) == 0, f"{n} must be a power of two"
    return n.bit_length() - 1


def _cparams(sem, vmem=V7X_VMEM_LIMIT):
    return pltpu.CompilerParams(dimension_semantics=sem, vmem_limit_bytes=vmem)


def _split_bf16(a):
    hi = a.astype(BF16)
    lo = (a - hi.astype(F32)).astype(BF16)
    return hi, lo


def _dot(a, b):
    return jnp.dot(a, b, preferred_element_type=F32)


def _dot_nt(a, b):
    return lax.dot_general(a, b, (((1,), (1,)), ((), ())), preferred_element_type=F32)


def _store_row_tiles(ref, val):
    rows, d = val.shape
    rt = d // LANES
    for c in range(rt):
        ref[pl.ds(c, rows, stride=rt), :] = val[:, c * LANES:(c + 1) * LANES]


def _load_row_tiles(ref, rt):
    rows = ref.shape[0] // rt
    return jnp.concatenate([ref[pl.ds(c, rows, stride=rt), :] for c in range(rt)], axis=1)


def _dot3(a, b):
    ah, al = _split_bf16(a)
    bh, bl = _split_bf16(b)
    return _dot(ah, bh) + (_dot(ah, bl) + _dot(al, bh))


def _mod_kernel(c_ref, w_ref, b_ref, o_ref):
    c = c_ref[...]
    s = c * jax.nn.sigmoid(c)
    o_ref[...] = _dot3(s, w_ref[...]) + b_ref[...]


def _modulation(cc, w_mod, b_mod):
    rows, d = cc.shape
    n = w_mod.shape[1]
    tn = min(n, 1536)
    return pl.pallas_call(
        _mod_kernel,
        grid=(n // tn,),
        in_specs=[pl.BlockSpec((rows, d), lambda j: (0, 0)),
                  pl.BlockSpec((d, tn), lambda j: (0, j)),
                  pl.BlockSpec((1, tn), lambda j: (0, j))],
        out_specs=pl.BlockSpec((rows, tn), lambda j: (0, j)),
        out_shape=jax.ShapeDtypeStruct((rows, n), F32),
        compiler_params=_cparams(("parallel",)),
        name="modulation",
    )(cc, w_mod, b_mod.reshape(1, n))


def _rope_table_kernel(cos_ref, sin_ref, *, qk_dim):
    s, w = cos_ref.shape
    half = qk_dim // 2
    nf = half // 2
    t = lax.broadcasted_iota(I32, (s, w), 0)
    lane = lax.broadcasted_iota(I32, (s, w), 1)
    d = lane & (qk_dim - 1)
    j = d & (nf - 1)
    row = t >> _log2(GRID_W)
    col = t & (GRID_W - 1)
    pos = jnp.where(d < half, row, col).astype(F32)
    inv = jnp.exp(j.astype(F32) * (-math.log(ROPE_BASE) / nf))
    ang = pos * inv
    first = (d & (half - 1)) < nf
    cos_ref[...] = jnp.cos(ang)
    sn = jnp.sin(ang)
    sin_ref[...] = jnp.where(first, -sn, sn)


def _rope_tables(s, qk_dim):
    return pl.pallas_call(
        functools.partial(_rope_table_kernel, qk_dim=qk_dim),
        out_shape=(jax.ShapeDtypeStruct((s, LANES), F32), jax.ShapeDtypeStruct((s, LANES), F32)),
        name="rope_tables",
    )()


def _group_rms(t, gain, qk_dim):
    w = t.shape[1]
    r = lax.broadcasted_iota(I32, (w, w), 0) >> _log2(qk_dim)
    c = lax.broadcasted_iota(I32, (w, w), 1) >> _log2(qk_dim)
    bd = jnp.where(r == c, 1.0 / qk_dim, 0.0).astype(BF16)
    hi, lo = _split_bf16(t * t)
    ms = _dot(hi, bd) + _dot(lo, bd)
    return t * lax.rsqrt(ms + EPS) * gain


def _rope(t, cos, sin_signed, qk_dim):
    w = t.shape[1]
    nf = qk_dim // 4
    lane = lax.broadcasted_iota(I32, t.shape, 1)
    first = (lane & (2 * nf - 1)) < nf
    partner = jnp.where(first, pltpu.roll(t, w - nf, axis=1), pltpu.roll(t, nf, axis=1))
    return t * cos + partner * sin_signed


def _proj_kernel(*refs, latent, qk_cols, v_cols, qk_dim):
    if latent:
        (x_ref, sh_ref, sc_ref, g_ref, w_ref, qg_ref, kg_ref, cos_ref, sin_ref,
         q_out, k_out, v_out, u_out) = refs
    else:
        x_ref, sh_ref, sc_ref, g_ref, w_ref, kg_ref, k_out, v_out = refs
    x = x_ref[0]
    ms = jnp.mean(x * x, axis=-1, keepdims=True)
    h = (x * lax.rsqrt(ms + EPS) * g_ref[...]) * (1.0 + sc_ref[0]) + sh_ref[0]
    proj = _dot(h.astype(BF16), w_ref[...])
    if latent:
        reps = qk_cols // LANES
        cos = jnp.concatenate([cos_ref[...]] * reps, axis=1)
        sin = jnp.concatenate([sin_ref[...]] * reps, axis=1)
        q = _rope(_group_rms(proj[:, :qk_cols], qg_ref[...], qk_dim), cos, sin, qk_dim)
        q_out[0] = (q * (qk_dim ** -0.5 * math.log2(math.e))).astype(BF16)
        k = _rope(_group_rms(proj[:, qk_cols:2 * qk_cols], kg_ref[...], qk_dim), cos, sin, qk_dim)
        k_out[0] = k.astype(BF16)
        v_out[0] = proj[:, 2 * qk_cols:2 * qk_cols + v_cols].astype(BF16)
        u_out[0] = proj[:, 2 * qk_cols + v_cols:]
    else:
        k = _group_rms(proj[:, :qk_cols], kg_ref[...], qk_dim)
        k_out[0] = k.astype(BF16)
        v_out[0] = proj[:, qk_cols:qk_cols + v_cols].astype(BF16)


def _project_latent(x, sh, sc, g, w_bf, qg, kg, cos_t, sin_t, qk_cols, v_cols, qk_dim):
    b, s, d = x.shape
    n = w_bf.shape[1]
    hy_cols = n - 2 * qk_cols - v_cols
    tm = min(ROW_TILE, s)
    row = lambda bi, i: (bi, i, 0)
    per_b = lambda bi, i: (bi, 0, 0)
    const = lambda bi, i: (0, 0)
    return pl.pallas_call(
        functools.partial(_proj_kernel, latent=True, qk_cols=qk_cols, v_cols=v_cols, qk_dim=qk_dim),
        grid=(b, s // tm),
        in_specs=[pl.BlockSpec((1, tm, d), row),
                  pl.BlockSpec((1, 1, d), per_b), pl.BlockSpec((1, 1, d), per_b),
                  pl.BlockSpec((1, d), const), pl.BlockSpec((d, n), const),
                  pl.BlockSpec((1, qk_cols), const), pl.BlockSpec((1, qk_cols), const),
                  pl.BlockSpec((tm, LANES), lambda bi, i: (i, 0)),
                  pl.BlockSpec((tm, LANES), lambda bi, i: (i, 0))],
        out_specs=[pl.BlockSpec((1, tm, qk_cols), row), pl.BlockSpec((1, tm, qk_cols), row),
                   pl.BlockSpec((1, tm, v_cols), row), pl.BlockSpec((1, tm, hy_cols), row)],
        out_shape=[jax.ShapeDtypeStruct((b, s, qk_cols), BF16), jax.ShapeDtypeStruct((b, s, qk_cols), BF16),
                   jax.ShapeDtypeStruct((b, s, v_cols), BF16), jax.ShapeDtypeStruct((b, s, hy_cols), F32)],
        compiler_params=_cparams(("parallel", "parallel")),
        name="project_latent",
    )(x, sh, sc, g, w_bf, qg, kg, cos_t, sin_t)


def _project_context(ctx, sh, sc, g, w_bf, kg, qk_cols, v_cols, qk_dim):
    b, lc, d = ctx.shape
    n = w_bf.shape[1]
    tm = min(ROW_TILE, lc)
    row = lambda bi, i: (bi, i, 0)
    shared = lambda bi, i: (0, 0, 0)
    const = lambda bi, i: (0, 0)
    return pl.pallas_call(
        functools.partial(_proj_kernel, latent=False, qk_cols=qk_cols, v_cols=v_cols, qk_dim=qk_dim),
        grid=(b, lc // tm),
        in_specs=[pl.BlockSpec((1, tm, d), row),
                  pl.BlockSpec((1, 1, d), shared), pl.BlockSpec((1, 1, d), shared),
                  pl.BlockSpec((1, d), const), pl.BlockSpec((d, n), const),
                  pl.BlockSpec((1, qk_cols), const)],
        out_specs=[pl.BlockSpec((1, tm, qk_cols), row), pl.BlockSpec((1, tm, v_cols), row)],
        out_shape=[jax.ShapeDtypeStruct((b, lc, qk_cols), BF16), jax.ShapeDtypeStruct((b, lc, v_cols), BF16)],
        compiler_params=_cparams(("parallel", "parallel")),
        name="project_context",
    )(ctx, sh, sc, g, w_bf, kg)


def _key_chunks(kk, n):
    tiles = kk // LANES
    n = min(n, tiles)
    return [(LANES * (i * tiles // n), LANES * ((i + 1) * tiles // n)) for i in range(n)]


def _attn_kernel(q_ref, k_ref, v_ref, lq1, lk1, lq2, lk2, sg_ref, o_ref, kt_ref, *bufs, lam_init, qk_dim):
    lam = (jnp.exp(jnp.sum(lq1[...] * lk1[...], axis=-1, keepdims=True))
           - jnp.exp(jnp.sum(lq2[...] * lk2[...], axis=-1, keepdims=True)) + lam_init)
    j = pl.program_id(2)
    even, odd = bufs[:4], bufs[4:]
    tq = q_ref.shape[1]
    chunks = _key_chunks(kt_ref.shape[1], ATT_KEY_CHUNKS)

    @pl.when(j == 0)
    def _():
        kt_ref[...] = k_ref[0].T
        for ref in odd:
            ref[...] = jnp.zeros_like(ref)

    def step(cur, prev):
        s1_w, s2_w, m1_w, m2_w = cur
        s1_r, s2_r, m1_r, m2_r = prev
        q = q_ref[0]
        lane = lax.broadcasted_iota(I32, q.shape, 1)
        q1 = jnp.where(lane < qk_dim, q, jnp.zeros_like(q))
        q2 = jnp.where(lane >= qk_dim, q, jnp.zeros_like(q))
        m1p, m2p = m1_r[:, :1], m2_r[:, :1]
        m1 = m2 = jnp.full((tq, 1), -jnp.inf, F32)
        v = v_ref[0]
        v1 = jnp.concatenate([v, jnp.ones_like(v)], axis=1)
        o1 = o2 = jnp.zeros((tq, 2 * LANES), F32)
        for c0, c1 in chunks:
            s1 = _dot(q1, kt_ref[:, c0:c1])
            s2 = _dot(q2, kt_ref[:, c0:c1])
            s1_w[:, c0:c1] = s1
            s2_w[:, c0:c1] = s2
            m1 = jnp.maximum(m1, jnp.max(s1, axis=-1, keepdims=True))
            m2 = jnp.maximum(m2, jnp.max(s2, axis=-1, keepdims=True))

            e1 = jnp.exp2(s1_r[:, c0:c1] - m1p)
            e2 = jnp.exp2(s2_r[:, c0:c1] - m2p)
            o1 = o1 + _dot(e1.astype(BF16), v1[c0:c1, :])
            o2 = o2 + _dot(e2.astype(BF16), v1[c0:c1, :])
        m1_w[...] = jnp.broadcast_to(m1, m1_w.shape)
        m2_w[...] = jnp.broadcast_to(m2, m2_w.shape)
        o = o1[:, :LANES] / o1[:, LANES:] - o2[:, :LANES] * (lam / o2[:, LANES:])
        ms = jnp.mean(o * o, axis=-1, keepdims=True)
        o_ref[0] = ((o * lax.rsqrt(ms + EPS) * sg_ref[...]) * (1.0 - lam_init)).astype(BF16)

    @pl.when((j & 1) == 0)
    def _():
        step(even, odd)

    @pl.when((j & 1) == 1)
    def _():
        step(odd, even)


def _diff_attention(q, k_all, v_all, lq1, lk1, lq2, lk2, subln_g, lam_init, qk_dim):
    b, s, w = q.shape
    kk = k_all.shape[1]
    tq = min(ATT_Q_TILE, s)
    nq = s // tq
    kv = lambda bi, h, i: (bi, 0, h)
    const = lambda bi, h, i: (0, 0)
    vec = pl.BlockSpec((1, qk_dim), const)
    wide = pltpu.VMEM((tq, kk), F32)
    stat = pltpu.VMEM((tq, LANES), F32)
    per_parity = [wide, wide, stat, stat]
    return pl.pallas_call(
        functools.partial(_attn_kernel, lam_init=lam_init, qk_dim=qk_dim),
        grid=(b, N_HEADS, nq + 1),
        in_specs=[pl.BlockSpec((1, tq, LANES), lambda bi, h, i: (bi, jnp.minimum(i, nq - 1), h)),
                  pl.BlockSpec((1, kk, LANES), kv), pl.BlockSpec((1, kk, LANES), kv), vec, vec, vec, vec,
                  pl.BlockSpec((1, LANES), const)],
        out_specs=pl.BlockSpec((1, tq, LANES), lambda bi, h, i: (bi, jnp.maximum(i - 1, 0), h)),
        out_shape=jax.ShapeDtypeStruct((b, s, w), BF16),
        scratch_shapes=[pltpu.VMEM((LANES, kk), BF16)] + per_parity + per_parity,
        compiler_params=_cparams(("parallel", "parallel", "arbitrary")),
        name="diff_attention",
    )(q, k_all, v_all, lq1, lk1, lq2, lk2, subln_g)


def _filter_kernel(w1_ref, b1_ref, f1_ref, w2_ref, b2_ref, f2_ref, w3_ref, o_ref, *, seq, hw):
    tl, n = o_ref.shape
    base = pl.program_id(0) * tl
    pos = (lax.broadcasted_iota(I32, (tl, LANES), 0) + base).astype(F32)
    lane = lax.broadcasted_iota(I32, (tl, LANES), 1)
    tn = pos / seq
    band_idx = jnp.where(lane <= N_BANDS, lane - 1, lane - 1 - N_BANDS).astype(F32)
    band = 1e-4 + band_idx * ((N_BANDS - 1 - 1e-4) / (N_BANDS - 1))
    ang = (2.0 * math.pi / seq) * pos * band
    feats = jnp.where(lane == 0, tn,
                      jnp.where(lane <= N_BANDS, jnp.sin(ang),
                                jnp.where(lane < FEAT_DIM, jnp.cos(ang), 0.0)))
    h = jnp.sin(f1_ref[...] * (_dot3(feats, w1_ref[...]) + b1_ref[...]))
    h = jnp.sin(f2_ref[...] * (_dot3(h, w2_ref[...]) + b2_ref[...]))
    h = _dot3(h, w3_ref[...])
    ch = (lax.broadcasted_iota(I32, (tl, n), 1) & ((1 << _log2(hw)) - 1)).astype(F32)
    lo = abs(math.log(DECAY_TARGET) / SLOW_DECAY_PCT)
    hi = abs(math.log(DECAY_TARGET) / FAST_DECAY_PCT)
    delta = lo + ch * ((hi - lo) / (hw - 1))
    tn_n = (lax.broadcasted_iota(I32, (tl, n), 0) + base).astype(F32) / seq
    o_ref[...] = (h * jnp.exp(-tn_n * delta)).astype(BF16)


def _hyena_filters(seq, hw, w1, b1, f1, w2, b2, f2, w3):
    fh = w2.shape[0]
    n = w3.shape[1]
    w1p = jnp.zeros((LANES, fh), F32).at[:FEAT_DIM].set(w1)
    tl = min(ROW_TILE, seq)
    const = lambda i: (0, 0)
    return pl.pallas_call(
        functools.partial(_filter_kernel, seq=seq, hw=hw),
        grid=(seq // tl,),
        in_specs=[pl.BlockSpec((LANES, fh), const), pl.BlockSpec((1, fh), const), pl.BlockSpec((1, fh), const),
                  pl.BlockSpec((fh, fh), const), pl.BlockSpec((1, fh), const), pl.BlockSpec((1, fh), const),
                  pl.BlockSpec((fh, n), const)],
        out_specs=pl.BlockSpec((tl, n), lambda i: (i, 0)),
        out_shape=jax.ShapeDtypeStruct((seq, n), BF16),
        compiler_params=_cparams(("parallel",)),
        name="hyena_filters",
    )(w1p, b1.reshape(1, fh), f1.reshape(1, fh), w2, b2.reshape(1, fh), f2.reshape(1, fh), w3)


def _dft_kernel(mf_ref, mi_ref, tfc, tfs, tic, tis, *, seq):
    r = tfc.shape[0]
    mask = (1 << _log2(4 * seq)) - 1
    unit = math.pi / (2 * seq)
    i_row = lax.broadcasted_iota(I32, (r, seq), 0)
    col = lax.broadcasted_iota(I32, (r, seq), 1)

    @pl.when(pl.program_id(0) == 0)
    def _():
        af = ((2 * i_row * col) & mask).astype(F32) * unit
        tfc[...] = jnp.cos(af)
        tfs[...] = jnp.sin(af)
        ai = (((2 * col + 1) * i_row) & mask).astype(F32) * unit
        tic[...] = jnp.cos(ai)
        tis[...] = jnp.sin(ai)

    r0 = pl.program_id(0) * r
    c1 = lax.broadcasted_iota(I32, (1, seq), 1)
    bf = (((2 * r0 + 1) * c1) & mask).astype(F32) * unit
    bi = (((2 * c1 + 1) * r0) & mask).astype(F32) * unit
    cbf, sbf = jnp.cos(bf), jnp.sin(bf)
    cbi, sbi = jnp.cos(bi), jnp.sin(bi)
    mf_ref[0] = (cbf * tfc[...] - sbf * tfs[...]).astype(BF16)
    mf_ref[1] = (sbf * tfc[...] + cbf * tfs[...]).astype(BF16)
    mi_ref[:, :seq] = (cbi * tic[...] - sbi * tis[...]).astype(BF16)
    mi_ref[:, seq:] = (sbi * tic[...] + cbi * tis[...]).astype(BF16)


def _dft_matrices(seq):
    r = min(DFT_TILE, seq)
    tbl = pltpu.VMEM((r, seq), F32)
    return pl.pallas_call(
        functools.partial(_dft_kernel, seq=seq),
        grid=(seq // r,),
        out_specs=[pl.BlockSpec((2, r, seq), lambda i: (0, i, 0)),
                   pl.BlockSpec((r, 2 * seq), lambda i: (i, 0))],
        out_shape=[jax.ShapeDtypeStruct((2, seq, seq), BF16), jax.ShapeDtypeStruct((seq, 2 * seq), BF16)],
        scratch_shapes=[tbl, tbl, tbl, tbl],
        compiler_params=_cparams(("arbitrary",)),
        name="dft_matrices",
    )()


def _spectrum_kernel(mf_ref, h_ref, g_ref, *, seq, hw):
    hb = h_ref[...]
    hc = _dot(mf_ref[0], hb)
    hs = _dot(mf_ref[1], hb)
    scale = 1.0 / seq
    g_ref[0, 0] = (hc[:, :hw] + hc[:, hw:]) * scale
    g_ref[0, 1] = (hs[:, :hw] - hs[:, hw:]) * scale


def _filter_spectra(mf, hfilt, hw):
    seq = mf.shape[1]
    r = min(DFT_TILE, seq)
    return pl.pallas_call(
        functools.partial(_spectrum_kernel, seq=seq, hw=hw),
        grid=(HYENA_ORDER, seq // r),
        in_specs=[pl.BlockSpec((2, r, seq), lambda n, i: (0, i, 0)),
                  pl.BlockSpec((seq, 2 * hw), lambda n, i: (0, n))],
        out_specs=pl.BlockSpec((1, 2, r, hw), lambda n, i: (n, 0, i, 0)),
        out_shape=jax.ShapeDtypeStruct((HYENA_ORDER, 2, seq, hw), F32),
        compiler_params=_cparams(("parallel", "parallel")),
        name="filter_spectra",
    )(mf, hfilt)


def _short_conv_kernel(u_ref, w_ref, b_ref, o_ref):
    u = u_ref[0]
    s = u.shape[0]
    t = lax.broadcasted_iota(I32, u.shape, 0)
    prev = jnp.where(t == 0, 0.0, pltpu.roll(u, 1, axis=0))
    nxt = jnp.where(t == s - 1, 0.0, pltpu.roll(u, s - 1, axis=0))
    o_ref[0] = b_ref[...] + prev * w_ref[0:1, :] + u * w_ref[1:2, :] + nxt * w_ref[2:3, :]


def _short_conv(u, w, bias):
    b, s, c = u.shape
    tc = min(256, c)
    return pl.pallas_call(
        _short_conv_kernel,
        grid=(b, c // tc),
        in_specs=[pl.BlockSpec((1, s, tc), lambda bi, j: (bi, 0, j)),
                  pl.BlockSpec((SHORT_CONV, tc), lambda bi, j: (0, j)),
                  pl.BlockSpec((1, tc), lambda bi, j: (0, j))],
        out_specs=pl.BlockSpec((1, s, tc), lambda bi, j: (bi, 0, j)),
        out_shape=jax.ShapeDtypeStruct((b, s, c), F32),
        compiler_params=_cparams(("parallel", "parallel")),
        name="short_conv",
    )(u, w, bias.reshape(1, c))


def _fwd_dft_kernel(mf_ref, z_ref, g_ref, y_ref, zb):
    @pl.when(pl.program_id(1) == 0)
    def _():
        zb[...] = z_ref[0].astype(BF16)

    uc = _dot(mf_ref[0], zb[...])
    us = _dot(mf_ref[1], zb[...])
    gc = g_ref[0, 0]
    gs = g_ref[0, 1]
    y_ref[0, 0] = (uc * gc - us * gs).astype(BF16)
    y_ref[0, 1] = (uc * gs + us * gc).astype(BF16)


def _fwd_dft(mf, z, z_col, g, order, hw):
    b, seq = z.shape[0], z.shape[1]
    r = min(DFT_TILE, seq)
    return pl.pallas_call(
        _fwd_dft_kernel,
        grid=(b, seq // r),
        in_specs=[pl.BlockSpec((2, r, seq), lambda bi, i: (0, i, 0)),
                  pl.BlockSpec((1, seq, hw), lambda bi, i: (bi, 0, z_col)),
                  pl.BlockSpec((1, 2, r, hw), lambda bi, i: (order, 0, i, 0))],
        out_specs=pl.BlockSpec((1, 2, r, hw), lambda bi, i: (bi, 0, i, 0)),
        out_shape=jax.ShapeDtypeStruct((b, 2, seq, hw), BF16),
        scratch_shapes=[pltpu.VMEM((seq, hw), BF16)],
        compiler_params=_cparams(("parallel", "arbitrary")),
        name="hyena_fwd_dft",
    )(mf, z, g)


def _inv_dft_kernel(mi_ref, y_ref, z_ref, gate_ref, skip_ref, *rest, final):
    if final:
        og_ref, o_ref = rest
    else:
        (o_ref,) = rest
    conv = _dot(mi_ref[...], y_ref[0])
    z = gate_ref[0] * (conv + z_ref[0] * skip_ref[0])
    if final:
        ms = jnp.mean(z * z, axis=-1, keepdims=True)
        o_ref[0] = (z * lax.rsqrt(ms + EPS) * og_ref[...]).astype(BF16)
    else:
        o_ref[0] = z


def _inv_dft(mi, y, z, z_col, gates, gate_col, skip, order, out_g, hw):
    b, seq = z.shape[0], z.shape[1]
    r = min(DFT_TILE, seq)
    final = out_g is not None
    in_specs = [pl.BlockSpec((r, 2 * seq), lambda bi, i: (i, 0)),
                pl.BlockSpec((1, 2 * seq, hw), lambda bi, i: (bi, 0, 0)),
                pl.BlockSpec((1, r, hw), lambda bi, i: (bi, i, z_col)),
                pl.BlockSpec((1, r, hw), lambda bi, i: (bi, i, gate_col)),
                pl.BlockSpec((1, 1, hw), lambda bi, i: (order, 0, 0))]
    args = [mi, y.reshape(b, 2 * seq, hw), z, gates, skip.reshape(HYENA_ORDER, 1, hw)]
    if final:
        in_specs.append(pl.BlockSpec((1, hw), lambda bi, i: (0, 0)))
        args.append(out_g.reshape(1, hw))
    return pl.pallas_call(
        functools.partial(_inv_dft_kernel, final=final),
        grid=(b, seq // r),
        in_specs=in_specs,
        out_specs=pl.BlockSpec((1, r, hw), lambda bi, i: (bi, i, 0)),
        out_shape=jax.ShapeDtypeStruct((b, seq, hw), BF16 if final else F32),
        compiler_params=_cparams(("parallel", "parallel")),
        name="hyena_inv_dft",
    )(*args)


def _out_kernel(a_ref, hy_ref, x_ref, wo_ref, g1_ref, sh_ref, sc_ref, n2_ref, rw_ref, rb_ref,
                xn_ref, h2_ref, lg_ref):
    aw = a_ref.shape[2]
    mix = _dot(a_ref[0], wo_ref[:aw, :]) + _dot(hy_ref[0], wo_ref[aw:, :])
    xn = x_ref[0] + g1_ref[0] * mix
    xn_ref[0] = xn
    ms = jnp.mean(xn * xn, axis=-1, keepdims=True)
    h2 = (xn * lax.rsqrt(ms + EPS) * n2_ref[...]) * (1.0 + sc_ref[0]) + sh_ref[0]
    _store_row_tiles(h2_ref, h2)
    hh, hl = _split_bf16(h2)
    wh, wl = _split_bf16(rw_ref[...])
    lg_ref[...] = _dot_nt(wh, hh) + (_dot_nt(wh, hl) + _dot_nt(wl, hh)) + rb_ref[...]


def _out_project(attn, hyn, x, wo_bf, g1, sh2, sc2, n2g, rw_t, rb):
    b, s, d = x.shape
    aw, hw = attn.shape[2], hyn.shape[2]
    ne = rw_t.shape[0]
    tm = min(ROW_TILE, s)
    nt = s // tm
    row = lambda bi, i: (bi, i, 0)
    per_b = lambda bi, i: (bi, 0, 0)
    const = lambda bi, i: (0, 0)
    return pl.pallas_call(
        _out_kernel,
        grid=(b, nt),
        in_specs=[pl.BlockSpec((1, tm, aw), row), pl.BlockSpec((1, tm, hw), row), pl.BlockSpec((1, tm, d), row),
                  pl.BlockSpec((aw + hw, d), const),
                  pl.BlockSpec((1, 1, d), per_b), pl.BlockSpec((1, 1, d), per_b), pl.BlockSpec((1, 1, d), per_b),
                  pl.BlockSpec((1, d), const), pl.BlockSpec((ne, d), const), pl.BlockSpec((ne, 1), const)],
        out_specs=[pl.BlockSpec((1, tm, d), row),
                   pl.BlockSpec((tm * (d // LANES), LANES), lambda bi, i: (bi * nt + i, 0)),
                   pl.BlockSpec((ne, tm), lambda bi, i: (0, bi * nt + i))],
        out_shape=[jax.ShapeDtypeStruct((b, s, d), F32), jax.ShapeDtypeStruct((b * s * (d // LANES), LANES), F32),
                   jax.ShapeDtypeStruct((ne, b * s), F32)],
        compiler_params=_cparams(("parallel", "parallel")),
        name="out_project",
    )(attn, hyn, x, wo_bf, g1, sh2, sc2, n2g, rw_t, rb)


def _route_kernel(lg_ref, gate_ref, lpos_ref, tstart_ref, tcnt_ref, cnt_ref, carry):
    ne, tl = lg_ref.shape

    @pl.when(pl.program_id(0) == 0)
    def _():
        carry[...] = jnp.zeros_like(carry)

    l = lg_ref[...]
    rows = lax.broadcasted_iota(I32, (ne, tl), 0).astype(F32)
    vals, sels = [], []
    for k in range(TOP_K):
        m = jnp.max(l, axis=0, keepdims=True)
        ik = jnp.min(jnp.where(l == m, rows, float(ne)), axis=0, keepdims=True)
        sel = rows == ik
        vals.append(m)
        sels.append(sel)
        l = jnp.where(sel, -jnp.inf, l)
    exps = [jnp.exp(v - vals[0]) for v in vals]
    denom = exps[0] + exps[1] + exps[2] + exps[3]
    for k in range(TOP_K):
        gate_ref[k:k + 1, :] = exps[k] / denom
    oh = jnp.zeros((ne, tl), F32)
    for sel in sels:
        oh = oh + jnp.where(sel, 1.0, 0.0)
    r = lax.broadcasted_iota(I32, (tl, tl), 0)
    c = lax.broadcasted_iota(I32, (tl, tl), 1)
    tri = jnp.where(r <= c, 1.0, 0.0).astype(BF16)
    cum = _dot(oh.astype(BF16), tri)
    n_col = jnp.sum(oh, axis=1, keepdims=True)
    er = lax.broadcasted_iota(I32, (ne, LANES), 0)
    ec = lax.broadcasted_iota(I32, (ne, LANES), 1)
    to_lane = lambda col: jnp.sum(jnp.where(er == ec, jnp.broadcast_to(col, (ne, LANES)), 0.0),
                                  axis=0, keepdims=True)
    n_lane = to_lane(n_col)
    off_col = jnp.sum(jnp.where(ec < er, jnp.broadcast_to(n_lane, (ne, LANES)), 0.0), axis=1, keepdims=True)
    slab_pos = cum - oh + off_col
    for k in range(TOP_K):
        lpos_ref[k:k + 1, :] = jnp.sum(jnp.where(sels[k], slab_pos, 0.0), axis=0, keepdims=True).astype(I32)
    tstart_ref[0] = to_lane(carry[:, 0:1]).astype(I32)
    tcnt_ref[0] = n_lane.astype(I32)
    carry[...] = carry[...] + n_col
    cnt_ref[...] = carry[...]


def _route(logits_t, tl):
    ne, t = logits_t.shape
    nt = t // tl
    blk = lambda i: (0, i)
    per_tile = pl.BlockSpec((1, 1, LANES), lambda i: (i, 0, 0))
    return pl.pallas_call(
        _route_kernel,
        grid=(nt,),
        in_specs=[pl.BlockSpec((ne, tl), blk)],
        out_specs=[pl.BlockSpec((TOP_K, tl), blk), pl.BlockSpec((TOP_K, tl), blk), per_tile, per_tile,
                   pl.BlockSpec((ne, LANES), lambda i: (0, 0))],
        out_shape=[jax.ShapeDtypeStruct((TOP_K, t), F32), jax.ShapeDtypeStruct((TOP_K, t), I32),
                   jax.ShapeDtypeStruct((nt, 1, LANES), I32), jax.ShapeDtypeStruct((nt, 1, LANES), I32),
                   jax.ShapeDtypeStruct((ne, LANES), F32)],
        scratch_shapes=[pltpu.VMEM((ne, LANES), F32)],
        compiler_params=_cparams(("arbitrary",)),
        name="moe_route",
    )(logits_t)


def _slots_kernel(cnt_ref, tstart_ref, run_ref, blk_ref, meta_ref, *, rows_per_block):
    ne = cnt_ref.shape[0]
    shift = _log2(rows_per_block)
    cnt = cnt_ref[...].astype(I32)
    padded = ((cnt + (rows_per_block - 1)) >> shift) << shift
    r = lax.broadcasted_iota(I32, (ne, LANES), 0)
    c = lax.broadcasted_iota(I32, (ne, LANES), 1)
    padded_lane = jnp.sum(jnp.where(r == c, padded, 0), axis=0, keepdims=True)
    cnt_lane = jnp.sum(jnp.where(r == c, cnt, 0), axis=0, keepdims=True)
    pend_lane = jnp.sum(jnp.where(r <= c, padded, 0), axis=0, keepdims=True)
    pend_col = jnp.sum(jnp.where(c <= r, jnp.broadcast_to(padded_lane, (ne, LANES)), 0),
                       axis=1, keepdims=True)
    run_ref[...] = tstart_ref[...] + (pend_lane - padded_lane)
    nbp = blk_ref.shape[1]
    j0 = lax.broadcasted_iota(I32, (ne, nbp), 1) * rows_per_block
    be = jnp.sum(jnp.where(jnp.broadcast_to(pend_col, (ne, nbp)) <= j0, 1, 0), axis=0, keepdims=True)
    blk_ref[...] = jnp.minimum(be, ne - 1)
    total = jnp.max(pend_col, axis=0, keepdims=True)
    meta_ref[0:1, :] = pend_lane - padded_lane + cnt_lane
    meta_ref[1:2, :] = padded_lane - cnt_lane
    meta_ref[2:3, :] = jnp.broadcast_to(total >> shift, (1, LANES))
    meta_ref[3:8, :] = jnp.zeros((5, LANES), I32)


def _slots(cnt, tstart, n_blocks, rows_per_block):
    nbp = -(-n_blocks // LANES) * LANES
    return pl.pallas_call(
        functools.partial(_slots_kernel, rows_per_block=rows_per_block),
        out_shape=[jax.ShapeDtypeStruct(tstart.shape, I32), jax.ShapeDtypeStruct((1, nbp), I32),
                   jax.ShapeDtypeStruct((8, LANES), I32)],
        compiler_params=pltpu.CompilerParams(vmem_limit_bytes=V7X_VMEM_LIMIT),
        name="moe_slots",
    )(cnt, tstart)


def _pad_chunks(rows_per_block):
    sizes, s = [], rows_per_block // 2
    while s >= 1:
        sizes.append(s)
        s //= 2
    return sizes


def _rows(start, size, rt):
    return pl.ds(pl.multiple_of(start * rt, rt), size * rt)


def _for_each_run_chunk(run_ref, cnt_ref, tile, ne, max_rows, act):
    sizes = _pad_chunks(2 * max_rows)

    def each(e, off):
        left = cnt_ref[tile * ne + e]
        pos, start = off, run_ref[tile * ne + e]
        for size in sizes:
            hit = (left & size) != 0

            @pl.when(hit)
            def _():
                act(pos, start, size)

            inc = jnp.where(hit, size, 0)
            pos, start = pos + inc, start + inc
        return off + left

    lax.fori_loop(0, ne, each, 0)


def _dispatch_kernel(run_ref, cnt_ref, padlo_ref, npad_ref, nused_ref, lpos_ref, h_ref, buf_ref,
                     slab, zeros, sems, zsem, *, rows_per_block, rt, n_tiles):
    tl = lpos_ref.shape[1]
    ne = padlo_ref.shape[0]
    sizes = _pad_chunks(rows_per_block)
    half = rows_per_block // 2
    i = pl.program_id(0)
    slot = i & 1

    def pad_copy(start, size):
        return pltpu.make_async_copy(zeros.at[_rows(0, size, rt)], buf_ref.at[_rows(start, size, rt)], zsem)

    def run_copies(tile, sl, wait):
        def act(pos, start, size):
            cp = pltpu.make_async_copy(slab.at[sl, _rows(pos, size, rt)], buf_ref.at[_rows(start, size, rt)],
                                       sems.at[sl])
            if wait:
                cp.wait()
            else:
                cp.start()

        _for_each_run_chunk(run_ref, cnt_ref, tile, ne, tl, act)

    @pl.when(i == 0)
    def _():
        zeros[...] = jnp.zeros_like(zeros)

        first, last = 2 * nused_ref[0], buf_ref.shape[0] // (half * rt)
        lax.fori_loop(first, last, lambda j, c: (pad_copy(j * half, half).start(), c)[1], 0)
        lax.fori_loop(first, last, lambda j, c: (pad_copy(j * half, half).wait(), c)[1], 0)

        def each(e, wait):
            start = padlo_ref[e]
            left = npad_ref[e]
            for size in sizes:
                hit = (left & size) != 0

                @pl.when(hit)
                def _():
                    cp = pad_copy(start, size)
                    if wait:
                        cp.wait()
                    else:
                        cp.start()

                start = start + jnp.where(hit, size, 0)

        lax.fori_loop(0, ne, lambda e, c: (each(e, False), c)[1], 0)
        lax.fori_loop(0, ne, lambda e, c: (each(e, True), c)[1], 0)

    @pl.when(i >= 2)
    def _():
        run_copies(i - 2, slot, True)

    def fill(t, c):
        row = h_ref[_rows(t, 1, rt), :]
        for k in range(TOP_K):
            slab[slot, _rows(lpos_ref[k, t], 1, rt), :] = row
        return c

    lax.fori_loop(0, tl, fill, 0, unroll=8)
    run_copies(i, slot, False)

    @pl.when(i == n_tiles - 1)
    def _():
        if n_tiles >= 2:
            run_copies(i - 1, 1 - slot, True)
        run_copies(i, slot, True)


def _dispatch(h2r, lpos, run_start, run_cnt, pad_lo, n_pad, n_used, n_rows, rows_per_block, rt, tl):
    n_tiles = lpos.shape[1] // tl
    return pl.pallas_call(
        functools.partial(_dispatch_kernel, rows_per_block=rows_per_block, rt=rt, n_tiles=n_tiles),
        grid_spec=pltpu.PrefetchScalarGridSpec(
            num_scalar_prefetch=5,
            grid=(n_tiles,),
            in_specs=[pl.BlockSpec((TOP_K, tl), lambda i, *_: (0, i), memory_space=pltpu.SMEM),
                      pl.BlockSpec((tl * rt, LANES), lambda i, *_: (i, 0))],
            out_specs=pl.BlockSpec(memory_space=pl.ANY),
            scratch_shapes=[pltpu.VMEM((2, TOP_K * tl * rt, LANES), F32),
                            pltpu.VMEM((rows_per_block // 2 * rt, LANES), F32),
                            pltpu.SemaphoreType.DMA((2,)), pltpu.SemaphoreType.DMA(())]),
        out_shape=jax.ShapeDtypeStruct((n_rows * rt, LANES), F32),
        compiler_params=_cparams(("arbitrary",)),
        name="moe_dispatch",
    )(run_start, run_cnt, pad_lo, n_pad, n_used, lpos, h2r)


def _expert_kernel(be_ref, nu_ref, x_ref, w1_ref, b1_ref, w2_ref, b2_ref, o_ref, w1b, w2b):
    j = pl.program_id(0)
    active = j < nu_ref[0]

    @pl.when(active & ((j == 0) | (be_ref[j] != be_ref[jnp.maximum(j - 1, 0)])))
    def _():
        w1b[...] = w1_ref[0].astype(BF16)
        w2b[...] = w2_ref[0].astype(BF16)

    @pl.when(active)
    def _():
        de = w2b.shape[0]
        gl = _dot(_load_row_tiles(x_ref, w1b.shape[0] // LANES).astype(BF16), w1b[...]) + b1_ref[0]
        g = jnp.minimum(gl[:, :de], SWIGLU_LIMIT)
        lin = jnp.clip(gl[:, de:], -SWIGLU_LIMIT, SWIGLU_LIMIT)
        glu = g * jax.nn.sigmoid(SWIGLU_ALPHA * g)
        _store_row_tiles(o_ref, _dot(((lin + 1.0) * glu).astype(BF16), w2b[...]) + b2_ref[0])

    @pl.when(pl.program_id(0) >= nu_ref[0])
    def _():
        o_ref[...] = jnp.zeros_like(o_ref)


def _experts(buf, block_e, n_used, w1, b1, w2, b2, rows_per_block):
    ne, d, d2 = w1.shape
    de = w2.shape[1]
    blk_shape = (rows_per_block * (d // LANES), LANES)
    nb = buf.shape[0] // blk_shape[0]
    rowblk = lambda j, be, nu: (jnp.minimum(j, nu[0] - 1), 0)
    by_e = lambda j, be, nu: (be[j], 0, 0)
    return pl.pallas_call(
        _expert_kernel,
        grid_spec=pltpu.PrefetchScalarGridSpec(
            num_scalar_prefetch=2,
            grid=(nb,),
            in_specs=[pl.BlockSpec(blk_shape, rowblk),
                      pl.BlockSpec((1, d, d2), by_e), pl.BlockSpec((1, 1, d2), by_e),
                      pl.BlockSpec((1, de, d), by_e), pl.BlockSpec((1, 1, d), by_e)],
            out_specs=pl.BlockSpec(blk_shape, lambda j, be, nu: (j, 0)),
            scratch_shapes=[pltpu.VMEM((d, d2), BF16), pltpu.VMEM((de, d), BF16)]),
        out_shape=jax.ShapeDtypeStruct(buf.shape, F32),
        compiler_params=_cparams(("arbitrary",)),
        name="moe_experts",
    )(block_e, n_used, buf, w1, b1.reshape(ne, 1, d2), w2, b2.reshape(ne, 1, d))


def _combine_kernel(run_ref, cnt_ref, lpos_ref, gate_ref, xn_ref, g2_ref, ob_ref, o_ref, slab, acc, sems,
                    *, rt, ne, n_tiles):
    tl = lpos_ref.shape[1]
    i = pl.program_id(0) * pl.num_programs(1) + pl.program_id(1)
    slot = i & 1

    def run_copies(tile, sl, wait):
        def act(pos, start, size):
            cp = pltpu.make_async_copy(ob_ref.at[_rows(start, size, rt)], slab.at[sl, _rows(pos, size, rt)],
                                       sems.at[sl])
            if wait:
                cp.wait()
            else:
                cp.start()

        _for_each_run_chunk(run_ref, cnt_ref, tile, ne, tl, act)

    @pl.when(i == 0)
    def _():
        run_copies(i, slot, False)

    @pl.when(i + 1 < n_tiles)
    def _():
        run_copies(i + 1, 1 - slot, False)

    run_copies(i, slot, True)

    def token(t, c):
        a = gate_ref[0, t] * slab[slot, _rows(lpos_ref[0, t], 1, rt), :]
        for k in range(1, TOP_K):
            a = a + gate_ref[k, t] * slab[slot, _rows(lpos_ref[k, t], 1, rt), :]
        acc[_rows(t, 1, rt), :] = a
        return c

    lax.fori_loop(0, tl, token, 0, unroll=8)
    o_ref[0] = xn_ref[0] + g2_ref[0] * _load_row_tiles(acc, rt)


def _combine(out_buf, lpos, gates, run_start, run_cnt, xn, g2, rt, tl):
    b, s, d = xn.shape
    nt = s // tl
    tok = lambda bi, i, *_: (0, bi * nt + i)
    return pl.pallas_call(
        functools.partial(_combine_kernel, rt=rt, ne=N_EXPERTS, n_tiles=b * nt),
        grid_spec=pltpu.PrefetchScalarGridSpec(
            num_scalar_prefetch=2,
            grid=(b, nt),
            in_specs=[pl.BlockSpec((TOP_K, tl), tok, memory_space=pltpu.SMEM),
                      pl.BlockSpec((TOP_K, tl), tok, memory_space=pltpu.SMEM),
                      pl.BlockSpec((1, tl, d), lambda bi, i, *_: (bi, i, 0)),
                      pl.BlockSpec((1, 1, d), lambda bi, i, *_: (bi, 0, 0)),
                      pl.BlockSpec(memory_space=pl.ANY)],
            out_specs=pl.BlockSpec((1, tl, d), lambda bi, i, *_: (bi, i, 0)),
            scratch_shapes=[pltpu.VMEM((2, TOP_K * tl * rt, LANES), F32), pltpu.VMEM((tl * rt, LANES), F32),
                            pltpu.SemaphoreType.DMA((2,))]),
        out_shape=jax.ShapeDtypeStruct((b, s, d), F32),
        compiler_params=_cparams(("arbitrary", "arbitrary")),
        name="moe_combine",
    )(run_start, run_cnt, lpos, gates, xn, g2, out_buf)


def _moe(h2r, logits_t, xn, g2, w1, b1, w2, b2):
    b, s, d = xn.shape
    t = b * s
    rt = d // LANES
    tl = min(TOKEN_TILE, s)
    assert s % tl == 0
    n_blocks = (t * TOP_K) // EXPERT_ROWS + N_EXPERTS
    n_rows = n_blocks * EXPERT_ROWS
    gates, lpos, tstart, tcnt, cnt = _route(logits_t, tl)
    run, blk, meta = _slots(cnt, tstart, n_blocks, EXPERT_ROWS)
    run_start = run[:, 0, :N_EXPERTS].reshape(-1)
    run_cnt = tcnt[:, 0, :N_EXPERTS].reshape(-1)
    buf = _dispatch(h2r, lpos, run_start, run_cnt, meta[0, :N_EXPERTS], meta[1, :N_EXPERTS], meta[2, :1],
                    n_rows, EXPERT_ROWS, rt, tl)
    out_buf = _experts(buf, blk[0, :n_blocks], meta[2, :1], w1, b1, w2, b2, EXPERT_ROWS)
    return _combine(out_buf, lpos, gates, run_start, run_cnt, xn, g2, rt, tl)


def _layer(x, ctx, c, c_ctx, p, lam_init):
    b, s, d = x.shape
    attn_w = d // 2
    hw = d - attn_w
    v_dim = attn_w // N_HEADS
    qk_dim = v_dim // 2
    qk_cols = N_HEADS * 2 * qk_dim
    v_cols = N_HEADS * v_dim
    assert 2 * qk_dim == LANES and v_dim == LANES and s % GRID_W == 0

    rows = -(-(b + 1) // 8) * 8
    cc = jnp.zeros((rows, d), F32).at[:b].set(c).at[b].set(c_ctx)
    mod = _modulation(cc, p['w_mod'], p['b_mod'])
    mod_x = mod[:b].reshape(b, N_MOD, 1, d)
    sh1, sc1, g1, sh2, sc2, g2 = [mod_x[:, i] for i in range(N_MOD)]
    mod_c = mod[b:b + 1].reshape(1, N_MOD, 1, d)
    csh1, csc1 = mod_c[:, 0], mod_c[:, 1]

    w_in_bf = p['w_in'].astype(BF16)
    qg = jnp.tile(p['q_norm_g'], qk_cols // qk_dim).reshape(1, qk_cols)
    kg = jnp.tile(p['k_norm_g'], qk_cols // qk_dim).reshape(1, qk_cols)
    n1g = p['norm1_g'].reshape(1, d)
    cos_t, sin_t = _rope_tables(s, qk_dim)
    q, k, v, u_hy = _project_latent(x, sh1, sc1, n1g, w_in_bf, qg, kg, cos_t, sin_t, qk_cols, v_cols, qk_dim)
    k_c, v_c = _project_context(ctx, csh1, csc1, n1g, w_in_bf[:, qk_cols:2 * qk_cols + v_cols], kg,
                                qk_cols, v_cols, qk_dim)
    k_all = jnp.concatenate([k_c, k], axis=1)
    v_all = jnp.concatenate([v_c, v], axis=1)
    vec = lambda a: a.reshape(1, qk_dim)
    attn = _diff_attention(q, k_all, v_all, vec(p['lam_q1']), vec(p['lam_k1']), vec(p['lam_q2']),
                           vec(p['lam_k2']), p['subln_g'].reshape(1, v_dim), lam_init, qk_dim)

    hfilt = _hyena_filters(s, hw, p['hy_w1'], p['hy_b1'], p['hy_f1'], p['hy_w2'], p['hy_b2'], p['hy_f2'], p['hy_w3'])
    mf, mi = _dft_matrices(s)
    g_spec = _filter_spectra(mf, hfilt, hw)
    uc = _short_conv(u_hy, p['hy_conv_w'], p['hy_conv_b'])
    y1 = _fwd_dft(mf, uc, 0, g_spec, 0, hw)
    z1 = _inv_dft(mi, y1, uc, 0, uc, 1, p['hy_skip'], 0, None, hw)
    y2 = _fwd_dft(mf, z1, 0, g_spec, 1, hw)
    hyn = _inv_dft(mi, y2, z1, 0, uc, 2, p['hy_skip'], 1, p['hy_out_g'], hw)

    xn, h2, logits_t = _out_project(attn, hyn, x, p['w_out'].astype(BF16), g1, sh2, sc2,
                                    p['norm2_g'].reshape(1, d), p['router_w'].T,
                                    p['router_b'].reshape(N_EXPERTS, 1))
    return _moe(h2, logits_t, xn, g2, p['exp_w1'], p['exp_b1'], p['exp_w2'], p['exp_b2'])


def kernel(x, c, ctx, c_ctx, w_mod, b_mod, norm1_g, norm2_g, w_in, q_norm_g, k_norm_g, lam_q1, lam_k1, lam_q2, lam_k2, subln_g, hy_conv_w, hy_conv_b, hy_w1, hy_b1, hy_f1, hy_w2, hy_b2, hy_f2, hy_w3, hy_skip, hy_out_g, w_out, router_w, router_b, exp_w1, exp_b1, exp_w2, exp_b2):
    depth = w_mod.shape[0]
    assert depth == 1, "context-token update between layers is not implemented"
    p = {
        'w_mod': w_mod[0], 'b_mod': b_mod[0], 'norm1_g': norm1_g[0], 'norm2_g': norm2_g[0],
        'w_in': w_in[0], 'q_norm_g': q_norm_g[0], 'k_norm_g': k_norm_g[0],
        'lam_q1': lam_q1[0], 'lam_k1': lam_k1[0], 'lam_q2': lam_q2[0], 'lam_k2': lam_k2[0],
        'subln_g': subln_g[0], 'hy_conv_w': hy_conv_w[0], 'hy_conv_b': hy_conv_b[0],
        'hy_w1': hy_w1[0], 'hy_b1': hy_b1[0], 'hy_f1': hy_f1[0], 'hy_w2': hy_w2[0],
        'hy_b2': hy_b2[0], 'hy_f2': hy_f2[0], 'hy_w3': hy_w3[0], 'hy_skip': hy_skip[0],
        'hy_out_g': hy_out_g[0], 'w_out': w_out[0], 'router_w': router_w[0],
        'router_b': router_b[0], 'exp_w1': exp_w1[0], 'exp_b1': exp_b1[0],
        'exp_w2': exp_w2[0], 'exp_b2': exp_b2[0],
    }
    lam_init = 0.8 - 0.6 * math.exp(-0.3 * 0)
    return _layer(x, ctx, c, c_ctx, p, lam_init)
```

```python
import functools
import math

import jax
import jax.numpy as jnp
from jax import lax
from jax.experimental import pallas as pl
from jax.experimental.pallas import tpu as pltpu

F32 = jnp.float32
BF16 = jnp.bfloat16
I32 = jnp.int32

GRID_W = 64
N_HEADS = 4
N_MOD = 6
SHORT_CONV = 3
HYENA_ORDER = 2
N_BANDS = 8
FEAT_DIM = 1 + 2 * N_BANDS
FILTER_HIDDEN = 64
DECAY_TARGET = 1e-2
FAST_DECAY_PCT = 0.3
SLOW_DECAY_PCT = 1.5
N_EXPERTS = 32
TOP_K = 4
SWIGLU_LIMIT = 7.0
SWIGLU_ALPHA = 1.702
ROPE_BASE = 10000.0
EPS = 1e-6

LANES = 128
V7X_VMEM_LIMIT = 56 * 1024 * 1024

ROW_TILE = 512
ATT_Q_TILE = 256
ATT_KEY_CHUNKS = 17
DFT_TILE = 256
RADIX = 4
EXPERT_ROWS = 512
TOKEN_TILE = 512


def _log2(n):
    assert n > 0 and n & (n - 1) == 0, f"{n} must be a power of two"
    return n.bit_length() - 1


def _cparams(sem, vmem=V7X_VMEM_LIMIT):
    return pltpu.CompilerParams(dimension_semantics=sem, vmem_limit_bytes=vmem)


def _split_bf16(a):
    hi = a.astype(BF16)
    lo = (a - hi.astype(F32)).astype(BF16)
    return hi, lo


def _dot(a, b):
    return jnp.dot(a, b, preferred_element_type=F32)


def _dot_nt(a, b):
    return lax.dot_general(a, b, (((1,), (1,)), ((), ())), preferred_element_type=F32)


def _store_row_tiles(ref, val):
    rows, d = val.shape
    rt = d // LANES
    for c in range(rt):
        ref[pl.ds(c, rows, stride=rt), :] = val[:, c * LANES:(c + 1) * LANES]


def _load_row_tiles(ref, rt):
    rows = ref.shape[0] // rt
    return jnp.concatenate([ref[pl.ds(c, rows, stride=rt), :] for c in range(rt)], axis=1)


def _dot3(a, b):
    ah, al = _split_bf16(a)
    bh, bl = _split_bf16(b)
    return _dot(ah, bh) + (_dot(ah, bl) + _dot(al, bh))


def _mod_kernel(c_ref, w_ref, b_ref, o_ref):
    c = c_ref[...]
    s = c * jax.nn.sigmoid(c)
    o_ref[...] = _dot3(s, w_ref[...]) + b_ref[...]


def _modulation(cc, w_mod, b_mod):
    rows, d = cc.shape
    n = w_mod.shape[1]
    tn = min(n, 1536)
    return pl.pallas_call(
        _mod_kernel,
        grid=(n // tn,),
        in_specs=[pl.BlockSpec((rows, d), lambda j: (0, 0)),
                  pl.BlockSpec((d, tn), lambda j: (0, j)),
                  pl.BlockSpec((1, tn), lambda j: (0, j))],
        out_specs=pl.BlockSpec((rows, tn), lambda j: (0, j)),
        out_shape=jax.ShapeDtypeStruct((rows, n), F32),
        compiler_params=_cparams(("parallel",)),
        name="modulation",
    )(cc, w_mod, b_mod.reshape(1, n))


def _rope_table_kernel(cos_ref, sin_ref, *, qk_dim):
    s, w = cos_ref.shape
    half = qk_dim // 2
    nf = half // 2
    t = lax.broadcasted_iota(I32, (s, w), 0)
    lane = lax.broadcasted_iota(I32, (s, w), 1)
    d = lane & (qk_dim - 1)
    j = d & (nf - 1)
    row = t >> _log2(GRID_W)
    col = t & (GRID_W - 1)
    pos = jnp.where(d < half, row, col).astype(F32)
    inv = jnp.exp(j.astype(F32) * (-math.log(ROPE_BASE) / nf))
    ang = pos * inv
    first = (d & (half - 1)) < nf
    cos_ref[...] = jnp.cos(ang)
    sn = jnp.sin(ang)
    sin_ref[...] = jnp.where(first, -sn, sn)


def _rope_tables(s, qk_dim):
    return pl.pallas_call(
        functools.partial(_rope_table_kernel, qk_dim=qk_dim),
        out_shape=(jax.ShapeDtypeStruct((s, LANES), F32), jax.ShapeDtypeStruct((s, LANES), F32)),
        name="rope_tables",
    )()


def _group_rms(t, gain, qk_dim):
    w = t.shape[1]
    r = lax.broadcasted_iota(I32, (w, w), 0) >> _log2(qk_dim)
    c = lax.broadcasted_iota(I32, (w, w), 1) >> _log2(qk_dim)
    bd = jnp.where(r == c, 1.0 / qk_dim, 0.0).astype(BF16)
    hi, lo = _split_bf16(t * t)
    ms = _dot(hi, bd) + _dot(lo, bd)
    return t * lax.rsqrt(ms + EPS) * gain


def _rope(t, cos, sin_signed, qk_dim):
    w = t.shape[1]
    nf = qk_dim // 4
    lane = lax.broadcasted_iota(I32, t.shape, 1)
    first = (lane & (2 * nf - 1)) < nf
    partner = jnp.where(first, pltpu.roll(t, w - nf, axis=1), pltpu.roll(t, nf, axis=1))
    return t * cos + partner * sin_signed


def _proj_kernel(*refs, latent, qk_cols, v_cols, qk_dim):
    if latent:
        (x_ref, sh_ref, sc_ref, g_ref, w_ref, qg_ref, kg_ref, cos_ref, sin_ref,
         q_out, k_out, v_out, u_out) = refs
    else:
        x_ref, sh_ref, sc_ref, g_ref, w_ref, kg_ref, k_out, v_out = refs
    x = x_ref[0]
    ms = jnp.mean(x * x, axis=-1, keepdims=True)
    h = (x * lax.rsqrt(ms + EPS) * g_ref[...]) * (1.0 + sc_ref[0]) + sh_ref[0]
    proj = _dot(h.astype(BF16), w_ref[...])
    if latent:
        reps = qk_cols // LANES
        cos = jnp.concatenate([cos_ref[...]] * reps, axis=1)
        sin = jnp.concatenate([sin_ref[...]] * reps, axis=1)
        q = _rope(_group_rms(proj[:, :qk_cols], qg_ref[...], qk_dim), cos, sin, qk_dim)
        q_out[0] = (q * (qk_dim ** -0.5 * math.log2(math.e))).astype(BF16)
        k = _rope(_group_rms(proj[:, qk_cols:2 * qk_cols], kg_ref[...], qk_dim), cos, sin, qk_dim)
        k_out[0] = k.astype(BF16)
        v_out[0] = proj[:, 2 * qk_cols:2 * qk_cols + v_cols].astype(BF16)
        u_out[0] = proj[:, 2 * qk_cols + v_cols:]
    else:
        k = _group_rms(proj[:, :qk_cols], kg_ref[...], qk_dim)
        k_out[0] = k.astype(BF16)
        v_out[0] = proj[:, qk_cols:qk_cols + v_cols].astype(BF16)


def _project_latent(x, sh, sc, g, w_bf, qg, kg, cos_t, sin_t, qk_cols, v_cols, qk_dim):
    b, s, d = x.shape
    n = w_bf.shape[1]
    hy_cols = n - 2 * qk_cols - v_cols
    tm = min(ROW_TILE, s)
    row = lambda bi, i: (bi, i, 0)
    per_b = lambda bi, i: (bi, 0, 0)
    const = lambda bi, i: (0, 0)
    return pl.pallas_call(
        functools.partial(_proj_kernel, latent=True, qk_cols=qk_cols, v_cols=v_cols, qk_dim=qk_dim),
        grid=(b, s // tm),
        in_specs=[pl.BlockSpec((1, tm, d), row),
                  pl.BlockSpec((1, 1, d), per_b), pl.BlockSpec((1, 1, d), per_b),
                  pl.BlockSpec((1, d), const), pl.BlockSpec((d, n), const),
                  pl.BlockSpec((1, qk_cols), const), pl.BlockSpec((1, qk_cols), const),
                  pl.BlockSpec((tm, LANES), lambda bi, i: (i, 0)),
                  pl.BlockSpec((tm, LANES), lambda bi, i: (i, 0))],
        out_specs=[pl.BlockSpec((1, tm, qk_cols), row), pl.BlockSpec((1, tm, qk_cols), row),
                   pl.BlockSpec((1, tm, v_cols), row), pl.BlockSpec((1, tm, hy_cols), row)],
        out_shape=[jax.ShapeDtypeStruct((b, s, qk_cols), BF16), jax.ShapeDtypeStruct((b, s, qk_cols), BF16),
                   jax.ShapeDtypeStruct((b, s, v_cols), BF16), jax.ShapeDtypeStruct((b, s, hy_cols), F32)],
        compiler_params=_cparams(("parallel", "parallel")),
        name="project_latent",
    )(x, sh, sc, g, w_bf, qg, kg, cos_t, sin_t)


def _project_context(ctx, sh, sc, g, w_bf, kg, qk_cols, v_cols, qk_dim):
    b, lc, d = ctx.shape
    n = w_bf.shape[1]
    tm = min(ROW_TILE, lc)
    row = lambda bi, i: (bi, i, 0)
    shared = lambda bi, i: (0, 0, 0)
    const = lambda bi, i: (0, 0)
    return pl.pallas_call(
        functools.partial(_proj_kernel, latent=False, qk_cols=qk_cols, v_cols=v_cols, qk_dim=qk_dim),
        grid=(b, lc // tm),
        in_specs=[pl.BlockSpec((1, tm, d), row),
                  pl.BlockSpec((1, 1, d), shared), pl.BlockSpec((1, 1, d), shared),
                  pl.BlockSpec((1, d), const), pl.BlockSpec((d, n), const),
                  pl.BlockSpec((1, qk_cols), const)],
        out_specs=[pl.BlockSpec((1, tm, qk_cols), row), pl.BlockSpec((1, tm, v_cols), row)],
        out_shape=[jax.ShapeDtypeStruct((b, lc, qk_cols), BF16), jax.ShapeDtypeStruct((b, lc, v_cols), BF16)],
        compiler_params=_cparams(("parallel", "parallel")),
        name="project_context",
    )(ctx, sh, sc, g, w_bf, kg)


def _key_chunks(kk, n):
    tiles = kk // LANES
    n = min(n, tiles)
    return [(LANES * (i * tiles // n), LANES * ((i + 1) * tiles // n)) for i in range(n)]


def _attn_kernel(q_ref, k_ref, v_ref, lq1, lk1, lq2, lk2, sg_ref, o_ref, kt_ref, *bufs, lam_init, qk_dim):
    lam = (jnp.exp(jnp.sum(lq1[...] * lk1[...], axis=-1, keepdims=True))
           - jnp.exp(jnp.sum(lq2[...] * lk2[...], axis=-1, keepdims=True)) + lam_init)
    j = pl.program_id(2)
    even, odd = bufs[:4], bufs[4:]
    tq = q_ref.shape[1]
    chunks = _key_chunks(kt_ref.shape[1], ATT_KEY_CHUNKS)

    @pl.when(j == 0)
    def _():
        kt_ref[...] = k_ref[0].T
        for ref in odd:
            ref[...] = jnp.zeros_like(ref)

    def step(cur, prev):
        s1_w, s2_w, m1_w, m2_w = cur
        s1_r, s2_r, m1_r, m2_r = prev
        q = q_ref[0]
        lane = lax.broadcasted_iota(I32, q.shape, 1)
        q1 = jnp.where(lane < qk_dim, q, jnp.zeros_like(q))
        q2 = jnp.where(lane >= qk_dim, q, jnp.zeros_like(q))
        m1p, m2p = m1_r[:, :1], m2_r[:, :1]
        m1 = m2 = jnp.full((tq, 1), -jnp.inf, F32)
        v = v_ref[0]
        v1 = jnp.concatenate([v, jnp.ones_like(v)], axis=1)
        o1 = o2 = jnp.zeros((tq, 2 * LANES), F32)
        for c0, c1 in chunks:
            s1 = _dot(q1, kt_ref[:, c0:c1])
            s2 = _dot(q2, kt_ref[:, c0:c1])
            s1_w[:, c0:c1] = s1
            s2_w[:, c0:c1] = s2
            m1 = jnp.maximum(m1, jnp.max(s1, axis=-1, keepdims=True))
            m2 = jnp.maximum(m2, jnp.max(s2, axis=-1, keepdims=True))

            e1 = jnp.exp2(s1_r[:, c0:c1] - m1p)
            e2 = jnp.exp2(s2_r[:, c0:c1] - m2p)
            o1 = o1 + _dot(e1.astype(BF16), v1[c0:c1, :])
            o2 = o2 + _dot(e2.astype(BF16), v1[c0:c1, :])
        m1_w[...] = jnp.broadcast_to(m1, m1_w.shape)
        m2_w[...] = jnp.broadcast_to(m2, m2_w.shape)
        o = o1[:, :LANES] / o1[:, LANES:] - o2[:, :LANES] * (lam / o2[:, LANES:])
        ms = jnp.mean(o * o, axis=-1, keepdims=True)
        o_ref[0] = ((o * lax.rsqrt(ms + EPS) * sg_ref[...]) * (1.0 - lam_init)).astype(BF16)

    @pl.when((j & 1) == 0)
    def _():
        step(even, odd)

    @pl.when((j & 1) == 1)
    def _():
        step(odd, even)


def _diff_attention(q, k_all, v_all, lq1, lk1, lq2, lk2, subln_g, lam_init, qk_dim):
    b, s, w = q.shape
    kk = k_all.shape[1]
    tq = min(ATT_Q_TILE, s)
    nq = s // tq
    kv = lambda bi, h, i: (bi, 0, h)
    const = lambda bi, h, i: (0, 0)
    vec = pl.BlockSpec((1, qk_dim), const)
    wide = pltpu.VMEM((tq, kk), F32)
    stat = pltpu.VMEM((tq, LANES), F32)
    per_parity = [wide, wide, stat, stat]
    return pl.pallas_call(
        functools.partial(_attn_kernel, lam_init=lam_init, qk_dim=qk_dim),
        grid=(b, N_HEADS, nq + 1),
        in_specs=[pl.BlockSpec((1, tq, LANES), lambda bi, h, i: (bi, jnp.minimum(i, nq - 1), h)),
                  pl.BlockSpec((1, kk, LANES), kv), pl.BlockSpec((1, kk, LANES), kv), vec, vec, vec, vec,
                  pl.BlockSpec((1, LANES), const)],
        out_specs=pl.BlockSpec((1, tq, LANES), lambda bi, h, i: (bi, jnp.maximum(i - 1, 0), h)),
        out_shape=jax.ShapeDtypeStruct((b, s, w), BF16),
        scratch_shapes=[pltpu.VMEM((LANES, kk), BF16)] + per_parity + per_parity,
        compiler_params=_cparams(("parallel", "parallel", "arbitrary")),
        name="diff_attention",
    )(q, k_all, v_all, lq1, lk1, lq2, lk2, subln_g)


def _filter_kernel(w1_ref, b1_ref, f1_ref, w2_ref, b2_ref, f2_ref, w3_ref, o_ref, *, seq, hw):
    tl, n = o_ref.shape
    base = pl.program_id(0) * tl
    quarter = seq // RADIX

    def position(shape):
        p = lax.broadcasted_iota(I32, shape, 0) + base
        return (((p & (quarter - 1)) << _log2(RADIX)) | (p >> _log2(quarter))).astype(F32)

    pos = position((tl, LANES))
    lane = lax.broadcasted_iota(I32, (tl, LANES), 1)
    tn = pos / seq
    band_idx = jnp.where(lane <= N_BANDS, lane - 1, lane - 1 - N_BANDS).astype(F32)
    band = 1e-4 + band_idx * ((N_BANDS - 1 - 1e-4) / (N_BANDS - 1))
    ang = (2.0 * math.pi / seq) * pos * band
    feats = jnp.where(lane == 0, tn,
                      jnp.where(lane <= N_BANDS, jnp.sin(ang),
                                jnp.where(lane < FEAT_DIM, jnp.cos(ang), 0.0)))
    h = jnp.sin(f1_ref[...] * (_dot3(feats, w1_ref[...]) + b1_ref[...]))
    h = jnp.sin(f2_ref[...] * (_dot3(h, w2_ref[...]) + b2_ref[...]))
    h = _dot3(h, w3_ref[...])
    ch = (lax.broadcasted_iota(I32, (tl, n), 1) & ((1 << _log2(hw)) - 1)).astype(F32)
    lo = abs(math.log(DECAY_TARGET) / SLOW_DECAY_PCT)
    hi = abs(math.log(DECAY_TARGET) / FAST_DECAY_PCT)
    delta = lo + ch * ((hi - lo) / (hw - 1))
    o_ref[...] = (h * jnp.exp(-(position((tl, n)) / seq) * delta)).astype(BF16)


def _hyena_filters(seq, hw, w1, b1, f1, w2, b2, f2, w3):
    fh = w2.shape[0]
    n = w3.shape[1]
    w1p = jnp.zeros((LANES, fh), F32).at[:FEAT_DIM].set(w1)
    tl = min(ROW_TILE, seq)
    const = lambda i: (0, 0)
    return pl.pallas_call(
        functools.partial(_filter_kernel, seq=seq, hw=hw),
        grid=(seq // tl,),
        in_specs=[pl.BlockSpec((LANES, fh), const), pl.BlockSpec((1, fh), const), pl.BlockSpec((1, fh), const),
                  pl.BlockSpec((fh, fh), const), pl.BlockSpec((1, fh), const), pl.BlockSpec((1, fh), const),
                  pl.BlockSpec((fh, n), const)],
        out_specs=pl.BlockSpec((tl, n), lambda i: (i, 0)),
        out_shape=jax.ShapeDtypeStruct((seq, n), BF16),
        compiler_params=_cparams(("parallel",)),
        name="hyena_filters",
    )(w1p, b1.reshape(1, fh), f1.reshape(1, fh), w2, b2.reshape(1, fh), f2.reshape(1, fh), w3)


def _dft_kernel(mf_ref, mi_ref, tfc, tfs, tic, tis, *, seq):
    rows, q = tic.shape
    mask = (1 << _log2(4 * seq)) - 1
    unit = math.pi / (2 * seq)
    i_row = lax.broadcasted_iota(I32, (rows, q), 0)
    col = lax.broadcasted_iota(I32, (rows, q), 1)

    @pl.when(pl.program_id(0) == 0)
    def _():
        for r in range(RADIX):
            af = ((2 * i_row * (RADIX * col + r)) & mask).astype(F32) * unit
            tfc[r] = jnp.cos(af)
            tfs[r] = jnp.sin(af)
        ai = (((2 * col + 1) * (RADIX * i_row)) & mask).astype(F32) * unit
        tic[...] = jnp.cos(ai)
        tis[...] = jnp.sin(ai)

    r0 = pl.program_id(0) * rows
    c1 = lax.broadcasted_iota(I32, (1, q), 1)
    for r in range(RADIX):
        bf = (((2 * r0 + 1) * (RADIX * c1 + r)) & mask).astype(F32) * unit
        bi = (((2 * c1 + 1) * (RADIX * r0 + r)) & mask).astype(F32) * unit
        cbf, sbf = jnp.cos(bf), jnp.sin(bf)
        cbi, sbi = jnp.cos(bi), jnp.sin(bi)
        mf_ref[r, 0] = (cbf * tfc[r] - sbf * tfs[r]).astype(BF16)
        mf_ref[r, 1] = (sbf * tfc[r] + cbf * tfs[r]).astype(BF16)
        mi_ref[r, :, :q] = (cbi * tic[...] - sbi * tis[...]).astype(BF16)
        mi_ref[r, :, q:] = (sbi * tic[...] + cbi * tis[...]).astype(BF16)


def _dft_matrices(seq):
    q = seq // RADIX
    rows = min(DFT_TILE, q)
    return pl.pallas_call(
        functools.partial(_dft_kernel, seq=seq),
        grid=(q // rows,),
        out_specs=[pl.BlockSpec((RADIX, 2, rows, q), lambda i: (0, 0, i, 0)),
                   pl.BlockSpec((RADIX, rows, 2 * q), lambda i: (0, i, 0))],
        out_shape=[jax.ShapeDtypeStruct((RADIX, 2, q, q), BF16), jax.ShapeDtypeStruct((RADIX, q, 2 * q), BF16)],
        scratch_shapes=[pltpu.VMEM((RADIX, rows, q), F32), pltpu.VMEM((RADIX, rows, q), F32),
                        pltpu.VMEM((rows, q), F32), pltpu.VMEM((rows, q), F32)],
        compiler_params=_cparams(("arbitrary",)),
        name="dft_matrices",
    )()


def _class_transform(mf_ref, x_of_class):
    tc, ts = [], []
    for r in range(RADIX):
        x = x_of_class(r)
        tc.append(_dot(mf_ref[r, 0], x))
        ts.append(_dot(mf_ref[r, 1], x))
    return [(tc[0] + tc[1] + tc[2] + tc[3], ts[0] + ts[1] + ts[2] + ts[3]),
            (tc[0] - tc[1] + tc[2] - tc[3], ts[1] - ts[0] + ts[3] - ts[2]),
            (tc[0] - ts[1] - tc[2] + ts[3], ts[0] + tc[1] - ts[2] - tc[3]),
            (tc[0] + ts[1] - tc[2] - ts[3], tc[1] - ts[0] + ts[2] - tc[3])]


def _spectrum_kernel(mf_ref, h_ref, g_ref, *, seq, hw):
    q = seq // RADIX
    groups = _class_transform(mf_ref, lambda r: h_ref[r * q:(r + 1) * q, :])
    scale = 1.0 / seq
    for x, (hc, hs) in enumerate(groups):
        g_ref[0, x, 0] = (hc[:, :hw] + hc[:, hw:]) * scale
        g_ref[0, x, 1] = (hs[:, :hw] - hs[:, hw:]) * scale


def _filter_spectra(mf, hfilt, hw):
    q = mf.shape[2]
    seq = q * RADIX
    rows = min(DFT_TILE, q)
    return pl.pallas_call(
        functools.partial(_spectrum_kernel, seq=seq, hw=hw),
        grid=(HYENA_ORDER, q // rows),
        in_specs=[pl.BlockSpec((RADIX, 2, rows, q), lambda n, i: (0, 0, i, 0)),
                  pl.BlockSpec((seq, 2 * hw), lambda n, i: (0, n))],
        out_specs=pl.BlockSpec((1, RADIX, 2, rows, hw), lambda n, i: (n, 0, 0, i, 0)),
        out_shape=jax.ShapeDtypeStruct((HYENA_ORDER, RADIX, 2, q, hw), F32),
        compiler_params=_cparams(("parallel", "parallel")),
        name="filter_spectra",
    )(mf, hfilt)


def _short_conv_kernel(u_ref, w_ref, b_ref, o_ref, y_ref):
    u = u_ref[0]
    s = u.shape[0]
    q = s // RADIX
    t = lax.broadcasted_iota(I32, u.shape, 0)
    prev = jnp.where(t == 0, 0.0, pltpu.roll(u, 1, axis=0))
    nxt = jnp.where(t == s - 1, 0.0, pltpu.roll(u, s - 1, axis=0))
    y_ref[...] = b_ref[...] + prev * w_ref[0:1, :] + u * w_ref[1:2, :] + nxt * w_ref[2:3, :]
    for r in range(RADIX):
        o_ref[0, r * q:(r + 1) * q, :] = y_ref[pl.ds(r, q, stride=RADIX), :]


def _short_conv(u, w, bias):
    b, s, c = u.shape
    tc = LANES
    return pl.pallas_call(
        _short_conv_kernel,
        grid=(b, c // tc),
        in_specs=[pl.BlockSpec((1, s, tc), lambda bi, j: (bi, 0, j)),
                  pl.BlockSpec((SHORT_CONV, tc), lambda bi, j: (0, j)),
                  pl.BlockSpec((1, tc), lambda bi, j: (0, j))],
        out_specs=pl.BlockSpec((1, s, tc), lambda bi, j: (bi, 0, j)),
        out_shape=jax.ShapeDtypeStruct((b, s, c), F32),
        scratch_shapes=[pltpu.VMEM((s, tc), F32)],
        compiler_params=_cparams(("parallel", "parallel")),
        name="short_conv",
    )(u, w, bias.reshape(1, c))


def _fwd_dft_kernel(mf_ref, z_ref, g_ref, y_ref, zb):
    q = zb.shape[0] // RADIX

    @pl.when(pl.program_id(1) == 0)
    def _():
        zb[...] = z_ref[0].astype(BF16)

    groups = _class_transform(mf_ref, lambda r: zb[r * q:(r + 1) * q, :])
    yc, ys = [], []
    for x, (uc, us) in enumerate(groups):
        gc, gs = g_ref[0, x, 0], g_ref[0, x, 1]
        yc.append(uc * gc - us * gs)
        ys.append(uc * gs + us * gc)
    a, b, c, d = range(RADIX)
    z = [(yc[a] + yc[c] + yc[b] + yc[d], ys[a] + ys[c] - ys[b] - ys[d]),
         (yc[a] + ys[c] - yc[b] + ys[d], ys[a] - yc[c] + ys[b] + yc[d]),
         (yc[a] - yc[c] + yc[b] - yc[d], ys[a] - ys[c] - ys[b] + ys[d]),
         (yc[a] - ys[c] - yc[b] - ys[d], ys[a] + yc[c] + ys[b] - yc[d])]
    for r, (zc, zs) in enumerate(z):
        y_ref[0, r, 0] = zc.astype(BF16)
        y_ref[0, r, 1] = zs.astype(BF16)


def _fwd_dft(mf, z, z_col, g, order, hw):
    b, seq = z.shape[0], z.shape[1]
    q = seq // RADIX
    rows = min(DFT_TILE, q)
    return pl.pallas_call(
        _fwd_dft_kernel,
        grid=(b, q // rows),
        in_specs=[pl.BlockSpec((RADIX, 2, rows, q), lambda bi, i: (0, 0, i, 0)),
                  pl.BlockSpec((1, seq, hw), lambda bi, i: (bi, 0, z_col)),
                  pl.BlockSpec((1, RADIX, 2, rows, hw), lambda bi, i: (order, 0, 0, i, 0))],
        out_specs=pl.BlockSpec((1, RADIX, 2, rows, hw), lambda bi, i: (bi, 0, 0, i, 0)),
        out_shape=jax.ShapeDtypeStruct((b, RADIX, 2, q, hw), BF16),
        scratch_shapes=[pltpu.VMEM((seq, hw), BF16)],
        compiler_params=_cparams(("parallel", "arbitrary")),
        name="hyena_fwd_dft",
    )(mf, z, g)


def _inv_dft_kernel(mi_ref, y_ref, z_ref, gate_ref, skip_ref, *rest, final):
    if final:
        og_ref, o_ref = rest
    else:
        (o_ref,) = rest
    rows = mi_ref.shape[1]
    for r in range(RADIX):
        conv = _dot(mi_ref[r], y_ref[0, r])
        z = gate_ref[0, r] * (conv + z_ref[0, r] * skip_ref[0])
        if final:
            ms = jnp.mean(z * z, axis=-1, keepdims=True)
            zn = z * lax.rsqrt(ms + EPS) * og_ref[...]
            for c in range(zn.shape[1] // LANES):
                o_ref[0, c, pl.ds(r, rows, stride=RADIX), :] = zn[:, c * LANES:(c + 1) * LANES]
        else:
            o_ref[0, r] = z


def _inv_dft(mi, y, z, z_col, gates, gate_col, skip, order, out_g, hw):
    b, seq = z.shape[0], z.shape[1]
    q = seq // RADIX
    rows = min(DFT_TILE, q)
    final = out_g is not None
    by_class = lambda a: a.reshape(b, RADIX, q, a.shape[2])
    in_specs = [pl.BlockSpec((RADIX, rows, 2 * q), lambda bi, i: (0, i, 0)),
                pl.BlockSpec((1, RADIX, 2 * q, hw), lambda bi, i: (bi, 0, 0, 0)),
                pl.BlockSpec((1, RADIX, rows, hw), lambda bi, i: (bi, 0, i, z_col)),
                pl.BlockSpec((1, RADIX, rows, hw), lambda bi, i: (bi, 0, i, gate_col)),
                pl.BlockSpec((1, 1, hw), lambda bi, i: (order, 0, 0))]
    args = [mi, y.reshape(b, RADIX, 2 * q, hw), by_class(z), by_class(gates), skip.reshape(HYENA_ORDER, 1, hw)]
    if final:
        in_specs.append(pl.BlockSpec((1, hw), lambda bi, i: (0, 0)))
        args.append(out_g.reshape(1, hw))
        out_spec = pl.BlockSpec((1, hw // LANES, RADIX * rows, LANES), lambda bi, i: (bi, 0, i, 0))
        out_shape = jax.ShapeDtypeStruct((b, hw // LANES, seq, LANES), F32)
    else:
        out_spec = pl.BlockSpec((1, RADIX, rows, hw), lambda bi, i: (bi, 0, i, 0))
        out_shape = jax.ShapeDtypeStruct((b, RADIX, q, hw), F32)
    out = pl.pallas_call(
        functools.partial(_inv_dft_kernel, final=final),
        grid=(b, q // rows),
        in_specs=in_specs,
        out_specs=out_spec,
        out_shape=out_shape,
        compiler_params=_cparams(("parallel", "parallel")),
        name="hyena_inv_dft",
    )(*args)
    return out if final else out.reshape(b, seq, hw)


def _out_kernel(a_ref, hy_ref, x_ref, wo_ref, g1_ref, sh_ref, sc_ref, n2_ref, rw_ref, rb_ref,
                xn_ref, h2_ref, lg_ref):
    aw = a_ref.shape[2]
    hy = jnp.concatenate([hy_ref[0, c] for c in range(hy_ref.shape[1])], axis=1).astype(BF16)
    mix = _dot(a_ref[0], wo_ref[:aw, :]) + _dot(hy, wo_ref[aw:, :])
    xn = x_ref[0] + g1_ref[0] * mix
    xn_ref[0] = xn
    ms = jnp.mean(xn * xn, axis=-1, keepdims=True)
    h2 = (xn * lax.rsqrt(ms + EPS) * n2_ref[...]) * (1.0 + sc_ref[0]) + sh_ref[0]
    _store_row_tiles(h2_ref, h2)
    hh, hl = _split_bf16(h2)
    wh, wl = _split_bf16(rw_ref[...])
    lg_ref[...] = _dot_nt(wh, hh) + (_dot_nt(wh, hl) + _dot_nt(wl, hh)) + rb_ref[...]


def _out_project(attn, hyn, x, wo_bf, g1, sh2, sc2, n2g, rw_t, rb):
    b, s, d = x.shape
    aw, hw = attn.shape[2], hyn.shape[1] * hyn.shape[3]
    ne = rw_t.shape[0]
    tm = min(ROW_TILE, s)
    nt = s // tm
    row = lambda bi, i: (bi, i, 0)
    per_b = lambda bi, i: (bi, 0, 0)
    const = lambda bi, i: (0, 0)
    return pl.pallas_call(
        _out_kernel,
        grid=(b, nt),
        in_specs=[pl.BlockSpec((1, tm, aw), row),
                  pl.BlockSpec((1, hw // LANES, tm, LANES), lambda bi, i: (bi, 0, i, 0)),
                  pl.BlockSpec((1, tm, d), row),
                  pl.BlockSpec((aw + hw, d), const),
                  pl.BlockSpec((1, 1, d), per_b), pl.BlockSpec((1, 1, d), per_b), pl.BlockSpec((1, 1, d), per_b),
                  pl.BlockSpec((1, d), const), pl.BlockSpec((ne, d), const), pl.BlockSpec((ne, 1), const)],
        out_specs=[pl.BlockSpec((1, tm, d), row),
                   pl.BlockSpec((tm * (d // LANES), LANES), lambda bi, i: (bi * nt + i, 0)),
                   pl.BlockSpec((ne, tm), lambda bi, i: (0, bi * nt + i))],
        out_shape=[jax.ShapeDtypeStruct((b, s, d), F32), jax.ShapeDtypeStruct((b * s * (d // LANES), LANES), F32),
                   jax.ShapeDtypeStruct((ne, b * s), F32)],
        compiler_params=_cparams(("parallel", "parallel")),
        name="out_project",
    )(attn, hyn, x, wo_bf, g1, sh2, sc2, n2g, rw_t, rb)


def _route_kernel(lg_ref, gate_ref, lpos_ref, tstart_ref, tcnt_ref, cnt_ref, carry):
    ne, tl = lg_ref.shape

    @pl.when(pl.program_id(0) == 0)
    def _():
        carry[...] = jnp.zeros_like(carry)

    l = lg_ref[...]
    rows = lax.broadcasted_iota(I32, (ne, tl), 0).astype(F32)
    vals, sels = [], []
    for k in range(TOP_K):
        m = jnp.max(l, axis=0, keepdims=True)
        ik = jnp.min(jnp.where(l == m, rows, float(ne)), axis=0, keepdims=True)
        sel = rows == ik
        vals.append(m)
        sels.append(sel)
        l = jnp.where(sel, -jnp.inf, l)
    exps = [jnp.exp(v - vals[0]) for v in vals]
    denom = exps[0] + exps[1] + exps[2] + exps[3]
    for k in range(TOP_K):
        gate_ref[k:k + 1, :] = exps[k] / denom
    oh = jnp.zeros((ne, tl), F32)
    for sel in sels:
        oh = oh + jnp.where(sel, 1.0, 0.0)
    r = lax.broadcasted_iota(I32, (tl, tl), 0)
    c = lax.broadcasted_iota(I32, (tl, tl), 1)
    tri = jnp.where(r <= c, 1.0, 0.0).astype(BF16)
    cum = _dot(oh.astype(BF16), tri)
    n_col = jnp.sum(oh, axis=1, keepdims=True)
    er = lax.broadcasted_iota(I32, (ne, LANES), 0)
    ec = lax.broadcasted_iota(I32, (ne, LANES), 1)
    to_lane = lambda col: jnp.sum(jnp.where(er == ec, jnp.broadcast_to(col, (ne, LANES)), 0.0),
                                  axis=0, keepdims=True)
    n_lane = to_lane(n_col)
    off_col = jnp.sum(jnp.where(ec < er, jnp.broadcast_to(n_lane, (ne, LANES)), 0.0), axis=1, keepdims=True)
    slab_pos = cum - oh + off_col
    for k in range(TOP_K):
        lpos_ref[k:k + 1, :] = jnp.sum(jnp.where(sels[k], slab_pos, 0.0), axis=0, keepdims=True).astype(I32)
    tstart_ref[0] = to_lane(carry[:, 0:1]).astype(I32)
    tcnt_ref[0] = n_lane.astype(I32)
    carry[...] = carry[...] + n_col
    cnt_ref[...] = carry[...]


def _route(logits_t, tl):
    ne, t = logits_t.shape
    nt = t // tl
    blk = lambda i: (0, i)
    per_tile = pl.BlockSpec((1, 1, LANES), lambda i: (i, 0, 0))
    return pl.pallas_call(
        _route_kernel,
        grid=(nt,),
        in_specs=[pl.BlockSpec((ne, tl), blk)],
        out_specs=[pl.BlockSpec((TOP_K, tl), blk), pl.BlockSpec((TOP_K, tl), blk), per_tile, per_tile,
                   pl.BlockSpec((ne, LANES), lambda i: (0, 0))],
        out_shape=[jax.ShapeDtypeStruct((TOP_K, t), F32), jax.ShapeDtypeStruct((TOP_K, t), I32),
                   jax.ShapeDtypeStruct((nt, 1, LANES), I32), jax.ShapeDtypeStruct((nt, 1, LANES), I32),
                   jax.ShapeDtypeStruct((ne, LANES), F32)],
        scratch_shapes=[pltpu.VMEM((ne, LANES), F32)],
        compiler_params=_cparams(("arbitrary",)),
        name="moe_route",
    )(logits_t)


def _slots_kernel(cnt_ref, tstart_ref, run_ref, blk_ref, meta_ref, *, rows_per_block):
    ne = cnt_ref.shape[0]
    shift = _log2(rows_per_block)
    cnt = cnt_ref[...].astype(I32)
    padded = ((cnt + (rows_per_block - 1)) >> shift) << shift
    r = lax.broadcasted_iota(I32, (ne, LANES), 0)
    c = lax.broadcasted_iota(I32, (ne, LANES), 1)
    padded_lane = jnp.sum(jnp.where(r == c, padded, 0), axis=0, keepdims=True)
    cnt_lane = jnp.sum(jnp.where(r == c, cnt, 0), axis=0, keepdims=True)
    pend_lane = jnp.sum(jnp.where(r <= c, padded, 0), axis=0, keepdims=True)
    pend_col = jnp.sum(jnp.where(c <= r, jnp.broadcast_to(padded_lane, (ne, LANES)), 0),
                       axis=1, keepdims=True)
    run_ref[...] = tstart_ref[...] + (pend_lane - padded_lane)
    nbp = blk_ref.shape[1]
    j0 = lax.broadcasted_iota(I32, (ne, nbp), 1) * rows_per_block
    be = jnp.sum(jnp.where(jnp.broadcast_to(pend_col, (ne, nbp)) <= j0, 1, 0), axis=0, keepdims=True)
    blk_ref[...] = jnp.minimum(be, ne - 1)
    total = jnp.max(pend_col, axis=0, keepdims=True)
    meta_ref[0:1, :] = pend_lane - padded_lane + cnt_lane
    meta_ref[1:2, :] = padded_lane - cnt_lane
    meta_ref[2:3, :] = jnp.broadcast_to(total >> shift, (1, LANES))
    meta_ref[3:8, :] = jnp.zeros((5, LANES), I32)


def _slots(cnt, tstart, n_blocks, rows_per_block):
    nbp = -(-n_blocks // LANES) * LANES
    return pl.pallas_call(
        functools.partial(_slots_kernel, rows_per_block=rows_per_block),
        out_shape=[jax.ShapeDtypeStruct(tstart.shape, I32), jax.ShapeDtypeStruct((1, nbp), I32),
                   jax.ShapeDtypeStruct((8, LANES), I32)],
        compiler_params=pltpu.CompilerParams(vmem_limit_bytes=V7X_VMEM_LIMIT),
        name="moe_slots",
    )(cnt, tstart)


def _pad_chunks(rows_per_block):
    sizes, s = [], rows_per_block // 2
    while s >= 1:
        sizes.append(s)
        s //= 2
    return sizes


def _rows(start, size, rt):
    return pl.ds(pl.multiple_of(start * rt, rt), size * rt)


def _for_each_run_chunk(run_ref, cnt_ref, tile, ne, max_rows, act):
    sizes = _pad_chunks(2 * max_rows)

    def each(e, off):
        left = cnt_ref[tile * ne + e]
        pos, start = off, run_ref[tile * ne + e]
        for size in sizes:
            hit = (left & size) != 0

            @pl.when(hit)
            def _():
                act(pos, start, size)

            inc = jnp.where(hit, size, 0)
            pos, start = pos + inc, start + inc
        return off + left

    lax.fori_loop(0, ne, each, 0)


def _dispatch_kernel(run_ref, cnt_ref, padlo_ref, npad_ref, nused_ref, lpos_ref, h_ref, buf_ref,
                     slab, zeros, sems, zsem, *, rows_per_block, rt, n_tiles):
    tl = lpos_ref.shape[1]
    ne = padlo_ref.shape[0]
    sizes = _pad_chunks(rows_per_block)
    half = rows_per_block // 2
    i = pl.program_id(0)
    slot = i & 1

    def pad_copy(start, size):
        return pltpu.make_async_copy(zeros.at[_rows(0, size, rt)], buf_ref.at[_rows(start, size, rt)], zsem)

    def run_copies(tile, sl, wait):
        def act(pos, start, size):
            cp = pltpu.make_async_copy(slab.at[sl, _rows(pos, size, rt)], buf_ref.at[_rows(start, size, rt)],
                                       sems.at[sl])
            if wait:
                cp.wait()
            else:
                cp.start()

        _for_each_run_chunk(run_ref, cnt_ref, tile, ne, tl, act)

    @pl.when(i == 0)
    def _():
        zeros[...] = jnp.zeros_like(zeros)

        first, last = 2 * nused_ref[0], buf_ref.shape[0] // (half * rt)
        lax.fori_loop(first, last, lambda j, c: (pad_copy(j * half, half).start(), c)[1], 0)
        lax.fori_loop(first, last, lambda j, c: (pad_copy(j * half, half).wait(), c)[1], 0)

        def each(e, wait):
            start = padlo_ref[e]
            left = npad_ref[e]
            for size in sizes:
                hit = (left & size) != 0

                @pl.when(hit)
                def _():
                    cp = pad_copy(start, size)
                    if wait:
                        cp.wait()
                    else:
                        cp.start()

                start = start + jnp.where(hit, size, 0)

        lax.fori_loop(0, ne, lambda e, c: (each(e, False), c)[1], 0)
        lax.fori_loop(0, ne, lambda e, c: (each(e, True), c)[1], 0)

    @pl.when(i >= 2)
    def _():
        run_copies(i - 2, slot, True)

    def fill(t, c):
        row = h_ref[_rows(t, 1, rt), :]
        for k in range(TOP_K):
            slab[slot, _rows(lpos_ref[k, t], 1, rt), :] = row
        return c

    lax.fori_loop(0, tl, fill, 0, unroll=8)
    run_copies(i, slot, False)

    @pl.when(i == n_tiles - 1)
    def _():
        if n_tiles >= 2:
            run_copies(i - 1, 1 - slot, True)
        run_copies(i, slot, True)


def _dispatch(h2r, lpos, run_start, run_cnt, pad_lo, n_pad, n_used, n_rows, rows_per_block, rt, tl):
    n_tiles = lpos.shape[1] // tl
    return pl.pallas_call(
        functools.partial(_dispatch_kernel, rows_per_block=rows_per_block, rt=rt, n_tiles=n_tiles),
        grid_spec=pltpu.PrefetchScalarGridSpec(
            num_scalar_prefetch=5,
            grid=(n_tiles,),
            in_specs=[pl.BlockSpec((TOP_K, tl), lambda i, *_: (0, i), memory_space=pltpu.SMEM),
                      pl.BlockSpec((tl * rt, LANES), lambda i, *_: (i, 0))],
            out_specs=pl.BlockSpec(memory_space=pl.ANY),
            scratch_shapes=[pltpu.VMEM((2, TOP_K * tl * rt, LANES), F32),
                            pltpu.VMEM((rows_per_block // 2 * rt, LANES), F32),
                            pltpu.SemaphoreType.DMA((2,)), pltpu.SemaphoreType.DMA(())]),
        out_shape=jax.ShapeDtypeStruct((n_rows * rt, LANES), F32),
        compiler_params=_cparams(("arbitrary",)),
        name="moe_dispatch",
    )(run_start, run_cnt, pad_lo, n_pad, n_used, lpos, h2r)


def _expert_kernel(be_ref, nu_ref, x_ref, w1_ref, b1_ref, w2_ref, b2_ref, o_ref, w1b, w2b):
    j = pl.program_id(0)
    active = j < nu_ref[0]

    @pl.when(active & ((j == 0) | (be_ref[j] != be_ref[jnp.maximum(j - 1, 0)])))
    def _():
        w1b[...] = w1_ref[0].astype(BF16)
        w2b[...] = w2_ref[0].astype(BF16)

    @pl.when(active)
    def _():
        de = w2b.shape[0]
        gl = _dot(_load_row_tiles(x_ref, w1b.shape[0] // LANES).astype(BF16), w1b[...]) + b1_ref[0]
        g = jnp.minimum(gl[:, :de], SWIGLU_LIMIT)
        lin = jnp.clip(gl[:, de:], -SWIGLU_LIMIT, SWIGLU_LIMIT)
        glu = g * jax.nn.sigmoid(SWIGLU_ALPHA * g)
        _store_row_tiles(o_ref, _dot(((lin + 1.0) * glu).astype(BF16), w2b[...]) + b2_ref[0])

    @pl.when(pl.program_id(0) >= nu_ref[0])
    def _():
        o_ref[...] = jnp.zeros_like(o_ref)


def _experts(buf, block_e, n_used, w1, b1, w2, b2, rows_per_block):
    ne, d, d2 = w1.shape
    de = w2.shape[1]
    blk_shape = (rows_per_block * (d // LANES), LANES)
    nb = buf.shape[0] // blk_shape[0]
    rowblk = lambda j, be, nu: (jnp.minimum(j, nu[0] - 1), 0)
    by_e = lambda j, be, nu: (be[j], 0, 0)
    return pl.pallas_call(
        _expert_kernel,
        grid_spec=pltpu.PrefetchScalarGridSpec(
            num_scalar_prefetch=2,
            grid=(nb,),
            in_specs=[pl.BlockSpec(blk_shape, rowblk),
                      pl.BlockSpec((1, d, d2), by_e), pl.BlockSpec((1, 1, d2), by_e),
                      pl.BlockSpec((1, de, d), by_e), pl.BlockSpec((1, 1, d), by_e)],
            out_specs=pl.BlockSpec(blk_shape, lambda j, be, nu: (j, 0)),
            scratch_shapes=[pltpu.VMEM((d, d2), BF16), pltpu.VMEM((de, d), BF16)]),
        out_shape=jax.ShapeDtypeStruct(buf.shape, F32),
        compiler_params=_cparams(("arbitrary",)),
        name="moe_experts",
    )(block_e, n_used, buf, w1, b1.reshape(ne, 1, d2), w2, b2.reshape(ne, 1, d))


def _combine_kernel(run_ref, cnt_ref, lpos_ref, gate_ref, xn_ref, g2_ref, ob_ref, o_ref, slab, acc, sems,
                    *, rt, ne, n_tiles):
    tl = lpos_ref.shape[1]
    i = pl.program_id(0) * pl.num_programs(1) + pl.program_id(1)
    slot = i & 1

    def run_copies(tile, sl, wait):
        def act(pos, start, size):
            cp = pltpu.make_async_copy(ob_ref.at[_rows(start, size, rt)], slab.at[sl, _rows(pos, size, rt)],
                                       sems.at[sl])
            if wait:
                cp.wait()
            else:
                cp.start()

        _for_each_run_chunk(run_ref, cnt_ref, tile, ne, tl, act)

    @pl.when(i == 0)
    def _():
        run_copies(i, slot, False)

    @pl.when(i + 1 < n_tiles)
    def _():
        run_copies(i + 1, 1 - slot, False)

    run_copies(i, slot, True)

    def token(t, c):
        a = gate_ref[0, t] * slab[slot, _rows(lpos_ref[0, t], 1, rt), :]
        for k in range(1, TOP_K):
            a = a + gate_ref[k, t] * slab[slot, _rows(lpos_ref[k, t], 1, rt), :]
        acc[_rows(t, 1, rt), :] = a
        return c

    lax.fori_loop(0, tl, token, 0, unroll=8)
    o_ref[0] = xn_ref[0] + g2_ref[0] * _load_row_tiles(acc, rt)


def _combine(out_buf, lpos, gates, run_start, run_cnt, xn, g2, rt, tl):
    b, s, d = xn.shape
    nt = s // tl
    tok = lambda bi, i, *_: (0, bi * nt + i)
    return pl.pallas_call(
        functools.partial(_combine_kernel, rt=rt, ne=N_EXPERTS, n_tiles=b * nt),
        grid_spec=pltpu.PrefetchScalarGridSpec(
            num_scalar_prefetch=2,
            grid=(b, nt),
            in_specs=[pl.BlockSpec((TOP_K, tl), tok, memory_space=pltpu.SMEM),
                      pl.BlockSpec((TOP_K, tl), tok, memory_space=pltpu.SMEM),
                      pl.BlockSpec((1, tl, d), lambda bi, i, *_: (bi, i, 0)),
                      pl.BlockSpec((1, 1, d), lambda bi, i, *_: (bi, 0, 0)),
                      pl.BlockSpec(memory_space=pl.ANY)],
            out_specs=pl.BlockSpec((1, tl, d), lambda bi, i, *_: (bi, i, 0)),
            scratch_shapes=[pltpu.VMEM((2, TOP_K * tl * rt, LANES), F32), pltpu.VMEM((tl * rt, LANES), F32),
                            pltpu.SemaphoreType.DMA((2,))]),
        out_shape=jax.ShapeDtypeStruct((b, s, d), F32),
        compiler_params=_cparams(("arbitrary", "arbitrary")),
        name="moe_combine",
    )(run_start, run_cnt, lpos, gates, xn, g2, out_buf)


def _moe(h2r, logits_t, xn, g2, w1, b1, w2, b2):
    b, s, d = xn.shape
    t = b * s
    rt = d // LANES
    tl = min(TOKEN_TILE, s)
    assert s % tl == 0
    n_blocks = (t * TOP_K) // EXPERT_ROWS + N_EXPERTS
    n_rows = n_blocks * EXPERT_ROWS
    gates, lpos, tstart, tcnt, cnt = _route(logits_t, tl)
    run, blk, meta = _slots(cnt, tstart, n_blocks, EXPERT_ROWS)
    run_start = run[:, 0, :N_EXPERTS].reshape(-1)
    run_cnt = tcnt[:, 0, :N_EXPERTS].reshape(-1)
    buf = _dispatch(h2r, lpos, run_start, run_cnt, meta[0, :N_EXPERTS], meta[1, :N_EXPERTS], meta[2, :1],
                    n_rows, EXPERT_ROWS, rt, tl)
    out_buf = _experts(buf, blk[0, :n_blocks], meta[2, :1], w1, b1, w2, b2, EXPERT_ROWS)
    return _combine(out_buf, lpos, gates, run_start, run_cnt, xn, g2, rt, tl)


def _layer(x, ctx, c, c_ctx, p, lam_init):
    b, s, d = x.shape
    attn_w = d // 2
    hw = d - attn_w
    v_dim = attn_w // N_HEADS
    qk_dim = v_dim // 2
    qk_cols = N_HEADS * 2 * qk_dim
    v_cols = N_HEADS * v_dim
    assert 2 * qk_dim == LANES and v_dim == LANES and s % GRID_W == 0

    rows = -(-(b + 1) // 8) * 8
    cc = jnp.zeros((rows, d), F32).at[:b].set(c).at[b].set(c_ctx)
    mod = _modulation(cc, p['w_mod'], p['b_mod'])
    mod_x = mod[:b].reshape(b, N_MOD, 1, d)
    sh1, sc1, g1, sh2, sc2, g2 = [mod_x[:, i] for i in range(N_MOD)]
    mod_c = mod[b:b + 1].reshape(1, N_MOD, 1, d)
    csh1, csc1 = mod_c[:, 0], mod_c[:, 1]

    w_in_bf = p['w_in'].astype(BF16)
    qg = jnp.tile(p['q_norm_g'], qk_cols // qk_dim).reshape(1, qk_cols)
    kg = jnp.tile(p['k_norm_g'], qk_cols // qk_dim).reshape(1, qk_cols)
    n1g = p['norm1_g'].reshape(1, d)
    cos_t, sin_t = _rope_tables(s, qk_dim)
    q, k, v, u_hy = _project_latent(x, sh1, sc1, n1g, w_in_bf, qg, kg, cos_t, sin_t, qk_cols, v_cols, qk_dim)
    k_c, v_c = _project_context(ctx, csh1, csc1, n1g, w_in_bf[:, qk_cols:2 * qk_cols + v_cols], kg,
                                qk_cols, v_cols, qk_dim)
    k_all = jnp.concatenate([k_c, k], axis=1)
    v_all = jnp.concatenate([v_c, v], axis=1)
    vec = lambda a: a.reshape(1, qk_dim)
    attn = _diff_attention(q, k_all, v_all, vec(p['lam_q1']), vec(p['lam_k1']), vec(p['lam_q2']),
                           vec(p['lam_k2']), p['subln_g'].reshape(1, v_dim), lam_init, qk_dim)

    hfilt = _hyena_filters(s, hw, p['hy_w1'], p['hy_b1'], p['hy_f1'], p['hy_w2'], p['hy_b2'], p['hy_f2'], p['hy_w3'])
    mf, mi = _dft_matrices(s)
    g_spec = _filter_spectra(mf, hfilt, hw)
    uc = _short_conv(u_hy, p['hy_conv_w'], p['hy_conv_b'])
    y1 = _fwd_dft(mf, uc, 0, g_spec, 0, hw)
    z1 = _inv_dft(mi, y1, uc, 0, uc, 1, p['hy_skip'], 0, None, hw)
    y2 = _fwd_dft(mf, z1, 0, g_spec, 1, hw)
    hyn = _inv_dft(mi, y2, z1, 0, uc, 2, p['hy_skip'], 1, p['hy_out_g'], hw)

    xn, h2, logits_t = _out_project(attn, hyn, x, p['w_out'].astype(BF16), g1, sh2, sc2,
                                    p['norm2_g'].reshape(1, d), p['router_w'].T,
                                    p['router_b'].reshape(N_EXPERTS, 1))
    return _moe(h2, logits_t, xn, g2, p['exp_w1'], p['exp_b1'], p['exp_w2'], p['exp_b2'])


def kernel(x, c, ctx, c_ctx, w_mod, b_mod, norm1_g, norm2_g, w_in, q_norm_g, k_norm_g, lam_q1, lam_k1, lam_q2, lam_k2, subln_g, hy_conv_w, hy_conv_b, hy_w1, hy_b1, hy_f1, hy_w2, hy_b2, hy_f2, hy_w3, hy_skip, hy_out_g, w_out, router_w, router_b, exp_w1, exp_b1, exp_w2, exp_b2):
    depth = w_mod.shape[0]
    assert depth == 1, "context-token update between layers is not implemented"
    p = {
        'w_mod': w_mod[0], 'b_mod': b_mod[0], 'norm1_g': norm1_g[0], 'norm2_g': norm2_g[0],
        'w_in': w_in[0], 'q_norm_g': q_norm_g[0], 'k_norm_g': k_norm_g[0],
        'lam_q1': lam_q1[0], 'lam_k1': lam_k1[0], 'lam_q2': lam_q2[0], 'lam_k2': lam_k2[0],
        'subln_g': subln_g[0], 'hy_conv_w': hy_conv_w[0], 'hy_conv_b': hy_conv_b[0],
        'hy_w1': hy_w1[0], 'hy_b1': hy_b1[0], 'hy_f1': hy_f1[0], 'hy_w2': hy_w2[0],
        'hy_b2': hy_b2[0], 'hy_f2': hy_f2[0], 'hy_w3': hy_w3[0], 'hy_skip': hy_skip[0],
        'hy_out_g': hy_out_g[0], 'w_out': w_out[0], 'router_w': router_w[0],
        'router_b': router_b[0], 'exp_w1': exp_w1[0], 'exp_b1': exp_b1[0],
        'exp_w2': exp_w2[0], 'exp_b2': exp_b2[0],
    }
    lam_init = 0.8 - 0.6 * math.exp(-0.3 * 0)
    return _layer(x, ctx, c, c_ctx, p, lam_init)
```

```python
import functools
import math

import jax
import jax.numpy as jnp
from jax import lax
from jax.experimental import pallas as pl
from jax.experimental.pallas import tpu as pltpu

F32 = jnp.float32
BF16 = jnp.bfloat16
I32 = jnp.int32

GRID_W = 64
N_HEADS = 4
N_MOD = 6
SHORT_CONV = 3
HYENA_ORDER = 2
N_BANDS = 8
FEAT_DIM = 1 + 2 * N_BANDS
FILTER_HIDDEN = 64
DECAY_TARGET = 1e-2
FAST_DECAY_PCT = 0.3
SLOW_DECAY_PCT = 1.5
N_EXPERTS = 32
TOP_K = 4
SWIGLU_LIMIT = 7.0
SWIGLU_ALPHA = 1.702
ROPE_BASE = 10000.0
EPS = 1e-6

LANES = 128
V7X_VMEM_LIMIT = 56 * 1024 * 1024

ROW_TILE = 512
ATT_Q_TILE = 256
ATT_KEY_CHUNKS = 17
DFT_TILE = 256
RADIX = 4
EXPERT_ROWS = 512
EXPERT_CHUNK = 256
TOKEN_TILE = 512


def _log2(n):
    assert n > 0 and n & (n - 1) == 0, f"{n} must be a power of two"
    return n.bit_length() - 1


def _cparams(sem, vmem=V7X_VMEM_LIMIT):
    return pltpu.CompilerParams(dimension_semantics=sem, vmem_limit_bytes=vmem)


def _split_bf16(a):
    hi = a.astype(BF16)
    lo = (a - hi.astype(F32)).astype(BF16)
    return hi, lo


def _dot(a, b):
    return jnp.dot(a, b, preferred_element_type=F32)


def _dot_nt(a, b):
    return lax.dot_general(a, b, (((1,), (1,)), ((), ())), preferred_element_type=F32)


def _store_row_tiles(ref, val):
    rows, d = val.shape
    rt = d // LANES
    for c in range(rt):
        ref[pl.ds(c, rows, stride=rt), :] = val[:, c * LANES:(c + 1) * LANES]


def _load_row_tiles(ref, rt):
    rows = ref.shape[0] // rt
    return jnp.concatenate([ref[pl.ds(c, rows, stride=rt), :] for c in range(rt)], axis=1)


def _dot3(a, b):
    ah, al = _split_bf16(a)
    bh, bl = _split_bf16(b)
    return _dot(ah, bh) + (_dot(ah, bl) + _dot(al, bh))


def _mod_kernel(c_ref, w_ref, b_ref, o_ref):
    c = c_ref[...]
    s = c * jax.nn.sigmoid(c)
    o_ref[...] = _dot3(s, w_ref[...]) + b_ref[...]


def _modulation(cc, w_mod, b_mod):
    rows, d = cc.shape
    n = w_mod.shape[1]
    tn = min(n, 1536)
    return pl.pallas_call(
        _mod_kernel,
        grid=(n // tn,),
        in_specs=[pl.BlockSpec((rows, d), lambda j: (0, 0)),
                  pl.BlockSpec((d, tn), lambda j: (0, j)),
                  pl.BlockSpec((1, tn), lambda j: (0, j))],
        out_specs=pl.BlockSpec((rows, tn), lambda j: (0, j)),
        out_shape=jax.ShapeDtypeStruct((rows, n), F32),
        compiler_params=_cparams(("parallel",)),
        name="modulation",
    )(cc, w_mod, b_mod.reshape(1, n))


def _rope_table_kernel(cos_ref, sin_ref, *, qk_dim):
    s, w = cos_ref.shape
    half = qk_dim // 2
    nf = half // 2
    t = lax.broadcasted_iota(I32, (s, w), 0)
    lane = lax.broadcasted_iota(I32, (s, w), 1)
    d = lane & (qk_dim - 1)
    j = d & (nf - 1)
    row = t >> _log2(GRID_W)
    col = t & (GRID_W - 1)
    pos = jnp.where(d < half, row, col).astype(F32)
    inv = jnp.exp(j.astype(F32) * (-math.log(ROPE_BASE) / nf))
    ang = pos * inv
    first = (d & (half - 1)) < nf
    cos_ref[...] = jnp.cos(ang)
    sn = jnp.sin(ang)
    sin_ref[...] = jnp.where(first, -sn, sn)


def _rope_tables(s, qk_dim):
    return pl.pallas_call(
        functools.partial(_rope_table_kernel, qk_dim=qk_dim),
        out_shape=(jax.ShapeDtypeStruct((s, LANES), F32), jax.ShapeDtypeStruct((s, LANES), F32)),
        name="rope_tables",
    )()


def _group_rms(t, gain, qk_dim):
    w = t.shape[1]
    r = lax.broadcasted_iota(I32, (w, w), 0) >> _log2(qk_dim)
    c = lax.broadcasted_iota(I32, (w, w), 1) >> _log2(qk_dim)
    bd = jnp.where(r == c, 1.0 / qk_dim, 0.0).astype(BF16)
    hi, lo = _split_bf16(t * t)
    ms = _dot(hi, bd) + _dot(lo, bd)
    return t * lax.rsqrt(ms + EPS) * gain


def _rope(t, cos, sin_signed, qk_dim):
    w = t.shape[1]
    nf = qk_dim // 4
    lane = lax.broadcasted_iota(I32, t.shape, 1)
    first = (lane & (2 * nf - 1)) < nf
    partner = jnp.where(first, pltpu.roll(t, w - nf, axis=1), pltpu.roll(t, nf, axis=1))
    return t * cos + partner * sin_signed


def _proj_kernel(*refs, latent, qk_cols, v_cols, qk_dim):
    if latent:
        (x_ref, sh_ref, sc_ref, g_ref, w_ref, qg_ref, kg_ref, cos_ref, sin_ref,
         q_out, k_out, v_out, u_out) = refs
    else:
        x_ref, sh_ref, sc_ref, g_ref, w_ref, kg_ref, k_out, v_out = refs
    x = x_ref[0]
    ms = jnp.mean(x * x, axis=-1, keepdims=True)
    h = (x * lax.rsqrt(ms + EPS) * g_ref[...]) * (1.0 + sc_ref[0]) + sh_ref[0]
    proj = _dot(h.astype(BF16), w_ref[...])
    if latent:
        reps = qk_cols // LANES
        cos = jnp.concatenate([cos_ref[...]] * reps, axis=1)
        sin = jnp.concatenate([sin_ref[...]] * reps, axis=1)
        q = _rope(_group_rms(proj[:, :qk_cols], qg_ref[...], qk_dim), cos, sin, qk_dim)
        q_out[0] = (q * (qk_dim ** -0.5 * math.log2(math.e))).astype(BF16)
        k = _rope(_group_rms(proj[:, qk_cols:2 * qk_cols], kg_ref[...], qk_dim), cos, sin, qk_dim)
        k_out[0] = k.astype(BF16)
        v_out[0] = proj[:, 2 * qk_cols:2 * qk_cols + v_cols].astype(BF16)
        u_out[0] = proj[:, 2 * qk_cols + v_cols:]
    else:
        k = _group_rms(proj[:, :qk_cols], kg_ref[...], qk_dim)
        k_out[0] = k.astype(BF16)
        v_out[0] = proj[:, qk_cols:qk_cols + v_cols].astype(BF16)


def _project_latent(x, sh, sc, g, w_bf, qg, kg, cos_t, sin_t, qk_cols, v_cols, qk_dim):
    b, s, d = x.shape
    n = w_bf.shape[1]
    hy_cols = n - 2 * qk_cols - v_cols
    tm = min(ROW_TILE, s)
    row = lambda bi, i: (bi, i, 0)
    per_b = lambda bi, i: (bi, 0, 0)
    const = lambda bi, i: (0, 0)
    return pl.pallas_call(
        functools.partial(_proj_kernel, latent=True, qk_cols=qk_cols, v_cols=v_cols, qk_dim=qk_dim),
        grid=(b, s // tm),
        in_specs=[pl.BlockSpec((1, tm, d), row),
                  pl.BlockSpec((1, 1, d), per_b), pl.BlockSpec((1, 1, d), per_b),
                  pl.BlockSpec((1, d), const), pl.BlockSpec((d, n), const),
                  pl.BlockSpec((1, qk_cols), const), pl.BlockSpec((1, qk_cols), const),
                  pl.BlockSpec((tm, LANES), lambda bi, i: (i, 0)),
                  pl.BlockSpec((tm, LANES), lambda bi, i: (i, 0))],
        out_specs=[pl.BlockSpec((1, tm, qk_cols), row), pl.BlockSpec((1, tm, qk_cols), row),
                   pl.BlockSpec((1, tm, v_cols), row), pl.BlockSpec((1, tm, hy_cols), row)],
        out_shape=[jax.ShapeDtypeStruct((b, s, qk_cols), BF16), jax.ShapeDtypeStruct((b, s, qk_cols), BF16),
                   jax.ShapeDtypeStruct((b, s, v_cols), BF16), jax.ShapeDtypeStruct((b, s, hy_cols), F32)],
        compiler_params=_cparams(("parallel", "parallel")),
        name="project_latent",
    )(x, sh, sc, g, w_bf, qg, kg, cos_t, sin_t)


def _project_context(ctx, sh, sc, g, w_bf, kg, qk_cols, v_cols, qk_dim):
    b, lc, d = ctx.shape
    n = w_bf.shape[1]
    tm = min(ROW_TILE, lc)
    row = lambda bi, i: (bi, i, 0)
    shared = lambda bi, i: (0, 0, 0)
    const = lambda bi, i: (0, 0)
    return pl.pallas_call(
        functools.partial(_proj_kernel, latent=False, qk_cols=qk_cols, v_cols=v_cols, qk_dim=qk_dim),
        grid=(b, lc // tm),
        in_specs=[pl.BlockSpec((1, tm, d), row),
                  pl.BlockSpec((1, 1, d), shared), pl.BlockSpec((1, 1, d), shared),
                  pl.BlockSpec((1, d), const), pl.BlockSpec((d, n), const),
                  pl.BlockSpec((1, qk_cols), const)],
        out_specs=[pl.BlockSpec((1, tm, qk_cols), row), pl.BlockSpec((1, tm, v_cols), row)],
        out_shape=[jax.ShapeDtypeStruct((b, lc, qk_cols), BF16), jax.ShapeDtypeStruct((b, lc, v_cols), BF16)],
        compiler_params=_cparams(("parallel", "parallel")),
        name="project_context",
    )(ctx, sh, sc, g, w_bf, kg)


def _key_chunks(kk, n):
    tiles = kk // LANES
    n = min(n, tiles)
    return [(LANES * (i * tiles // n), LANES * ((i + 1) * tiles // n)) for i in range(n)]


def _attn_kernel(q_ref, kc_ref, k_ref, vc_ref, v_ref, lq1, lk1, lq2, lk2, sg_ref, o_ref, kt_ref, v1_ref, *bufs,
                 lam_init, qk_dim):
    lam = (jnp.exp(jnp.sum(lq1[...] * lk1[...], axis=-1, keepdims=True))
           - jnp.exp(jnp.sum(lq2[...] * lk2[...], axis=-1, keepdims=True)) + lam_init)
    j = pl.program_id(2)
    even, odd = bufs[:4], bufs[4:]
    tq = q_ref.shape[1]
    chunks = _key_chunks(kt_ref.shape[1], ATT_KEY_CHUNKS)

    @pl.when(j == 0)
    def _():
        lc = kc_ref.shape[1]
        kt_ref[:, :lc] = kc_ref[0].T
        kt_ref[:, lc:] = k_ref[0].T
        v1_ref[:lc, :LANES] = vc_ref[0]
        v1_ref[lc:, :LANES] = v_ref[0]
        v1_ref[:, LANES:] = jnp.ones((v1_ref.shape[0], LANES), BF16)
        for ref in odd:
            ref[...] = jnp.zeros_like(ref)

    def step(cur, prev):
        s1_w, s2_w, m1_w, m2_w = cur
        s1_r, s2_r, m1_r, m2_r = prev
        q = q_ref[0]
        lane = lax.broadcasted_iota(I32, q.shape, 1)
        q1 = jnp.where(lane < qk_dim, q, jnp.zeros_like(q))
        q2 = jnp.where(lane >= qk_dim, q, jnp.zeros_like(q))
        m1p, m2p = m1_r[:, :1], m2_r[:, :1]
        m1 = m2 = jnp.full((tq, 1), -jnp.inf, F32)
        o1 = o2 = jnp.zeros((tq, 2 * LANES), F32)
        for c0, c1 in chunks:
            s1 = _dot(q1, kt_ref[:, c0:c1])
            s2 = _dot(q2, kt_ref[:, c0:c1])
            s1_w[:, c0:c1] = s1
            s2_w[:, c0:c1] = s2
            m1 = jnp.maximum(m1, jnp.max(s1, axis=-1, keepdims=True))
            m2 = jnp.maximum(m2, jnp.max(s2, axis=-1, keepdims=True))

            e1 = jnp.exp2(s1_r[:, c0:c1] - m1p)
            e2 = jnp.exp2(s2_r[:, c0:c1] - m2p)
            o1 = o1 + _dot(e1.astype(BF16), v1_ref[c0:c1, :])
            o2 = o2 + _dot(e2.astype(BF16), v1_ref[c0:c1, :])
        m1_w[...] = jnp.broadcast_to(m1, m1_w.shape)
        m2_w[...] = jnp.broadcast_to(m2, m2_w.shape)
        o = o1[:, :LANES] / o1[:, LANES:] - o2[:, :LANES] * (lam / o2[:, LANES:])
        ms = jnp.mean(o * o, axis=-1, keepdims=True)
        o_ref[0] = ((o * lax.rsqrt(ms + EPS) * sg_ref[...]) * (1.0 - lam_init)).astype(BF16)

    @pl.when((j & 1) == 0)
    def _():
        step(even, odd)

    @pl.when((j & 1) == 1)
    def _():
        step(odd, even)


def _diff_attention(q, k_c, k, v_c, v, lq1, lk1, lq2, lk2, subln_g, lam_init, qk_dim):
    b, s, w = q.shape
    lc = k_c.shape[1]
    assert lc % LANES == 0
    kk = lc + s
    tq = min(ATT_Q_TILE, s)
    nq = s // tq
    kv = lambda bi, h, i: (bi, 0, h)
    const = lambda bi, h, i: (0, 0)
    vec = pl.BlockSpec((1, qk_dim), const)
    wide = pltpu.VMEM((tq, kk), F32)
    stat = pltpu.VMEM((tq, LANES), F32)
    per_parity = [wide, wide, stat, stat]
    return pl.pallas_call(
        functools.partial(_attn_kernel, lam_init=lam_init, qk_dim=qk_dim),
        grid=(b, N_HEADS, nq + 1),
        in_specs=[pl.BlockSpec((1, tq, LANES), lambda bi, h, i: (bi, jnp.minimum(i, nq - 1), h)),
                  pl.BlockSpec((1, lc, LANES), kv), pl.BlockSpec((1, s, LANES), kv),
                  pl.BlockSpec((1, lc, LANES), kv), pl.BlockSpec((1, s, LANES), kv), vec, vec, vec, vec,
                  pl.BlockSpec((1, LANES), const)],
        out_specs=pl.BlockSpec((1, tq, LANES), lambda bi, h, i: (bi, jnp.maximum(i - 1, 0), h)),
        out_shape=jax.ShapeDtypeStruct((b, s, w), BF16),
        scratch_shapes=[pltpu.VMEM((LANES, kk), BF16), pltpu.VMEM((kk, 2 * LANES), BF16)] + per_parity + per_parity,
        compiler_params=_cparams(("parallel", "parallel", "arbitrary")),
        name="diff_attention",
    )(q, k_c, k, v_c, v, lq1, lk1, lq2, lk2, subln_g)


def _filter_kernel(w1_ref, b1_ref, f1_ref, w2_ref, b2_ref, f2_ref, w3_ref, o_ref, *, seq, hw):
    tl, n = o_ref.shape
    base = pl.program_id(0) * tl
    quarter = seq // RADIX

    def position(shape):
        p = lax.broadcasted_iota(I32, shape, 0) + base
        return (((p & (quarter - 1)) << _log2(RADIX)) | (p >> _log2(quarter))).astype(F32)

    pos = position((tl, LANES))
    lane = lax.broadcasted_iota(I32, (tl, LANES), 1)
    tn = pos / seq
    band_idx = jnp.where(lane <= N_BANDS, lane - 1, lane - 1 - N_BANDS).astype(F32)
    band = 1e-4 + band_idx * ((N_BANDS - 1 - 1e-4) / (N_BANDS - 1))
    ang = (2.0 * math.pi / seq) * pos * band
    feats = jnp.where(lane == 0, tn,
                      jnp.where(lane <= N_BANDS, jnp.sin(ang),
                                jnp.where(lane < FEAT_DIM, jnp.cos(ang), 0.0)))
    h = jnp.sin(f1_ref[...] * (_dot3(feats, w1_ref[...]) + b1_ref[...]))
    h = jnp.sin(f2_ref[...] * (_dot3(h, w2_ref[...]) + b2_ref[...]))
    h = _dot3(h, w3_ref[...])
    ch = (lax.broadcasted_iota(I32, (tl, n), 1) & ((1 << _log2(hw)) - 1)).astype(F32)
    lo = abs(math.log(DECAY_TARGET) / SLOW_DECAY_PCT)
    hi = abs(math.log(DECAY_TARGET) / FAST_DECAY_PCT)
    delta = lo + ch * ((hi - lo) / (hw - 1))
    o_ref[...] = (h * jnp.exp(-(position((tl, n)) / seq) * delta)).astype(BF16)


def _hyena_filters(seq, hw, w1, b1, f1, w2, b2, f2, w3):
    fh = w2.shape[0]
    n = w3.shape[1]
    w1p = jnp.zeros((LANES, fh), F32).at[:FEAT_DIM].set(w1)
    tl = min(ROW_TILE, seq)
    const = lambda i: (0, 0)
    return pl.pallas_call(
        functools.partial(_filter_kernel, seq=seq, hw=hw),
        grid=(seq // tl,),
        in_specs=[pl.BlockSpec((LANES, fh), const), pl.BlockSpec((1, fh), const), pl.BlockSpec((1, fh), const),
                  pl.BlockSpec((fh, fh), const), pl.BlockSpec((1, fh), const), pl.BlockSpec((1, fh), const),
                  pl.BlockSpec((fh, n), const)],
        out_specs=pl.BlockSpec((tl, n), lambda i: (i, 0)),
        out_shape=jax.ShapeDtypeStruct((seq, n), BF16),
        compiler_params=_cparams(("parallel",)),
        name="hyena_filters",
    )(w1p, b1.reshape(1, fh), f1.reshape(1, fh), w2, b2.reshape(1, fh), f2.reshape(1, fh), w3)


def _dft_kernel(mf_ref, mi_ref, tfc, tfs, tic, tis, *, seq):
    rows, q = tic.shape
    mask = (1 << _log2(4 * seq)) - 1
    unit = math.pi / (2 * seq)
    i_row = lax.broadcasted_iota(I32, (rows, q), 0)
    col = lax.broadcasted_iota(I32, (rows, q), 1)

    @pl.when(pl.program_id(0) == 0)
    def _():
        for r in range(RADIX):
            af = ((2 * i_row * (RADIX * col + r)) & mask).astype(F32) * unit
            tfc[r] = jnp.cos(af)
            tfs[r] = jnp.sin(af)
        ai = (((2 * col + 1) * (RADIX * i_row)) & mask).astype(F32) * unit
        tic[...] = jnp.cos(ai)
        tis[...] = jnp.sin(ai)

    r0 = pl.program_id(0) * rows
    c1 = lax.broadcasted_iota(I32, (1, q), 1)
    for r in range(RADIX):
        bf = (((2 * r0 + 1) * (RADIX * c1 + r)) & mask).astype(F32) * unit
        bi = (((2 * c1 + 1) * (RADIX * r0 + r)) & mask).astype(F32) * unit
        cbf, sbf = jnp.cos(bf), jnp.sin(bf)
        cbi, sbi = jnp.cos(bi), jnp.sin(bi)
        mf_ref[r, 0] = (cbf * tfc[r] - sbf * tfs[r]).astype(BF16)
        mf_ref[r, 1] = (sbf * tfc[r] + cbf * tfs[r]).astype(BF16)
        mi_ref[r, :, :q] = (cbi * tic[...] - sbi * tis[...]).astype(BF16)
        mi_ref[r, :, q:] = (sbi * tic[...] + cbi * tis[...]).astype(BF16)


def _dft_matrices(seq):
    q = seq // RADIX
    rows = min(DFT_TILE, q)
    return pl.pallas_call(
        functools.partial(_dft_kernel, seq=seq),
        grid=(q // rows,),
        out_specs=[pl.BlockSpec((RADIX, 2, rows, q), lambda i: (0, 0, i, 0)),
                   pl.BlockSpec((RADIX, rows, 2 * q), lambda i: (0, i, 0))],
        out_shape=[jax.ShapeDtypeStruct((RADIX, 2, q, q), BF16), jax.ShapeDtypeStruct((RADIX, q, 2 * q), BF16)],
        scratch_shapes=[pltpu.VMEM((RADIX, rows, q), F32), pltpu.VMEM((RADIX, rows, q), F32),
                        pltpu.VMEM((rows, q), F32), pltpu.VMEM((rows, q), F32)],
        compiler_params=_cparams(("arbitrary",)),
        name="dft_matrices",
    )()


def _class_transform(mf_ref, x_of_class):
    tc, ts = [], []
    for r in range(RADIX):
        x = x_of_class(r)
        tc.append(_dot(mf_ref[r, 0], x))
        ts.append(_dot(mf_ref[r, 1], x))
    return [(tc[0] + tc[1] + tc[2] + tc[3], ts[0] + ts[1] + ts[2] + ts[3]),
            (tc[0] - tc[1] + tc[2] - tc[3], ts[1] - ts[0] + ts[3] - ts[2]),
            (tc[0] - ts[1] - tc[2] + ts[3], ts[0] + tc[1] - ts[2] - tc[3]),
            (tc[0] + ts[1] - tc[2] - ts[3], tc[1] - ts[0] + ts[2] - tc[3])]


def _spectrum_kernel(mf_ref, h_ref, g_ref, *, seq, hw):
    q = seq // RADIX
    groups = _class_transform(mf_ref, lambda r: h_ref[r * q:(r + 1) * q, :])
    scale = 1.0 / seq
    for x, (hc, hs) in enumerate(groups):
        g_ref[0, x, 0] = (hc[:, :hw] + hc[:, hw:]) * scale
        g_ref[0, x, 1] = (hs[:, :hw] - hs[:, hw:]) * scale


def _filter_spectra(mf, hfilt, hw):
    q = mf.shape[2]
    seq = q * RADIX
    rows = min(DFT_TILE, q)
    return pl.pallas_call(
        functools.partial(_spectrum_kernel, seq=seq, hw=hw),
        grid=(HYENA_ORDER, q // rows),
        in_specs=[pl.BlockSpec((RADIX, 2, rows, q), lambda n, i: (0, 0, i, 0)),
                  pl.BlockSpec((seq, 2 * hw), lambda n, i: (0, n))],
        out_specs=pl.BlockSpec((1, RADIX, 2, rows, hw), lambda n, i: (n, 0, 0, i, 0)),
        out_shape=jax.ShapeDtypeStruct((HYENA_ORDER, RADIX, 2, q, hw), F32),
        compiler_params=_cparams(("parallel", "parallel")),
        name="filter_spectra",
    )(mf, hfilt)


def _short_conv_kernel(u_ref, w_ref, b_ref, o_ref, y_ref):
    u = u_ref[0]
    s = u.shape[0]
    q = s // RADIX
    t = lax.broadcasted_iota(I32, u.shape, 0)
    prev = jnp.where(t == 0, 0.0, pltpu.roll(u, 1, axis=0))
    nxt = jnp.where(t == s - 1, 0.0, pltpu.roll(u, s - 1, axis=0))
    y_ref[...] = b_ref[...] + prev * w_ref[0:1, :] + u * w_ref[1:2, :] + nxt * w_ref[2:3, :]
    for r in range(RADIX):
        o_ref[0, r * q:(r + 1) * q, :] = y_ref[pl.ds(r, q, stride=RADIX), :]


def _short_conv(u, w, bias):
    b, s, c = u.shape
    tc = LANES
    return pl.pallas_call(
        _short_conv_kernel,
        grid=(b, c // tc),
        in_specs=[pl.BlockSpec((1, s, tc), lambda bi, j: (bi, 0, j)),
                  pl.BlockSpec((SHORT_CONV, tc), lambda bi, j: (0, j)),
                  pl.BlockSpec((1, tc), lambda bi, j: (0, j))],
        out_specs=pl.BlockSpec((1, s, tc), lambda bi, j: (bi, 0, j)),
        out_shape=jax.ShapeDtypeStruct((b, s, c), F32),
        scratch_shapes=[pltpu.VMEM((s, tc), F32)],
        compiler_params=_cparams(("parallel", "parallel")),
        name="short_conv",
    )(u, w, bias.reshape(1, c))


def _fwd_dft_kernel(mf_ref, z_ref, g_ref, y_ref, zb):
    q = zb.shape[0] // RADIX

    @pl.when(pl.program_id(1) == 0)
    def _():
        zb[...] = z_ref[0].astype(BF16)

    groups = _class_transform(mf_ref, lambda r: zb[r * q:(r + 1) * q, :])
    yc, ys = [], []
    for x, (uc, us) in enumerate(groups):
        gc, gs = g_ref[0, x, 0], g_ref[0, x, 1]
        yc.append(uc * gc - us * gs)
        ys.append(uc * gs + us * gc)
    a, b, c, d = range(RADIX)
    z = [(yc[a] + yc[c] + yc[b] + yc[d], ys[a] + ys[c] - ys[b] - ys[d]),
         (yc[a] + ys[c] - yc[b] + ys[d], ys[a] - yc[c] + ys[b] + yc[d]),
         (yc[a] - yc[c] + yc[b] - yc[d], ys[a] - ys[c] - ys[b] + ys[d]),
         (yc[a] - ys[c] - yc[b] - ys[d], ys[a] + yc[c] + ys[b] - yc[d])]
    for r, (zc, zs) in enumerate(z):
        y_ref[0, r, 0] = zc.astype(BF16)
        y_ref[0, r, 1] = zs.astype(BF16)


def _fwd_dft(mf, z, z_col, g, order, hw):
    b, seq = z.shape[0], z.shape[1]
    q = seq // RADIX
    rows = min(DFT_TILE, q)
    return pl.pallas_call(
        _fwd_dft_kernel,
        grid=(b, q // rows),
        in_specs=[pl.BlockSpec((RADIX, 2, rows, q), lambda bi, i: (0, 0, i, 0)),
                  pl.BlockSpec((1, seq, hw), lambda bi, i: (bi, 0, z_col)),
                  pl.BlockSpec((1, RADIX, 2, rows, hw), lambda bi, i: (order, 0, 0, i, 0))],
        out_specs=pl.BlockSpec((1, RADIX, 2, rows, hw), lambda bi, i: (bi, 0, 0, i, 0)),
        out_shape=jax.ShapeDtypeStruct((b, RADIX, 2, q, hw), BF16),
        scratch_shapes=[pltpu.VMEM((seq, hw), BF16)],
        compiler_params=_cparams(("parallel", "arbitrary")),
        name="hyena_fwd_dft",
    )(mf, z, g)


def _inv_dft_kernel(mi_ref, y_ref, z_ref, gate_ref, skip_ref, *rest, final):
    if final:
        og_ref, o_ref = rest
    else:
        (o_ref,) = rest
    rows = mi_ref.shape[1]
    for r in range(RADIX):
        conv = _dot(mi_ref[r], y_ref[0, r])
        z = gate_ref[0, r] * (conv + z_ref[0, r] * skip_ref[0])
        if final:
            ms = jnp.mean(z * z, axis=-1, keepdims=True)
            zn = z * lax.rsqrt(ms + EPS) * og_ref[...]
            for c in range(zn.shape[1] // LANES):
                o_ref[0, c, pl.ds(r, rows, stride=RADIX), :] = zn[:, c * LANES:(c + 1) * LANES]
        else:
            o_ref[0, r] = z


def _inv_dft(mi, y, z, z_col, gates, gate_col, skip, order, out_g, hw):
    b, seq = z.shape[0], z.shape[1]
    q = seq // RADIX
    rows = min(DFT_TILE, q)
    final = out_g is not None
    by_class = lambda a: a.reshape(b, RADIX, q, a.shape[2])
    in_specs = [pl.BlockSpec((RADIX, rows, 2 * q), lambda bi, i: (0, i, 0)),
                pl.BlockSpec((1, RADIX, 2 * q, hw), lambda bi, i: (bi, 0, 0, 0)),
                pl.BlockSpec((1, RADIX, rows, hw), lambda bi, i: (bi, 0, i, z_col)),
                pl.BlockSpec((1, RADIX, rows, hw), lambda bi, i: (bi, 0, i, gate_col)),
                pl.BlockSpec((1, 1, hw), lambda bi, i: (order, 0, 0))]
    args = [mi, y.reshape(b, RADIX, 2 * q, hw), by_class(z), by_class(gates), skip.reshape(HYENA_ORDER, 1, hw)]
    if final:
        in_specs.append(pl.BlockSpec((1, hw), lambda bi, i: (0, 0)))
        args.append(out_g.reshape(1, hw))
        out_spec = pl.BlockSpec((1, hw // LANES, RADIX * rows, LANES), lambda bi, i: (bi, 0, i, 0))
        out_shape = jax.ShapeDtypeStruct((b, hw // LANES, seq, LANES), F32)
    else:
        out_spec = pl.BlockSpec((1, RADIX, rows, hw), lambda bi, i: (bi, 0, i, 0))
        out_shape = jax.ShapeDtypeStruct((b, RADIX, q, hw), F32)
    out = pl.pallas_call(
        functools.partial(_inv_dft_kernel, final=final),
        grid=(b, q // rows),
        in_specs=in_specs,
        out_specs=out_spec,
        out_shape=out_shape,
        compiler_params=_cparams(("parallel", "parallel")),
        name="hyena_inv_dft",
    )(*args)
    return out if final else out.reshape(b, seq, hw)


def _out_kernel(a_ref, hy_ref, x_ref, wo_ref, g1_ref, sh_ref, sc_ref, n2_ref, rw_ref, rb_ref,
                xn_ref, h2_ref, lg_ref):
    aw = a_ref.shape[2]
    hy = jnp.concatenate([hy_ref[0, c] for c in range(hy_ref.shape[1])], axis=1).astype(BF16)
    mix = _dot(a_ref[0], wo_ref[:aw, :]) + _dot(hy, wo_ref[aw:, :])
    xn = x_ref[0] + g1_ref[0] * mix
    xn_ref[0] = xn
    ms = jnp.mean(xn * xn, axis=-1, keepdims=True)
    h2 = (xn * lax.rsqrt(ms + EPS) * n2_ref[...]) * (1.0 + sc_ref[0]) + sh_ref[0]
    _store_row_tiles(h2_ref, h2)
    hh, hl = _split_bf16(h2)
    wh, wl = _split_bf16(rw_ref[...])
    lg_ref[...] = _dot_nt(wh, hh) + (_dot_nt(wh, hl) + _dot_nt(wl, hh)) + rb_ref[...]


def _out_project(attn, hyn, x, wo_bf, g1, sh2, sc2, n2g, rw_t, rb):
    b, s, d = x.shape
    aw, hw = attn.shape[2], hyn.shape[1] * hyn.shape[3]
    ne = rw_t.shape[0]
    tm = min(ROW_TILE, s)
    nt = s // tm
    row = lambda bi, i: (bi, i, 0)
    per_b = lambda bi, i: (bi, 0, 0)
    const = lambda bi, i: (0, 0)
    return pl.pallas_call(
        _out_kernel,
        grid=(b, nt),
        in_specs=[pl.BlockSpec((1, tm, aw), row),
                  pl.BlockSpec((1, hw // LANES, tm, LANES), lambda bi, i: (bi, 0, i, 0)),
                  pl.BlockSpec((1, tm, d), row),
                  pl.BlockSpec((aw + hw, d), const),
                  pl.BlockSpec((1, 1, d), per_b), pl.BlockSpec((1, 1, d), per_b), pl.BlockSpec((1, 1, d), per_b),
                  pl.BlockSpec((1, d), const), pl.BlockSpec((ne, d), const), pl.BlockSpec((ne, 1), const)],
        out_specs=[pl.BlockSpec((1, tm, d), row),
                   pl.BlockSpec((tm * (d // LANES), LANES), lambda bi, i: (bi * nt + i, 0)),
                   pl.BlockSpec((ne, tm), lambda bi, i: (0, bi * nt + i))],
        out_shape=[jax.ShapeDtypeStruct((b, s, d), F32), jax.ShapeDtypeStruct((b * s * (d // LANES), LANES), F32),
                   jax.ShapeDtypeStruct((ne, b * s), F32)],
        compiler_params=_cparams(("parallel", "parallel")),
        name="out_project",
    )(attn, hyn, x, wo_bf, g1, sh2, sc2, n2g, rw_t, rb)


def _route_kernel(lg_ref, gate_ref, lpos_ref, tstart_ref, tcnt_ref, cnt_ref, carry):
    ne, tl = lg_ref.shape

    @pl.when(pl.program_id(0) == 0)
    def _():
        carry[...] = jnp.zeros_like(carry)

    l = lg_ref[...]
    rows = lax.broadcasted_iota(I32, (ne, tl), 0).astype(F32)
    vals, sels = [], []
    for k in range(TOP_K):
        m = jnp.max(l, axis=0, keepdims=True)
        ik = jnp.min(jnp.where(l == m, rows, float(ne)), axis=0, keepdims=True)
        sel = rows == ik
        vals.append(m)
        sels.append(sel)
        l = jnp.where(sel, -jnp.inf, l)
    exps = [jnp.exp(v - vals[0]) for v in vals]
    denom = exps[0] + exps[1] + exps[2] + exps[3]
    for k in range(TOP_K):
        gate_ref[k:k + 1, :] = exps[k] / denom
    oh = jnp.zeros((ne, tl), F32)
    for sel in sels:
        oh = oh + jnp.where(sel, 1.0, 0.0)
    r = lax.broadcasted_iota(I32, (tl, tl), 0)
    c = lax.broadcasted_iota(I32, (tl, tl), 1)
    tri = jnp.where(r <= c, 1.0, 0.0).astype(BF16)
    cum = _dot(oh.astype(BF16), tri)
    n_col = jnp.sum(oh, axis=1, keepdims=True)
    er = lax.broadcasted_iota(I32, (ne, LANES), 0)
    ec = lax.broadcasted_iota(I32, (ne, LANES), 1)
    to_lane = lambda col: jnp.sum(jnp.where(er == ec, jnp.broadcast_to(col, (ne, LANES)), 0.0),
                                  axis=0, keepdims=True)
    n_lane = to_lane(n_col)
    off_col = jnp.sum(jnp.where(ec < er, jnp.broadcast_to(n_lane, (ne, LANES)), 0.0), axis=1, keepdims=True)
    slab_pos = cum - oh + off_col
    for k in range(TOP_K):
        lpos_ref[k:k + 1, :] = jnp.sum(jnp.where(sels[k], slab_pos, 0.0), axis=0, keepdims=True).astype(I32)
    tstart_ref[0] = to_lane(carry[:, 0:1]).astype(I32)
    tcnt_ref[0] = n_lane.astype(I32)
    carry[...] = carry[...] + n_col
    cnt_ref[...] = carry[...]


def _route(logits_t, tl):
    ne, t = logits_t.shape
    nt = t // tl
    blk = lambda i: (0, i)
    per_tile = pl.BlockSpec((1, 1, LANES), lambda i: (i, 0, 0))
    return pl.pallas_call(
        _route_kernel,
        grid=(nt,),
        in_specs=[pl.BlockSpec((ne, tl), blk)],
        out_specs=[pl.BlockSpec((TOP_K, tl), blk), pl.BlockSpec((TOP_K, tl), blk), per_tile, per_tile,
                   pl.BlockSpec((ne, LANES), lambda i: (0, 0))],
        out_shape=[jax.ShapeDtypeStruct((TOP_K, t), F32), jax.ShapeDtypeStruct((TOP_K, t), I32),
                   jax.ShapeDtypeStruct((nt, 1, LANES), I32), jax.ShapeDtypeStruct((nt, 1, LANES), I32),
                   jax.ShapeDtypeStruct((ne, LANES), F32)],
        scratch_shapes=[pltpu.VMEM((ne, LANES), F32)],
        compiler_params=_cparams(("arbitrary",)),
        name="moe_route",
    )(logits_t)


def _slots_kernel(cnt_ref, tstart_ref, run_ref, blk_ref, meta_ref, *, rows_per_block):
    ne = cnt_ref.shape[0]
    shift = _log2(rows_per_block)
    cnt = cnt_ref[...].astype(I32)
    padded = ((cnt + (rows_per_block - 1)) >> shift) << shift
    r = lax.broadcasted_iota(I32, (ne, LANES), 0)
    c = lax.broadcasted_iota(I32, (ne, LANES), 1)
    padded_lane = jnp.sum(jnp.where(r == c, padded, 0), axis=0, keepdims=True)
    cnt_lane = jnp.sum(jnp.where(r == c, cnt, 0), axis=0, keepdims=True)
    pend_lane = jnp.sum(jnp.where(r <= c, padded, 0), axis=0, keepdims=True)
    pend_col = jnp.sum(jnp.where(c <= r, jnp.broadcast_to(padded_lane, (ne, LANES)), 0),
                       axis=1, keepdims=True)
    run_ref[...] = tstart_ref[...] + (pend_lane - padded_lane)
    nbp = blk_ref.shape[1]
    j0 = lax.broadcasted_iota(I32, (ne, nbp), 1) * rows_per_block
    be = jnp.sum(jnp.where(jnp.broadcast_to(pend_col, (ne, nbp)) <= j0, 1, 0), axis=0, keepdims=True)
    blk_ref[...] = jnp.minimum(be, ne - 1)
    total = jnp.max(pend_col, axis=0, keepdims=True)
    meta_ref[0:1, :] = pend_lane - padded_lane + cnt_lane
    meta_ref[1:2, :] = padded_lane - cnt_lane
    meta_ref[2:3, :] = jnp.broadcast_to(total >> shift, (1, LANES))
    meta_ref[3:8, :] = jnp.zeros((5, LANES), I32)


def _slots(cnt, tstart, n_blocks, rows_per_block):
    nbp = -(-n_blocks // LANES) * LANES
    return pl.pallas_call(
        functools.partial(_slots_kernel, rows_per_block=rows_per_block),
        out_shape=[jax.ShapeDtypeStruct(tstart.shape, I32), jax.ShapeDtypeStruct((1, nbp), I32),
                   jax.ShapeDtypeStruct((8, LANES), I32)],
        compiler_params=pltpu.CompilerParams(vmem_limit_bytes=V7X_VMEM_LIMIT),
        name="moe_slots",
    )(cnt, tstart)


def _pad_chunks(rows_per_block):
    sizes, s = [], rows_per_block // 2
    while s >= 1:
        sizes.append(s)
        s //= 2
    return sizes


def _rows(start, size, rt):
    return pl.ds(pl.multiple_of(start * rt, rt), size * rt)


def _for_each_run_chunk(run_ref, cnt_ref, tile, ne, max_rows, act):
    sizes = _pad_chunks(2 * max_rows)

    def each(e, off):
        left = cnt_ref[tile * ne + e]
        pos, start = off, run_ref[tile * ne + e]
        for size in sizes:
            hit = (left & size) != 0

            @pl.when(hit)
            def _():
                act(pos, start, size)

            inc = jnp.where(hit, size, 0)
            pos, start = pos + inc, start + inc
        return off + left

    lax.fori_loop(0, ne, each, 0)


def _dispatch_kernel(run_ref, cnt_ref, padlo_ref, npad_ref, nused_ref, lpos_ref, h_ref, buf_ref,
                     slab, zeros, sems, zsem, *, rows_per_block, rt, n_tiles):
    tl = lpos_ref.shape[1]
    ne = padlo_ref.shape[0]
    sizes = _pad_chunks(rows_per_block)
    half = rows_per_block // 2
    i = pl.program_id(0)
    slot = i & 1

    def pad_copy(start, size):
        return pltpu.make_async_copy(zeros.at[_rows(0, size, rt)], buf_ref.at[_rows(start, size, rt)], zsem)

    def run_copies(tile, sl, wait):
        def act(pos, start, size):
            cp = pltpu.make_async_copy(slab.at[sl, _rows(pos, size, rt)], buf_ref.at[_rows(start, size, rt)],
                                       sems.at[sl])
            if wait:
                cp.wait()
            else:
                cp.start()

        _for_each_run_chunk(run_ref, cnt_ref, tile, ne, tl, act)

    @pl.when(i == 0)
    def _():
        zeros[...] = jnp.zeros_like(zeros)

        first, last = 2 * nused_ref[0], buf_ref.shape[0] // (half * rt)
        lax.fori_loop(first, last, lambda j, c: (pad_copy(j * half, half).start(), c)[1], 0)
        lax.fori_loop(first, last, lambda j, c: (pad_copy(j * half, half).wait(), c)[1], 0)

        def each(e, wait):
            start = padlo_ref[e]
            left = npad_ref[e]
            for size in sizes:
                hit = (left & size) != 0

                @pl.when(hit)
                def _():
                    cp = pad_copy(start, size)
                    if wait:
                        cp.wait()
                    else:
                        cp.start()

                start = start + jnp.where(hit, size, 0)

        lax.fori_loop(0, ne, lambda e, c: (each(e, False), c)[1], 0)
        lax.fori_loop(0, ne, lambda e, c: (each(e, True), c)[1], 0)

    @pl.when(i >= 2)
    def _():
        run_copies(i - 2, slot, True)

    def fill(t, c):
        row = h_ref[_rows(t, 1, rt), :]
        for k in range(TOP_K):
            slab[slot, _rows(lpos_ref[k, t], 1, rt), :] = row
        return c

    lax.fori_loop(0, tl, fill, 0, unroll=8)
    run_copies(i, slot, False)

    @pl.when(i == n_tiles - 1)
    def _():
        if n_tiles >= 2:
            run_copies(i - 1, 1 - slot, True)
        run_copies(i, slot, True)


def _dispatch(h2r, lpos, run_start, run_cnt, pad_lo, n_pad, n_used, n_rows, rows_per_block, rt, tl):
    n_tiles = lpos.shape[1] // tl
    return pl.pallas_call(
        functools.partial(_dispatch_kernel, rows_per_block=rows_per_block, rt=rt, n_tiles=n_tiles),
        grid_spec=pltpu.PrefetchScalarGridSpec(
            num_scalar_prefetch=5,
            grid=(n_tiles,),
            in_specs=[pl.BlockSpec((TOP_K, tl), lambda i, *_: (0, i), memory_space=pltpu.SMEM),
                      pl.BlockSpec((tl * rt, LANES), lambda i, *_: (i, 0))],
            out_specs=pl.BlockSpec(memory_space=pl.ANY),
            scratch_shapes=[pltpu.VMEM((2, TOP_K * tl * rt, LANES), F32),
                            pltpu.VMEM((rows_per_block // 2 * rt, LANES), F32),
                            pltpu.SemaphoreType.DMA((2,)), pltpu.SemaphoreType.DMA(())]),
        out_shape=jax.ShapeDtypeStruct((n_rows * rt, LANES), F32),
        compiler_params=_cparams(("arbitrary",)),
        name="moe_dispatch",
    )(run_start, run_cnt, pad_lo, n_pad, n_used, lpos, h2r)


def _expert_kernel(be_ref, nu_ref, x_ref, w1_ref, b1_ref, w2_ref, b2_ref, o_ref, w1b, w2b):
    j = pl.program_id(0)
    active = j < nu_ref[0]

    @pl.when(active & ((j == 0) | (be_ref[j] != be_ref[jnp.maximum(j - 1, 0)])))
    def _():
        w1b[...] = w1_ref[0].astype(BF16)
        w2b[...] = w2_ref[0].astype(BF16)

    @pl.when(active)
    def _():
        de = w2b.shape[0]
        x = _load_row_tiles(x_ref, w1b.shape[0] // LANES).astype(BF16)
        out = None
        for c0 in range(0, de, EXPERT_CHUNK):
            c1 = c0 + EXPERT_CHUNK
            g = _dot(x, w1b[:, c0:c1]) + b1_ref[0, :, c0:c1]
            lin = _dot(x, w1b[:, de + c0:de + c1]) + b1_ref[0, :, de + c0:de + c1]
            g = jnp.minimum(g, SWIGLU_LIMIT)
            lin = jnp.clip(lin, -SWIGLU_LIMIT, SWIGLU_LIMIT)
            h = ((lin + 1.0) * (g * jax.nn.sigmoid(SWIGLU_ALPHA * g))).astype(BF16)
            part = _dot(h, w2b[c0:c1, :])
            out = part if out is None else out + part
        _store_row_tiles(o_ref, out + b2_ref[0])

    @pl.when(pl.program_id(0) >= nu_ref[0])
    def _():
        o_ref[...] = jnp.zeros_like(o_ref)


def _experts(buf, block_e, n_used, w1, b1, w2, b2, rows_per_block):
    ne, d, d2 = w1.shape
    de = w2.shape[1]
    blk_shape = (rows_per_block * (d // LANES), LANES)
    nb = buf.shape[0] // blk_shape[0]
    rowblk = lambda j, be, nu: (jnp.minimum(j, nu[0] - 1), 0)
    by_e = lambda j, be, nu: (be[j], 0, 0)
    return pl.pallas_call(
        _expert_kernel,
        grid_spec=pltpu.PrefetchScalarGridSpec(
            num_scalar_prefetch=2,
            grid=(nb,),
            in_specs=[pl.BlockSpec(blk_shape, rowblk),
                      pl.BlockSpec((1, d, d2), by_e), pl.BlockSpec((1, 1, d2), by_e),
                      pl.BlockSpec((1, de, d), by_e), pl.BlockSpec((1, 1, d), by_e)],
            out_specs=pl.BlockSpec(blk_shape, lambda j, be, nu: (j, 0)),
            scratch_shapes=[pltpu.VMEM((d, d2), BF16), pltpu.VMEM((de, d), BF16)]),
        out_shape=jax.ShapeDtypeStruct(buf.shape, F32),
        compiler_params=_cparams(("arbitrary",)),
        name="moe_experts",
    )(block_e, n_used, buf, w1, b1.reshape(ne, 1, d2), w2, b2.reshape(ne, 1, d))


def _combine_kernel(run_ref, cnt_ref, lpos_ref, gate_ref, xn_ref, g2_ref, ob_ref, o_ref, slab, acc, sems,
                    *, rt, ne, n_tiles):
    tl = lpos_ref.shape[1]
    i = pl.program_id(0) * pl.num_programs(1) + pl.program_id(1)
    slot = i & 1

    def run_copies(tile, sl, wait):
        def act(pos, start, size):
            cp = pltpu.make_async_copy(ob_ref.at[_rows(start, size, rt)], slab.at[sl, _rows(pos, size, rt)],
                                       sems.at[sl])
            if wait:
                cp.wait()
            else:
                cp.start()

        _for_each_run_chunk(run_ref, cnt_ref, tile, ne, tl, act)

    @pl.when(i == 0)
    def _():
        run_copies(i, slot, False)

    @pl.when(i + 1 < n_tiles)
    def _():
        run_copies(i + 1, 1 - slot, False)

    run_copies(i, slot, True)

    def token(t, c):
        a = gate_ref[0, t] * slab[slot, _rows(lpos_ref[0, t], 1, rt), :]
        for k in range(1, TOP_K):
            a = a + gate_ref[k, t] * slab[slot, _rows(lpos_ref[k, t], 1, rt), :]
        acc[_rows(t, 1, rt), :] = a
        return c

    lax.fori_loop(0, tl, token, 0, unroll=8)
    o_ref[0] = xn_ref[0] + g2_ref[0] * _load_row_tiles(acc, rt)


def _combine(out_buf, lpos, gates, run_start, run_cnt, xn, g2, rt, tl):
    b, s, d = xn.shape
    nt = s // tl
    tok = lambda bi, i, *_: (0, bi * nt + i)
    return pl.pallas_call(
        functools.partial(_combine_kernel, rt=rt, ne=N_EXPERTS, n_tiles=b * nt),
        grid_spec=pltpu.PrefetchScalarGridSpec(
            num_scalar_prefetch=2,
            grid=(b, nt),
            in_specs=[pl.BlockSpec((TOP_K, tl), tok, memory_space=pltpu.SMEM),
                      pl.BlockSpec((TOP_K, tl), tok, memory_space=pltpu.SMEM),
                      pl.BlockSpec((1, tl, d), lambda bi, i, *_: (bi, i, 0)),
                      pl.BlockSpec((1, 1, d), lambda bi, i, *_: (bi, 0, 0)),
                      pl.BlockSpec(memory_space=pl.ANY)],
            out_specs=pl.BlockSpec((1, tl, d), lambda bi, i, *_: (bi, i, 0)),
            scratch_shapes=[pltpu.VMEM((2, TOP_K * tl * rt, LANES), F32), pltpu.VMEM((tl * rt, LANES), F32),
                            pltpu.SemaphoreType.DMA((2,))]),
        out_shape=jax.ShapeDtypeStruct((b, s, d), F32),
        compiler_params=_cparams(("arbitrary", "arbitrary")),
        name="moe_combine",
    )(run_start, run_cnt, lpos, gates, xn, g2, out_buf)


def _moe(h2r, logits_t, xn, g2, w1, b1, w2, b2):
    b, s, d = xn.shape
    t = b * s
    rt = d // LANES
    tl = min(TOKEN_TILE, s)
    assert s % tl == 0
    n_blocks = (t * TOP_K) // EXPERT_ROWS + N_EXPERTS
    n_rows = n_blocks * EXPERT_ROWS
    gates, lpos, tstart, tcnt, cnt = _route(logits_t, tl)
    run, blk, meta = _slots(cnt, tstart, n_blocks, EXPERT_ROWS)
    run_start = run[:, 0, :N_EXPERTS].reshape(-1)
    run_cnt = tcnt[:, 0, :N_EXPERTS].reshape(-1)
    buf = _dispatch(h2r, lpos, run_start, run_cnt, meta[0, :N_EXPERTS], meta[1, :N_EXPERTS], meta[2, :1],
                    n_rows, EXPERT_ROWS, rt, tl)
    out_buf = _experts(buf, blk[0, :n_blocks], meta[2, :1], w1, b1, w2, b2, EXPERT_ROWS)
    return _combine(out_buf, lpos, gates, run_start, run_cnt, xn, g2, rt, tl)


def _layer(x, ctx, c, c_ctx, p, lam_init):
    b, s, d = x.shape
    attn_w = d // 2
    hw = d - attn_w
    v_dim = attn_w // N_HEADS
    qk_dim = v_dim // 2
    qk_cols = N_HEADS * 2 * qk_dim
    v_cols = N_HEADS * v_dim
    assert 2 * qk_dim == LANES and v_dim == LANES and s % GRID_W == 0

    rows = -(-(b + 1) // 8) * 8
    cc = jnp.zeros((rows, d), F32).at[:b].set(c).at[b].set(c_ctx)
    mod = _modulation(cc, p['w_mod'], p['b_mod'])
    mod_x = mod[:b].reshape(b, N_MOD, 1, d)
    sh1, sc1, g1, sh2, sc2, g2 = [mod_x[:, i] for i in range(N_MOD)]
    mod_c = mod[b:b + 1].reshape(1, N_MOD, 1, d)
    csh1, csc1 = mod_c[:, 0], mod_c[:, 1]

    w_in_bf = p['w_in'].astype(BF16)
    qg = jnp.tile(p['q_norm_g'], qk_cols // qk_dim).reshape(1, qk_cols)
    kg = jnp.tile(p['k_norm_g'], qk_cols // qk_dim).reshape(1, qk_cols)
    n1g = p['norm1_g'].reshape(1, d)
    cos_t, sin_t = _rope_tables(s, qk_dim)
    q, k, v, u_hy = _project_latent(x, sh1, sc1, n1g, w_in_bf, qg, kg, cos_t, sin_t, qk_cols, v_cols, qk_dim)
    k_c, v_c = _project_context(ctx, csh1, csc1, n1g, w_in_bf[:, qk_cols:2 * qk_cols + v_cols], kg,
                                qk_cols, v_cols, qk_dim)
    vec = lambda a: a.reshape(1, qk_dim)
    attn = _diff_attention(q, k_c, k, v_c, v, vec(p['lam_q1']), vec(p['lam_k1']), vec(p['lam_q2']),
                           vec(p['lam_k2']), p['subln_g'].reshape(1, v_dim), lam_init, qk_dim)

    hfilt = _hyena_filters(s, hw, p['hy_w1'], p['hy_b1'], p['hy_f1'], p['hy_w2'], p['hy_b2'], p['hy_f2'], p['hy_w3'])
    mf, mi = _dft_matrices(s)
    g_spec = _filter_spectra(mf, hfilt, hw)
    uc = _short_conv(u_hy, p['hy_conv_w'], p['hy_conv_b'])
    y1 = _fwd_dft(mf, uc, 0, g_spec, 0, hw)
    z1 = _inv_dft(mi, y1, uc, 0, uc, 1, p['hy_skip'], 0, None, hw)
    y2 = _fwd_dft(mf, z1, 0, g_spec, 1, hw)
    hyn = _inv_dft(mi, y2, z1, 0, uc, 2, p['hy_skip'], 1, p['hy_out_g'], hw)

    xn, h2, logits_t = _out_project(attn, hyn, x, p['w_out'].astype(BF16), g1, sh2, sc2,
                                    p['norm2_g'].reshape(1, d), p['router_w'].T,
                                    p['router_b'].reshape(N_EXPERTS, 1))
    return _moe(h2, logits_t, xn, g2, p['exp_w1'], p['exp_b1'], p['exp_w2'], p['exp_b2'])


def kernel(x, c, ctx, c_ctx, w_mod, b_mod, norm1_g, norm2_g, w_in, q_norm_g, k_norm_g, lam_q1, lam_k1, lam_q2, lam_k2, subln_g, hy_conv_w, hy_conv_b, hy_w1, hy_b1, hy_f1, hy_w2, hy_b2, hy_f2, hy_w3, hy_skip, hy_out_g, w_out, router_w, router_b, exp_w1, exp_b1, exp_w2, exp_b2):
    depth = w_mod.shape[0]
    assert depth == 1, "context-token update between layers is not implemented"
    p = {
        'w_mod': w_mod[0], 'b_mod': b_mod[0], 'norm1_g': norm1_g[0], 'norm2_g': norm2_g[0],
        'w_in': w_in[0], 'q_norm_g': q_norm_g[0], 'k_norm_g': k_norm_g[0],
        'lam_q1': lam_q1[0], 'lam_k1': lam_k1[0], 'lam_q2': lam_q2[0], 'lam_k2': lam_k2[0],
        'subln_g': subln_g[0], 'hy_conv_w': hy_conv_w[0], 'hy_conv_b': hy_conv_b[0],
        'hy_w1': hy_w1[0], 'hy_b1': hy_b1[0], 'hy_f1': hy_f1[0], 'hy_w2': hy_w2[0],
        'hy_b2': hy_b2[0], 'hy_f2': hy_f2[0], 'hy_w3': hy_w3[0], 'hy_skip': hy_skip[0],
        'hy_out_g': hy_out_g[0], 'w_out': w_out[0], 'router_w': router_w[0],
        'router_b': router_b[0], 'exp_w1': exp_w1[0], 'exp_b1': exp_b1[0],
        'exp_w2': exp_w2[0], 'exp_b2': exp_b2[0],
    }
    lam_init = 0.8 - 0.6 * math.exp(-0.3 * 0)
    return _layer(x, ctx, c, c_ctx, p, lam_init)
```

```python
import functools
import math

import jax
import jax.numpy as jnp
from jax import lax
from jax.experimental import pallas as pl
from jax.experimental.pallas import tpu as pltpu

F32 = jnp.float32
BF16 = jnp.bfloat16
I32 = jnp.int32

GRID_W = 64
N_HEADS = 4
N_MOD = 6
SHORT_CONV = 3
HYENA_ORDER = 2
N_BANDS = 8
FEAT_DIM = 1 + 2 * N_BANDS
FILTER_HIDDEN = 64
DECAY_TARGET = 1e-2
FAST_DECAY_PCT = 0.3
SLOW_DECAY_PCT = 1.5
N_EXPERTS = 32
TOP_K = 4
SWIGLU_LIMIT = 7.0
SWIGLU_ALPHA = 1.702
ROPE_BASE = 10000.0
EPS = 1e-6

LANES = 128
V7X_VMEM_LIMIT = 56 * 1024 * 1024

ROW_TILE = 512
ATT_Q_TILE = 256
ATT_KEY_CHUNKS = 17
DFT_TILE = 256
RADIX = 4
EXPERT_ROWS = 512
TOKEN_TILE = 512


def _log2(n):
    assert n > 0 and n & (n - 1) == 0, f"{n} must be a power of two"
    return n.bit_length() - 1


def _cparams(sem, vmem=V7X_VMEM_LIMIT):
    return pltpu.CompilerParams(dimension_semantics=sem, vmem_limit_bytes=vmem)


def _split_bf16(a):
    hi = a.astype(BF16)
    lo = (a - hi.astype(F32)).astype(BF16)
    return hi, lo


def _dot(a, b):
    return jnp.dot(a, b, preferred_element_type=F32)


def _dot_nt(a, b):
    return lax.dot_general(a, b, (((1,), (1,)), ((), ())), preferred_element_type=F32)


def _store_row_tiles(ref, val):
    rows, d = val.shape
    rt = d // LANES
    for c in range(rt):
        ref[pl.ds(c, rows, stride=rt), :] = val[:, c * LANES:(c + 1) * LANES]


def _load_row_tiles(ref, rt):
    rows = ref.shape[0] // rt
    return jnp.concatenate([ref[pl.ds(c, rows, stride=rt), :] for c in range(rt)], axis=1)


def _dot3(a, b):
    ah, al = _split_bf16(a)
    bh, bl = _split_bf16(b)
    return _dot(ah, bh) + (_dot(ah, bl) + _dot(al, bh))


def _mod_kernel(c_ref, w_ref, b_ref, o_ref):
    c = c_ref[...]
    s = c * jax.nn.sigmoid(c)
    o_ref[...] = _dot3(s, w_ref[...]) + b_ref[...]


def _modulation(cc, w_mod, b_mod):
    rows, d = cc.shape
    n = w_mod.shape[1]
    tn = min(n, 1536)
    return pl.pallas_call(
        _mod_kernel,
        grid=(n // tn,),
        in_specs=[pl.BlockSpec((rows, d), lambda j: (0, 0)),
                  pl.BlockSpec((d, tn), lambda j: (0, j)),
                  pl.BlockSpec((1, tn), lambda j: (0, j))],
        out_specs=pl.BlockSpec((rows, tn), lambda j: (0, j)),
        out_shape=jax.ShapeDtypeStruct((rows, n), F32),
        compiler_params=_cparams(("parallel",)),
        name="modulation",
    )(cc, w_mod, b_mod.reshape(1, n))


def _rope_table_kernel(cos_ref, sin_ref, *, qk_dim):
    s, w = cos_ref.shape
    half = qk_dim // 2
    nf = half // 2
    t = lax.broadcasted_iota(I32, (s, w), 0)
    lane = lax.broadcasted_iota(I32, (s, w), 1)
    d = lane & (qk_dim - 1)
    j = d & (nf - 1)
    row = t >> _log2(GRID_W)
    col = t & (GRID_W - 1)
    pos = jnp.where(d < half, row, col).astype(F32)
    inv = jnp.exp(j.astype(F32) * (-math.log(ROPE_BASE) / nf))
    ang = pos * inv
    first = (d & (half - 1)) < nf
    cos_ref[...] = jnp.cos(ang)
    sn = jnp.sin(ang)
    sin_ref[...] = jnp.where(first, -sn, sn)


def _rope_tables(s, qk_dim):
    return pl.pallas_call(
        functools.partial(_rope_table_kernel, qk_dim=qk_dim),
        out_shape=(jax.ShapeDtypeStruct((s, LANES), F32), jax.ShapeDtypeStruct((s, LANES), F32)),
        name="rope_tables",
    )()


def _group_rms(t, gain, qk_dim):
    w = t.shape[1]
    r = lax.broadcasted_iota(I32, (w, w), 0) >> _log2(qk_dim)
    c = lax.broadcasted_iota(I32, (w, w), 1) >> _log2(qk_dim)
    bd = jnp.where(r == c, 1.0 / qk_dim, 0.0).astype(BF16)
    hi, lo = _split_bf16(t * t)
    ms = _dot(hi, bd) + _dot(lo, bd)
    return t * lax.rsqrt(ms + EPS) * gain


def _rope(t, cos, sin_signed, qk_dim):
    w = t.shape[1]
    nf = qk_dim // 4
    lane = lax.broadcasted_iota(I32, t.shape, 1)
    first = (lane & (2 * nf - 1)) < nf
    partner = jnp.where(first, pltpu.roll(t, w - nf, axis=1), pltpu.roll(t, nf, axis=1))
    return t * cos + partner * sin_signed


def _proj_kernel(*refs, latent, qk_cols, v_cols, qk_dim):
    if latent:
        (x_ref, sh_ref, sc_ref, g_ref, w_ref, qg_ref, kg_ref, cos_ref, sin_ref,
         q_out, k_out, v_out, u_out) = refs
    else:
        x_ref, sh_ref, sc_ref, g_ref, w_ref, kg_ref, k_out, v_out = refs
    x = x_ref[0]
    ms = jnp.mean(x * x, axis=-1, keepdims=True)
    h = (x * lax.rsqrt(ms + EPS) * g_ref[...]) * (1.0 + sc_ref[0]) + sh_ref[0]
    proj = _dot(h.astype(BF16), w_ref[...])
    if latent:
        reps = qk_cols // LANES
        cos = jnp.concatenate([cos_ref[...]] * reps, axis=1)
        sin = jnp.concatenate([sin_ref[...]] * reps, axis=1)
        q = _rope(_group_rms(proj[:, :qk_cols], qg_ref[...], qk_dim), cos, sin, qk_dim)
        q_out[0] = (q * (qk_dim ** -0.5 * math.log2(math.e))).astype(BF16)
        k = _rope(_group_rms(proj[:, qk_cols:2 * qk_cols], kg_ref[...], qk_dim), cos, sin, qk_dim)
        k_out[0] = k.astype(BF16)
        v_out[0] = proj[:, 2 * qk_cols:2 * qk_cols + v_cols].astype(BF16)
        u_out[0] = proj[:, 2 * qk_cols + v_cols:].astype(BF16)
    else:
        k = _group_rms(proj[:, :qk_cols], kg_ref[...], qk_dim)
        k_out[0] = k.astype(BF16)
        v_out[0] = proj[:, qk_cols:qk_cols + v_cols].astype(BF16)


def _project_latent(x, sh, sc, g, w_bf, qg, kg, cos_t, sin_t, qk_cols, v_cols, qk_dim):
    b, s, d = x.shape
    n = w_bf.shape[1]
    hy_cols = n - 2 * qk_cols - v_cols
    tm = min(ROW_TILE, s)
    row = lambda bi, i: (bi, i, 0)
    per_b = lambda bi, i: (bi, 0, 0)
    const = lambda bi, i: (0, 0)
    return pl.pallas_call(
        functools.partial(_proj_kernel, latent=True, qk_cols=qk_cols, v_cols=v_cols, qk_dim=qk_dim),
        grid=(b, s // tm),
        in_specs=[pl.BlockSpec((1, tm, d), row),
                  pl.BlockSpec((1, 1, d), per_b), pl.BlockSpec((1, 1, d), per_b),
                  pl.BlockSpec((1, d), const), pl.BlockSpec((d, n), const),
                  pl.BlockSpec((1, qk_cols), const), pl.BlockSpec((1, qk_cols), const),
                  pl.BlockSpec((tm, LANES), lambda bi, i: (i, 0)),
                  pl.BlockSpec((tm, LANES), lambda bi, i: (i, 0))],
        out_specs=[pl.BlockSpec((1, tm, qk_cols), row), pl.BlockSpec((1, tm, qk_cols), row),
                   pl.BlockSpec((1, tm, v_cols), row), pl.BlockSpec((1, tm, hy_cols), row)],
        out_shape=[jax.ShapeDtypeStruct((b, s, qk_cols), BF16), jax.ShapeDtypeStruct((b, s, qk_cols), BF16),
                   jax.ShapeDtypeStruct((b, s, v_cols), BF16), jax.ShapeDtypeStruct((b, s, hy_cols), BF16)],
        compiler_params=_cparams(("parallel", "parallel")),
        name="project_latent",
    )(x, sh, sc, g, w_bf, qg, kg, cos_t, sin_t)


def _project_context(ctx, sh, sc, g, w_bf, kg, qk_cols, v_cols, qk_dim):
    b, lc, d = ctx.shape
    n = w_bf.shape[1]
    tm = min(ROW_TILE, lc)
    row = lambda bi, i: (bi, i, 0)
    shared = lambda bi, i: (0, 0, 0)
    const = lambda bi, i: (0, 0)
    return pl.pallas_call(
        functools.partial(_proj_kernel, latent=False, qk_cols=qk_cols, v_cols=v_cols, qk_dim=qk_dim),
        grid=(b, lc // tm),
        in_specs=[pl.BlockSpec((1, tm, d), row),
                  pl.BlockSpec((1, 1, d), shared), pl.BlockSpec((1, 1, d), shared),
                  pl.BlockSpec((1, d), const), pl.BlockSpec((d, n), const),
                  pl.BlockSpec((1, qk_cols), const)],
        out_specs=[pl.BlockSpec((1, tm, qk_cols), row), pl.BlockSpec((1, tm, v_cols), row)],
        out_shape=[jax.ShapeDtypeStruct((b, lc, qk_cols), BF16), jax.ShapeDtypeStruct((b, lc, v_cols), BF16)],
        compiler_params=_cparams(("parallel", "parallel")),
        name="project_context",
    )(ctx, sh, sc, g, w_bf, kg)


def _key_chunks(kk, n):
    tiles = kk // LANES
    n = min(n, tiles)
    return [(LANES * (i * tiles // n), LANES * ((i + 1) * tiles // n)) for i in range(n)]


def _attn_kernel(q_ref, kc_ref, k_ref, vc_ref, v_ref, lq1, lk1, lq2, lk2, sg_ref, o_ref, kt_ref, v1_ref, *bufs,
                 lam_init, qk_dim):
    lam = (jnp.exp(jnp.sum(lq1[...] * lk1[...], axis=-1, keepdims=True))
           - jnp.exp(jnp.sum(lq2[...] * lk2[...], axis=-1, keepdims=True)) + lam_init)
    j = pl.program_id(2)
    even, odd = bufs[:4], bufs[4:]
    tq = q_ref.shape[1]
    chunks = _key_chunks(kt_ref.shape[1], ATT_KEY_CHUNKS)

    @pl.when(j == 0)
    def _():
        lc = kc_ref.shape[1]
        kt_ref[:, :lc] = kc_ref[0].T
        kt_ref[:, lc:] = k_ref[0].T
        v1_ref[:lc, :LANES] = vc_ref[0]
        v1_ref[lc:, :LANES] = v_ref[0]
        v1_ref[:, LANES:] = jnp.ones((v1_ref.shape[0], LANES), BF16)
        for ref in odd:
            ref[...] = jnp.zeros_like(ref)

    def step(cur, prev):
        s1_w, s2_w, m1_w, m2_w = cur
        s1_r, s2_r, m1_r, m2_r = prev
        q = q_ref[0]
        lane = lax.broadcasted_iota(I32, q.shape, 1)
        q1 = jnp.where(lane < qk_dim, q, jnp.zeros_like(q))
        q2 = jnp.where(lane >= qk_dim, q, jnp.zeros_like(q))
        m1p, m2p = m1_r[:, :1], m2_r[:, :1]
        m1 = m2 = jnp.full((tq, 1), -jnp.inf, F32)
        o1 = o2 = jnp.zeros((tq, 2 * LANES), F32)
        for c0, c1 in chunks:
            s1 = _dot(q1, kt_ref[:, c0:c1])
            s2 = _dot(q2, kt_ref[:, c0:c1])
            s1_w[:, c0:c1] = s1
            s2_w[:, c0:c1] = s2
            m1 = jnp.maximum(m1, jnp.max(s1, axis=-1, keepdims=True))
            m2 = jnp.maximum(m2, jnp.max(s2, axis=-1, keepdims=True))

            e1 = jnp.exp2(s1_r[:, c0:c1] - m1p)
            e2 = jnp.exp2(s2_r[:, c0:c1] - m2p)
            o1 = o1 + _dot(e1.astype(BF16), v1_ref[c0:c1, :])
            o2 = o2 + _dot(e2.astype(BF16), v1_ref[c0:c1, :])
        m1_w[...] = jnp.broadcast_to(m1, m1_w.shape)
        m2_w[...] = jnp.broadcast_to(m2, m2_w.shape)
        o = o1[:, :LANES] / o1[:, LANES:] - o2[:, :LANES] * (lam / o2[:, LANES:])
        ms = jnp.mean(o * o, axis=-1, keepdims=True)
        o_ref[0] = ((o * lax.rsqrt(ms + EPS) * sg_ref[...]) * (1.0 - lam_init)).astype(BF16)

    @pl.when((j & 1) == 0)
    def _():
        step(even, odd)

    @pl.when((j & 1) == 1)
    def _():
        step(odd, even)


def _diff_attention(q, k_c, k, v_c, v, lq1, lk1, lq2, lk2, subln_g, lam_init, qk_dim):
    b, s, w = q.shape
    lc = k_c.shape[1]
    assert lc % LANES == 0
    kk = lc + s
    tq = min(ATT_Q_TILE, s)
    nq = s // tq
    kv = lambda bi, h, i: (bi, 0, h)
    const = lambda bi, h, i: (0, 0)
    vec = pl.BlockSpec((1, qk_dim), const)
    wide = pltpu.VMEM((tq, kk), F32)
    stat = pltpu.VMEM((tq, LANES), F32)
    per_parity = [wide, wide, stat, stat]
    return pl.pallas_call(
        functools.partial(_attn_kernel, lam_init=lam_init, qk_dim=qk_dim),
        grid=(b, N_HEADS, nq + 1),
        in_specs=[pl.BlockSpec((1, tq, LANES), lambda bi, h, i: (bi, jnp.minimum(i, nq - 1), h)),
                  pl.BlockSpec((1, lc, LANES), kv), pl.BlockSpec((1, s, LANES), kv),
                  pl.BlockSpec((1, lc, LANES), kv), pl.BlockSpec((1, s, LANES), kv), vec, vec, vec, vec,
                  pl.BlockSpec((1, LANES), const)],
        out_specs=pl.BlockSpec((1, tq, LANES), lambda bi, h, i: (bi, jnp.maximum(i - 1, 0), h)),
        out_shape=jax.ShapeDtypeStruct((b, s, w), BF16),
        scratch_shapes=[pltpu.VMEM((LANES, kk), BF16), pltpu.VMEM((kk, 2 * LANES), BF16)] + per_parity + per_parity,
        compiler_params=_cparams(("parallel", "parallel", "arbitrary")),
        name="diff_attention",
    )(q, k_c, k, v_c, v, lq1, lk1, lq2, lk2, subln_g)


def _filter_kernel(w1_ref, b1_ref, f1_ref, w2_ref, b2_ref, f2_ref, w3_ref, o_ref, *, seq, hw):
    tl, n = o_ref.shape
    base = pl.program_id(0) * tl
    quarter = seq // RADIX

    def position(shape):
        p = lax.broadcasted_iota(I32, shape, 0) + base
        return (((p & (quarter - 1)) << _log2(RADIX)) | (p >> _log2(quarter))).astype(F32)

    pos = position((tl, LANES))
    lane = lax.broadcasted_iota(I32, (tl, LANES), 1)
    tn = pos / seq
    band_idx = jnp.where(lane <= N_BANDS, lane - 1, lane - 1 - N_BANDS).astype(F32)
    band = 1e-4 + band_idx * ((N_BANDS - 1 - 1e-4) / (N_BANDS - 1))
    ang = (2.0 * math.pi / seq) * pos * band
    feats = jnp.where(lane == 0, tn,
                      jnp.where(lane <= N_BANDS, jnp.sin(ang),
                                jnp.where(lane < FEAT_DIM, jnp.cos(ang), 0.0)))
    h = jnp.sin(f1_ref[...] * (_dot3(feats, w1_ref[...]) + b1_ref[...]))
    h = jnp.sin(f2_ref[...] * (_dot3(h, w2_ref[...]) + b2_ref[...]))
    h = _dot3(h, w3_ref[...])
    ch = (lax.broadcasted_iota(I32, (tl, n), 1) & ((1 << _log2(hw)) - 1)).astype(F32)
    lo = abs(math.log(DECAY_TARGET) / SLOW_DECAY_PCT)
    hi = abs(math.log(DECAY_TARGET) / FAST_DECAY_PCT)
    delta = lo + ch * ((hi - lo) / (hw - 1))
    o_ref[...] = (h * jnp.exp(-(position((tl, n)) / seq) * delta)).astype(BF16)


def _hyena_filters(seq, hw, w1, b1, f1, w2, b2, f2, w3):
    fh = w2.shape[0]
    n = w3.shape[1]
    w1p = jnp.zeros((LANES, fh), F32).at[:FEAT_DIM].set(w1)
    tl = min(ROW_TILE, seq)
    const = lambda i: (0, 0)
    return pl.pallas_call(
        functools.partial(_filter_kernel, seq=seq, hw=hw),
        grid=(seq // tl,),
        in_specs=[pl.BlockSpec((LANES, fh), const), pl.BlockSpec((1, fh), const), pl.BlockSpec((1, fh), const),
                  pl.BlockSpec((fh, fh), const), pl.BlockSpec((1, fh), const), pl.BlockSpec((1, fh), const),
                  pl.BlockSpec((fh, n), const)],
        out_specs=pl.BlockSpec((tl, n), lambda i: (i, 0)),
        out_shape=jax.ShapeDtypeStruct((seq, n), BF16),
        compiler_params=_cparams(("parallel",)),
        name="hyena_filters",
    )(w1p, b1.reshape(1, fh), f1.reshape(1, fh), w2, b2.reshape(1, fh), f2.reshape(1, fh), w3)


def _dft_kernel(mf_ref, mi_ref, tfc, tfs, tic, tis, *, seq):
    rows, q = tic.shape
    mask = (1 << _log2(4 * seq)) - 1
    unit = math.pi / (2 * seq)
    i_row = lax.broadcasted_iota(I32, (rows, q), 0)
    col = lax.broadcasted_iota(I32, (rows, q), 1)

    @pl.when(pl.program_id(0) == 0)
    def _():
        for r in range(RADIX):
            af = ((2 * i_row * (RADIX * col + r)) & mask).astype(F32) * unit
            tfc[r] = jnp.cos(af)
            tfs[r] = jnp.sin(af)
        ai = (((2 * col + 1) * (RADIX * i_row)) & mask).astype(F32) * unit
        tic[...] = jnp.cos(ai)
        tis[...] = jnp.sin(ai)

    r0 = pl.program_id(0) * rows
    c1 = lax.broadcasted_iota(I32, (1, q), 1)
    for r in range(RADIX):
        bf = (((2 * r0 + 1) * (RADIX * c1 + r)) & mask).astype(F32) * unit
        bi = (((2 * c1 + 1) * (RADIX * r0 + r)) & mask).astype(F32) * unit
        cbf, sbf = jnp.cos(bf), jnp.sin(bf)
        cbi, sbi = jnp.cos(bi), jnp.sin(bi)
        mf_ref[r, 0] = (cbf * tfc[r] - sbf * tfs[r]).astype(BF16)
        mf_ref[r, 1] = (sbf * tfc[r] + cbf * tfs[r]).astype(BF16)
        mi_ref[r, :, :q] = (cbi * tic[...] - sbi * tis[...]).astype(BF16)
        mi_ref[r, :, q:] = (sbi * tic[...] + cbi * tis[...]).astype(BF16)


def _dft_matrices(seq):
    q = seq // RADIX
    rows = min(DFT_TILE, q)
    return pl.pallas_call(
        functools.partial(_dft_kernel, seq=seq),
        grid=(q // rows,),
        out_specs=[pl.BlockSpec((RADIX, 2, rows, q), lambda i: (0, 0, i, 0)),
                   pl.BlockSpec((RADIX, rows, 2 * q), lambda i: (0, i, 0))],
        out_shape=[jax.ShapeDtypeStruct((RADIX, 2, q, q), BF16), jax.ShapeDtypeStruct((RADIX, q, 2 * q), BF16)],
        scratch_shapes=[pltpu.VMEM((RADIX, rows, q), F32), pltpu.VMEM((RADIX, rows, q), F32),
                        pltpu.VMEM((rows, q), F32), pltpu.VMEM((rows, q), F32)],
        compiler_params=_cparams(("arbitrary",)),
        name="dft_matrices",
    )()


def _class_transform(mf_ref, x_of_class):
    tc, ts = [], []
    for r in range(RADIX):
        x = x_of_class(r)
        tc.append(_dot(mf_ref[r, 0], x))
        ts.append(_dot(mf_ref[r, 1], x))
    return [(tc[0] + tc[1] + tc[2] + tc[3], ts[0] + ts[1] + ts[2] + ts[3]),
            (tc[0] - tc[1] + tc[2] - tc[3], ts[1] - ts[0] + ts[3] - ts[2]),
            (tc[0] - ts[1] - tc[2] + ts[3], ts[0] + tc[1] - ts[2] - tc[3]),
            (tc[0] + ts[1] - tc[2] - ts[3], tc[1] - ts[0] + ts[2] - tc[3])]


def _spectrum_kernel(mf_ref, h_ref, g_ref, *, seq, hw):
    q = seq // RADIX
    groups = _class_transform(mf_ref, lambda r: h_ref[r * q:(r + 1) * q, :])
    scale = 1.0 / seq
    for x, (hc, hs) in enumerate(groups):
        g_ref[0, x, 0] = ((hc[:, :hw] + hc[:, hw:]) * scale).astype(g_ref.dtype)
        g_ref[0, x, 1] = ((hs[:, :hw] - hs[:, hw:]) * scale).astype(g_ref.dtype)


def _filter_spectra(mf, hfilt, hw):
    q = mf.shape[2]
    seq = q * RADIX
    rows = min(DFT_TILE, q)
    return pl.pallas_call(
        functools.partial(_spectrum_kernel, seq=seq, hw=hw),
        grid=(HYENA_ORDER, q // rows),
        in_specs=[pl.BlockSpec((RADIX, 2, rows, q), lambda n, i: (0, 0, i, 0)),
                  pl.BlockSpec((seq, 2 * hw), lambda n, i: (0, n))],
        out_specs=pl.BlockSpec((1, RADIX, 2, rows, hw), lambda n, i: (n, 0, 0, i, 0)),
        out_shape=jax.ShapeDtypeStruct((HYENA_ORDER, RADIX, 2, q, hw), BF16),
        compiler_params=_cparams(("parallel", "parallel")),
        name="filter_spectra",
    )(mf, hfilt)


def _short_conv_kernel(u_ref, w_ref, b_ref, o_ref, y_ref):
    u = u_ref[0].astype(F32)
    s = u.shape[0]
    q = s // RADIX
    t = lax.broadcasted_iota(I32, u.shape, 0)
    prev = jnp.where(t == 0, 0.0, pltpu.roll(u, 1, axis=0))
    nxt = jnp.where(t == s - 1, 0.0, pltpu.roll(u, s - 1, axis=0))
    y_ref[...] = b_ref[...] + prev * w_ref[0:1, :] + u * w_ref[1:2, :] + nxt * w_ref[2:3, :]
    for r in range(RADIX):
        o_ref[0, r * q:(r + 1) * q, :] = y_ref[pl.ds(r, q, stride=RADIX), :].astype(o_ref.dtype)


def _short_conv(u, w, bias):
    b, s, c = u.shape
    tc = LANES
    return pl.pallas_call(
        _short_conv_kernel,
        grid=(b, c // tc),
        in_specs=[pl.BlockSpec((1, s, tc), lambda bi, j: (bi, 0, j)),
                  pl.BlockSpec((SHORT_CONV, tc), lambda bi, j: (0, j)),
                  pl.BlockSpec((1, tc), lambda bi, j: (0, j))],
        out_specs=pl.BlockSpec((1, s, tc), lambda bi, j: (bi, 0, j)),
        out_shape=jax.ShapeDtypeStruct((b, s, c), BF16),
        scratch_shapes=[pltpu.VMEM((s, tc), F32)],
        compiler_params=_cparams(("parallel", "parallel")),
        name="short_conv",
    )(u, w, bias.reshape(1, c))


def _fwd_dft_kernel(mf_ref, z_ref, g_ref, y_ref, zb):
    q = zb.shape[0] // RADIX

    @pl.when(pl.program_id(1) == 0)
    def _():
        zb[...] = z_ref[0].astype(BF16)

    groups = _class_transform(mf_ref, lambda r: zb[r * q:(r + 1) * q, :])
    yc, ys = [], []
    for x, (uc, us) in enumerate(groups):
        gc, gs = g_ref[0, x, 0].astype(F32), g_ref[0, x, 1].astype(F32)
        yc.append(uc * gc - us * gs)
        ys.append(uc * gs + us * gc)
    a, b, c, d = range(RADIX)
    z = [(yc[a] + yc[c] + yc[b] + yc[d], ys[a] + ys[c] - ys[b] - ys[d]),
         (yc[a] + ys[c] - yc[b] + ys[d], ys[a] - yc[c] + ys[b] + yc[d]),
         (yc[a] - yc[c] + yc[b] - yc[d], ys[a] - ys[c] - ys[b] + ys[d]),
         (yc[a] - ys[c] - yc[b] - ys[d], ys[a] + yc[c] + ys[b] - yc[d])]
    for r, (zc, zs) in enumerate(z):
        y_ref[0, r, 0] = zc.astype(BF16)
        y_ref[0, r, 1] = zs.astype(BF16)


def _fwd_dft(mf, z, z_col, g, order, hw):
    b, seq = z.shape[0], z.shape[1]
    q = seq // RADIX
    rows = min(DFT_TILE, q)
    return pl.pallas_call(
        _fwd_dft_kernel,
        grid=(b, q // rows),
        in_specs=[pl.BlockSpec((RADIX, 2, rows, q), lambda bi, i: (0, 0, i, 0)),
                  pl.BlockSpec((1, seq, hw), lambda bi, i: (bi, 0, z_col)),
                  pl.BlockSpec((1, RADIX, 2, rows, hw), lambda bi, i: (order, 0, 0, i, 0))],
        out_specs=pl.BlockSpec((1, RADIX, 2, rows, hw), lambda bi, i: (bi, 0, 0, i, 0)),
        out_shape=jax.ShapeDtypeStruct((b, RADIX, 2, q, hw), BF16),
        scratch_shapes=[pltpu.VMEM((seq, hw), BF16)],
        compiler_params=_cparams(("parallel", "arbitrary")),
        name="hyena_fwd_dft",
    )(mf, z, g)


def _inv_dft_kernel(mi_ref, y_ref, z_ref, gate_ref, skip_ref, *rest, final):
    if final:
        og_ref, o_ref = rest
    else:
        (o_ref,) = rest
    rows = mi_ref.shape[1]
    for r in range(RADIX):
        conv = _dot(mi_ref[r], y_ref[0, r])
        z = gate_ref[0, r].astype(F32) * (conv + z_ref[0, r].astype(F32) * skip_ref[0])
        if final:
            ms = jnp.mean(z * z, axis=-1, keepdims=True)
            zn = z * lax.rsqrt(ms + EPS) * og_ref[...]
            for c in range(zn.shape[1] // LANES):
                o_ref[0, c, pl.ds(r, rows, stride=RADIX), :] = zn[:, c * LANES:(c + 1) * LANES]
        else:
            o_ref[0, r] = z.astype(o_ref.dtype)


def _inv_dft(mi, y, z, z_col, gates, gate_col, skip, order, out_g, hw):
    b, seq = z.shape[0], z.shape[1]
    q = seq // RADIX
    rows = min(DFT_TILE, q)
    final = out_g is not None
    by_class = lambda a: a.reshape(b, RADIX, q, a.shape[2])
    in_specs = [pl.BlockSpec((RADIX, rows, 2 * q), lambda bi, i: (0, i, 0)),
                pl.BlockSpec((1, RADIX, 2 * q, hw), lambda bi, i: (bi, 0, 0, 0)),
                pl.BlockSpec((1, RADIX, rows, hw), lambda bi, i: (bi, 0, i, z_col)),
                pl.BlockSpec((1, RADIX, rows, hw), lambda bi, i: (bi, 0, i, gate_col)),
                pl.BlockSpec((1, 1, hw), lambda bi, i: (order, 0, 0))]
    args = [mi, y.reshape(b, RADIX, 2 * q, hw), by_class(z), by_class(gates), skip.reshape(HYENA_ORDER, 1, hw)]
    if final:
        in_specs.append(pl.BlockSpec((1, hw), lambda bi, i: (0, 0)))
        args.append(out_g.reshape(1, hw))
        out_spec = pl.BlockSpec((1, hw // LANES, RADIX * rows, LANES), lambda bi, i: (bi, 0, i, 0))
        out_shape = jax.ShapeDtypeStruct((b, hw // LANES, seq, LANES), F32)
    else:
        out_spec = pl.BlockSpec((1, RADIX, rows, hw), lambda bi, i: (bi, 0, i, 0))
        out_shape = jax.ShapeDtypeStruct((b, RADIX, q, hw), BF16)
    out = pl.pallas_call(
        functools.partial(_inv_dft_kernel, final=final),
        grid=(b, q // rows),
        in_specs=in_specs,
        out_specs=out_spec,
        out_shape=out_shape,
        compiler_params=_cparams(("parallel", "parallel")),
        name="hyena_inv_dft",
    )(*args)
    return out if final else out.reshape(b, seq, hw)


def _out_kernel(a_ref, hy_ref, x_ref, wo_ref, g1_ref, sh_ref, sc_ref, n2_ref, rw_ref, rb_ref,
                xn_ref, h2_ref, lg_ref):
    aw = a_ref.shape[2]
    hy = jnp.concatenate([hy_ref[0, c] for c in range(hy_ref.shape[1])], axis=1).astype(BF16)
    mix = _dot(a_ref[0], wo_ref[:aw, :]) + _dot(hy, wo_ref[aw:, :])
    xn = x_ref[0] + g1_ref[0] * mix
    xn_ref[0] = xn
    ms = jnp.mean(xn * xn, axis=-1, keepdims=True)
    h2 = (xn * lax.rsqrt(ms + EPS) * n2_ref[...]) * (1.0 + sc_ref[0]) + sh_ref[0]
    _store_row_tiles(h2_ref, h2)
    hh, hl = _split_bf16(h2)
    wh, wl = _split_bf16(rw_ref[...])
    lg_ref[...] = _dot_nt(wh, hh) + (_dot_nt(wh, hl) + _dot_nt(wl, hh)) + rb_ref[...]


def _out_project(attn, hyn, x, wo_bf, g1, sh2, sc2, n2g, rw_t, rb):
    b, s, d = x.shape
    aw, hw = attn.shape[2], hyn.shape[1] * hyn.shape[3]
    ne = rw_t.shape[0]
    tm = min(ROW_TILE, s)
    nt = s // tm
    row = lambda bi, i: (bi, i, 0)
    per_b = lambda bi, i: (bi, 0, 0)
    const = lambda bi, i: (0, 0)
    return pl.pallas_call(
        _out_kernel,
        grid=(b, nt),
        in_specs=[pl.BlockSpec((1, tm, aw), row),
                  pl.BlockSpec((1, hw // LANES, tm, LANES), lambda bi, i: (bi, 0, i, 0)),
                  pl.BlockSpec((1, tm, d), row),
                  pl.BlockSpec((aw + hw, d), const),
                  pl.BlockSpec((1, 1, d), per_b), pl.BlockSpec((1, 1, d), per_b), pl.BlockSpec((1, 1, d), per_b),
                  pl.BlockSpec((1, d), const), pl.BlockSpec((ne, d), const), pl.BlockSpec((ne, 1), const)],
        out_specs=[pl.BlockSpec((1, tm, d), row),
                   pl.BlockSpec((tm * (d // LANES), LANES), lambda bi, i: (bi * nt + i, 0)),
                   pl.BlockSpec((ne, tm), lambda bi, i: (0, bi * nt + i))],
        out_shape=[jax.ShapeDtypeStruct((b, s, d), F32), jax.ShapeDtypeStruct((b * s * (d // LANES), LANES), F32),
                   jax.ShapeDtypeStruct((ne, b * s), F32)],
        compiler_params=_cparams(("parallel", "parallel")),
        name="out_project",
    )(attn, hyn, x, wo_bf, g1, sh2, sc2, n2g, rw_t, rb)


def _route_kernel(lg_ref, gate_ref, lpos_ref, tstart_ref, tcnt_ref, cnt_ref, carry, *, rt):
    ne, tl = lg_ref.shape

    @pl.when(pl.program_id(0) == 0)
    def _():
        carry[...] = jnp.zeros_like(carry)

    l = lg_ref[...]
    rows = lax.broadcasted_iota(I32, (ne, tl), 0).astype(F32)
    vals, sels = [], []
    for k in range(TOP_K):
        m = jnp.max(l, axis=0, keepdims=True)
        ik = jnp.min(jnp.where(l == m, rows, float(ne)), axis=0, keepdims=True)
        sel = rows == ik
        vals.append(m)
        sels.append(sel)
        l = jnp.where(sel, -jnp.inf, l)
    exps = [jnp.exp(v - vals[0]) for v in vals]
    denom = exps[0] + exps[1] + exps[2] + exps[3]
    for k in range(TOP_K):
        gate_ref[k:k + 1, :] = exps[k] / denom
    oh = jnp.zeros((ne, tl), F32)
    for sel in sels:
        oh = oh + jnp.where(sel, 1.0, 0.0)
    r = lax.broadcasted_iota(I32, (tl, tl), 0)
    c = lax.broadcasted_iota(I32, (tl, tl), 1)
    tri = jnp.where(r <= c, 1.0, 0.0).astype(BF16)
    cum = _dot(oh.astype(BF16), tri)
    n_col = jnp.sum(oh, axis=1, keepdims=True)
    er = lax.broadcasted_iota(I32, (ne, LANES), 0)
    ec = lax.broadcasted_iota(I32, (ne, LANES), 1)
    to_lane = lambda col: jnp.sum(jnp.where(er == ec, jnp.broadcast_to(col, (ne, LANES)), 0.0),
                                  axis=0, keepdims=True)
    n_lane = to_lane(n_col)
    off_col = jnp.sum(jnp.where(ec < er, jnp.broadcast_to(n_lane, (ne, LANES)), 0.0), axis=1, keepdims=True)
    slab_pos = cum - oh + off_col
    for k in range(TOP_K):
        pos = jnp.sum(jnp.where(sels[k], slab_pos, 0.0), axis=0, keepdims=True)
        lpos_ref[k:k + 1, :] = (pos * rt).astype(I32)
    tstart_ref[0] = to_lane(carry[:, 0:1]).astype(I32)
    tcnt_ref[0] = n_lane.astype(I32)
    carry[...] = carry[...] + n_col
    cnt_ref[...] = carry[...]


def _route(logits_t, tl, rt):
    ne, t = logits_t.shape
    nt = t // tl
    blk = lambda i: (0, i)
    per_tile = pl.BlockSpec((1, 1, LANES), lambda i: (i, 0, 0))
    return pl.pallas_call(
        functools.partial(_route_kernel, rt=rt),
        grid=(nt,),
        in_specs=[pl.BlockSpec((ne, tl), blk)],
        out_specs=[pl.BlockSpec((TOP_K, tl), blk), pl.BlockSpec((TOP_K, tl), blk), per_tile, per_tile,
                   pl.BlockSpec((ne, LANES), lambda i: (0, 0))],
        out_shape=[jax.ShapeDtypeStruct((TOP_K, t), F32), jax.ShapeDtypeStruct((TOP_K, t), I32),
                   jax.ShapeDtypeStruct((nt, 1, LANES), I32), jax.ShapeDtypeStruct((nt, 1, LANES), I32),
                   jax.ShapeDtypeStruct((ne, LANES), F32)],
        scratch_shapes=[pltpu.VMEM((ne, LANES), F32)],
        compiler_params=_cparams(("arbitrary",)),
        name="moe_route",
    )(logits_t)


def _slots_kernel(cnt_ref, tstart_ref, run_ref, blk_ref, meta_ref, *, rows_per_block):
    ne = cnt_ref.shape[0]
    shift = _log2(rows_per_block)
    cnt = cnt_ref[...].astype(I32)
    padded = ((cnt + (rows_per_block - 1)) >> shift) << shift
    r = lax.broadcasted_iota(I32, (ne, LANES), 0)
    c = lax.broadcasted_iota(I32, (ne, LANES), 1)
    padded_lane = jnp.sum(jnp.where(r == c, padded, 0), axis=0, keepdims=True)
    cnt_lane = jnp.sum(jnp.where(r == c, cnt, 0), axis=0, keepdims=True)
    pend_lane = jnp.sum(jnp.where(r <= c, padded, 0), axis=0, keepdims=True)
    pend_col = jnp.sum(jnp.where(c <= r, jnp.broadcast_to(padded_lane, (ne, LANES)), 0),
                       axis=1, keepdims=True)
    run_ref[...] = tstart_ref[...] + (pend_lane - padded_lane)
    nbp = blk_ref.shape[1]
    j0 = lax.broadcasted_iota(I32, (ne, nbp), 1) * rows_per_block
    be = jnp.sum(jnp.where(jnp.broadcast_to(pend_col, (ne, nbp)) <= j0, 1, 0), axis=0, keepdims=True)
    blk_ref[...] = jnp.minimum(be, ne - 1)
    total = jnp.max(pend_col, axis=0, keepdims=True)
    meta_ref[0:1, :] = pend_lane - padded_lane + cnt_lane
    meta_ref[1:2, :] = padded_lane - cnt_lane
    meta_ref[2:3, :] = jnp.broadcast_to(total >> shift, (1, LANES))
    meta_ref[3:8, :] = jnp.zeros((5, LANES), I32)


def _slots(cnt, tstart, n_blocks, rows_per_block):
    nbp = -(-n_blocks // LANES) * LANES
    return pl.pallas_call(
        functools.partial(_slots_kernel, rows_per_block=rows_per_block),
        out_shape=[jax.ShapeDtypeStruct(tstart.shape, I32), jax.ShapeDtypeStruct((1, nbp), I32),
                   jax.ShapeDtypeStruct((8, LANES), I32)],
        compiler_params=pltpu.CompilerParams(vmem_limit_bytes=V7X_VMEM_LIMIT),
        name="moe_slots",
    )(cnt, tstart)


def _pad_chunks(rows_per_block):
    sizes, s = [], rows_per_block // 2
    while s >= 1:
        sizes.append(s)
        s //= 2
    return sizes


def _rows(start, size, rt):
    return pl.ds(pl.multiple_of(start * rt, rt), size * rt)


def _for_each_run_chunk(run_ref, cnt_ref, tile, ne, max_rows, act):
    sizes = _pad_chunks(2 * max_rows)

    def each(e, off):
        left = cnt_ref[tile * ne + e]
        pos, start = off, run_ref[tile * ne + e]
        for size in sizes:
            hit = (left & size) != 0

            @pl.when(hit)
            def _():
                act(pos, start, size)

            inc = jnp.where(hit, size, 0)
            pos, start = pos + inc, start + inc
        return off + left

    lax.fori_loop(0, ne, each, 0)


def _dispatch_kernel(run_ref, cnt_ref, padlo_ref, npad_ref, nused_ref, lpos_ref, h_ref, buf_ref,
                     slab, zeros, sems, zsem, *, rows_per_block, rt, n_tiles):
    tl = lpos_ref.shape[1]
    ne = padlo_ref.shape[0]
    sizes = _pad_chunks(rows_per_block)
    half = rows_per_block // 2
    i = pl.program_id(0)
    slot = i & 1

    def pad_copy(start, size):
        return pltpu.make_async_copy(zeros.at[_rows(0, size, rt)], buf_ref.at[_rows(start, size, rt)], zsem)

    def run_copies(tile, sl, wait):
        def act(pos, start, size):
            cp = pltpu.make_async_copy(slab.at[sl, _rows(pos, size, rt)], buf_ref.at[_rows(start, size, rt)],
                                       sems.at[sl])
            if wait:
                cp.wait()
            else:
                cp.start()

        _for_each_run_chunk(run_ref, cnt_ref, tile, ne, tl, act)

    @pl.when(i == 0)
    def _():
        zeros[...] = jnp.zeros_like(zeros)

        first, last = 2 * nused_ref[0], buf_ref.shape[0] // (half * rt)
        lax.fori_loop(first, last, lambda j, c: (pad_copy(j * half, half).start(), c)[1], 0)
        lax.fori_loop(first, last, lambda j, c: (pad_copy(j * half, half).wait(), c)[1], 0)

        def each(e, wait):
            start = padlo_ref[e]
            left = npad_ref[e]
            for size in sizes:
                hit = (left & size) != 0

                @pl.when(hit)
                def _():
                    cp = pad_copy(start, size)
                    if wait:
                        cp.wait()
                    else:
                        cp.start()

                start = start + jnp.where(hit, size, 0)

        lax.fori_loop(0, ne, lambda e, c: (each(e, False), c)[1], 0)
        lax.fori_loop(0, ne, lambda e, c: (each(e, True), c)[1], 0)

    def step(sl):
        @pl.when(i >= 2)
        def _():
            run_copies(i - 2, sl, True)

        def fill(t, c):
            row = h_ref[_rows(t, 1, rt), :]
            for k in range(TOP_K):
                slab[sl, pl.ds(pl.multiple_of(lpos_ref[k, t], rt), rt), :] = row
            return c

        lax.fori_loop(0, tl, fill, 0, unroll=8)
        run_copies(i, sl, False)

        @pl.when(i == n_tiles - 1)
        def _():
            if n_tiles >= 2:
                run_copies(i - 1, 1 - sl, True)
            run_copies(i, sl, True)

    for sl in range(2):
        pl.when(slot == sl)(functools.partial(step, sl))


def _dispatch(h2r, lpos, run_start, run_cnt, pad_lo, n_pad, n_used, n_rows, rows_per_block, rt, tl):
    n_tiles = lpos.shape[1] // tl
    return pl.pallas_call(
        functools.partial(_dispatch_kernel, rows_per_block=rows_per_block, rt=rt, n_tiles=n_tiles),
        grid_spec=pltpu.PrefetchScalarGridSpec(
            num_scalar_prefetch=5,
            grid=(n_tiles,),
            in_specs=[pl.BlockSpec((TOP_K, tl), lambda i, *_: (0, i), memory_space=pltpu.SMEM),
                      pl.BlockSpec((tl * rt, LANES), lambda i, *_: (i, 0))],
            out_specs=pl.BlockSpec(memory_space=pl.ANY),
            scratch_shapes=[pltpu.VMEM((2, TOP_K * tl * rt, LANES), F32),
                            pltpu.VMEM((rows_per_block // 2 * rt, LANES), F32),
                            pltpu.SemaphoreType.DMA((2,)), pltpu.SemaphoreType.DMA(())]),
        out_shape=jax.ShapeDtypeStruct((n_rows * rt, LANES), F32),
        compiler_params=_cparams(("arbitrary",)),
        name="moe_dispatch",
    )(run_start, run_cnt, pad_lo, n_pad, n_used, lpos, h2r)


def _expert_kernel(be_ref, nu_ref, x_ref, w1_ref, b1_ref, w2_ref, b2_ref, o_ref, w1b, w2b):
    j = pl.program_id(0)
    active = j < nu_ref[0]

    @pl.when(active & ((j == 0) | (be_ref[j] != be_ref[jnp.maximum(j - 1, 0)])))
    def _():
        w1b[...] = w1_ref[0].astype(BF16)
        w2b[...] = w2_ref[0].astype(BF16)

    @pl.when(active)
    def _():
        de = w2b.shape[0]
        gl = _dot(_load_row_tiles(x_ref, w1b.shape[0] // LANES).astype(BF16), w1b[...]) + b1_ref[0]
        g = jnp.minimum(gl[:, :de], SWIGLU_LIMIT)
        lin = jnp.clip(gl[:, de:], -SWIGLU_LIMIT, SWIGLU_LIMIT)
        glu = g * jax.nn.sigmoid(SWIGLU_ALPHA * g)
        _store_row_tiles(o_ref, _dot(((lin + 1.0) * glu).astype(BF16), w2b[...]) + b2_ref[0])

    @pl.when(pl.program_id(0) >= nu_ref[0])
    def _():
        o_ref[...] = jnp.zeros_like(o_ref)


def _experts(buf, block_e, n_used, w1, b1, w2, b2, rows_per_block):
    ne, d, d2 = w1.shape
    de = w2.shape[1]
    blk_shape = (rows_per_block * (d // LANES), LANES)
    nb = buf.shape[0] // blk_shape[0]
    rowblk = lambda j, be, nu: (jnp.minimum(j, nu[0] - 1), 0)
    by_e = lambda j, be, nu: (be[j], 0, 0)
    return pl.pallas_call(
        _expert_kernel,
        grid_spec=pltpu.PrefetchScalarGridSpec(
            num_scalar_prefetch=2,
            grid=(nb,),
            in_specs=[pl.BlockSpec(blk_shape, rowblk),
                      pl.BlockSpec((1, d, d2), by_e), pl.BlockSpec((1, 1, d2), by_e),
                      pl.BlockSpec((1, de, d), by_e), pl.BlockSpec((1, 1, d), by_e)],
            out_specs=pl.BlockSpec(blk_shape, lambda j, be, nu: (j, 0)),
            scratch_shapes=[pltpu.VMEM((d, d2), BF16), pltpu.VMEM((de, d), BF16)]),
        out_shape=jax.ShapeDtypeStruct(buf.shape, F32),
        compiler_params=_cparams(("arbitrary",)),
        name="moe_experts",
    )(block_e, n_used, buf, w1, b1.reshape(ne, 1, d2), w2, b2.reshape(ne, 1, d))


def _combine_kernel(run_ref, cnt_ref, lpos_ref, gate_ref, xn_ref, g2_ref, ob_ref, o_ref, slab, acc, sems,
                    *, rt, ne, n_tiles):
    tl = lpos_ref.shape[1]
    i = pl.program_id(0) * pl.num_programs(1) + pl.program_id(1)
    slot = i & 1

    def run_copies(tile, sl, wait):
        def act(pos, start, size):
            cp = pltpu.make_async_copy(ob_ref.at[_rows(start, size, rt)], slab.at[sl, _rows(pos, size, rt)],
                                       sems.at[sl])
            if wait:
                cp.wait()
            else:
                cp.start()

        _for_each_run_chunk(run_ref, cnt_ref, tile, ne, tl, act)

    def step(sl):
        @pl.when(i == 0)
        def _():
            run_copies(i, sl, False)

        @pl.when(i + 1 < n_tiles)
        def _():
            run_copies(i + 1, 1 - sl, False)

        run_copies(i, sl, True)

        def row(k, t):
            return slab[sl, pl.ds(pl.multiple_of(lpos_ref[k, t], rt), rt), :]

        def token(t, c):
            a = gate_ref[0, t] * row(0, t)
            for k in range(1, TOP_K):
                a = a + gate_ref[k, t] * row(k, t)
            acc[_rows(t, 1, rt), :] = a
            return c

        lax.fori_loop(0, tl, token, 0, unroll=8)
        o_ref[0] = xn_ref[0] + g2_ref[0] * _load_row_tiles(acc, rt)

    for sl in range(2):
        pl.when(slot == sl)(functools.partial(step, sl))


def _combine(out_buf, lpos, gates, run_start, run_cnt, xn, g2, rt, tl):
    b, s, d = xn.shape
    nt = s // tl
    tok = lambda bi, i, *_: (0, bi * nt + i)
    return pl.pallas_call(
        functools.partial(_combine_kernel, rt=rt, ne=N_EXPERTS, n_tiles=b * nt),
        grid_spec=pltpu.PrefetchScalarGridSpec(
            num_scalar_prefetch=2,
            grid=(b, nt),
            in_specs=[pl.BlockSpec((TOP_K, tl), tok, memory_space=pltpu.SMEM),
                      pl.BlockSpec((TOP_K, tl), tok, memory_space=pltpu.SMEM),
                      pl.BlockSpec((1, tl, d), lambda bi, i, *_: (bi, i, 0)),
                      pl.BlockSpec((1, 1, d), lambda bi, i, *_: (bi, 0, 0)),
                      pl.BlockSpec(memory_space=pl.ANY)],
            out_specs=pl.BlockSpec((1, tl, d), lambda bi, i, *_: (bi, i, 0)),
            scratch_shapes=[pltpu.VMEM((2, TOP_K * tl * rt, LANES), F32), pltpu.VMEM((tl * rt, LANES), F32),
                            pltpu.SemaphoreType.DMA((2,))]),
        out_shape=jax.ShapeDtypeStruct((b, s, d), F32),
        compiler_params=_cparams(("arbitrary", "arbitrary")),
        name="moe_combine",
    )(run_start, run_cnt, lpos, gates, xn, g2, out_buf)


def _moe(h2r, logits_t, xn, g2, w1, b1, w2, b2):
    b, s, d = xn.shape
    t = b * s
    rt = d // LANES
    tl = min(TOKEN_TILE, s)
    assert s % tl == 0
    n_blocks = (t * TOP_K) // EXPERT_ROWS + N_EXPERTS
    n_rows = n_blocks * EXPERT_ROWS
    gates, lpos, tstart, tcnt, cnt = _route(logits_t, tl, rt)
    run, blk, meta = _slots(cnt, tstart, n_blocks, EXPERT_ROWS)
    run_start = run[:, 0, :N_EXPERTS].reshape(-1)
    run_cnt = tcnt[:, 0, :N_EXPERTS].reshape(-1)
    buf = _dispatch(h2r, lpos, run_start, run_cnt, meta[0, :N_EXPERTS], meta[1, :N_EXPERTS], meta[2, :1],
                    n_rows, EXPERT_ROWS, rt, tl)
    out_buf = _experts(buf, blk[0, :n_blocks], meta[2, :1], w1, b1, w2, b2, EXPERT_ROWS)
    return _combine(out_buf, lpos, gates, run_start, run_cnt, xn, g2, rt, tl)


def _layer(x, ctx, c, c_ctx, p, lam_init):
    b, s, d = x.shape
    attn_w = d // 2
    hw = d - attn_w
    v_dim = attn_w // N_HEADS
    qk_dim = v_dim // 2
    qk_cols = N_HEADS * 2 * qk_dim
    v_cols = N_HEADS * v_dim
    assert 2 * qk_dim == LANES and v_dim == LANES and s % GRID_W == 0

    rows = -(-(b + 1) // 8) * 8
    cc = jnp.zeros((rows, d), F32).at[:b].set(c).at[b].set(c_ctx)
    mod = _modulation(cc, p['w_mod'], p['b_mod'])
    mod_x = mod[:b].reshape(b, N_MOD, 1, d)
    sh1, sc1, g1, sh2, sc2, g2 = [mod_x[:, i] for i in range(N_MOD)]
    mod_c = mod[b:b + 1].reshape(1, N_MOD, 1, d)
    csh1, csc1 = mod_c[:, 0], mod_c[:, 1]

    w_in_bf = p['w_in'].astype(BF16)
    qg = jnp.tile(p['q_norm_g'], qk_cols // qk_dim).reshape(1, qk_cols)
    kg = jnp.tile(p['k_norm_g'], qk_cols // qk_dim).reshape(1, qk_cols)
    n1g = p['norm1_g'].reshape(1, d)
    cos_t, sin_t = _rope_tables(s, qk_dim)
    q, k, v, u_hy = _project_latent(x, sh1, sc1, n1g, w_in_bf, qg, kg, cos_t, sin_t, qk_cols, v_cols, qk_dim)
    k_c, v_c = _project_context(ctx, csh1, csc1, n1g, w_in_bf[:, qk_cols:2 * qk_cols + v_cols], kg,
                                qk_cols, v_cols, qk_dim)
    vec = lambda a: a.reshape(1, qk_dim)
    attn = _diff_attention(q, k_c, k, v_c, v, vec(p['lam_q1']), vec(p['lam_k1']), vec(p['lam_q2']),
                           vec(p['lam_k2']), p['subln_g'].reshape(1, v_dim), lam_init, qk_dim)

    hfilt = _hyena_filters(s, hw, p['hy_w1'], p['hy_b1'], p['hy_f1'], p['hy_w2'], p['hy_b2'], p['hy_f2'], p['hy_w3'])
    mf, mi = _dft_matrices(s)
    g_spec = _filter_spectra(mf, hfilt, hw)
    uc = _short_conv(u_hy, p['hy_conv_w'], p['hy_conv_b'])
    y1 = _fwd_dft(mf, uc, 0, g_spec, 0, hw)
    z1 = _inv_dft(mi, y1, uc, 0, uc, 1, p['hy_skip'], 0, None, hw)
    y2 = _fwd_dft(mf, z1, 0, g_spec, 1, hw)
    hyn = _inv_dft(mi, y2, z1, 0, uc, 2, p['hy_skip'], 1, p['hy_out_g'], hw)

    xn, h2, logits_t = _out_project(attn, hyn, x, p['w_out'].astype(BF16), g1, sh2, sc2,
                                    p['norm2_g'].reshape(1, d), p['router_w'].T,
                                    p['router_b'].reshape(N_EXPERTS, 1))
    return _moe(h2, logits_t, xn, g2, p['exp_w1'], p['exp_b1'], p['exp_w2'], p['exp_b2'])


def kernel(x, c, ctx, c_ctx, w_mod, b_mod, norm1_g, norm2_g, w_in, q_norm_g, k_norm_g, lam_q1, lam_k1, lam_q2, lam_k2, subln_g, hy_conv_w, hy_conv_b, hy_w1, hy_b1, hy_f1, hy_w2, hy_b2, hy_f2, hy_w3, hy_skip, hy_out_g, w_out, router_w, router_b, exp_w1, exp_b1, exp_w2, exp_b2):
    depth = w_mod.shape[0]
    assert depth == 1, "context-token update between layers is not implemented"
    p = {
        'w_mod': w_mod[0], 'b_mod': b_mod[0], 'norm1_g': norm1_g[0], 'norm2_g': norm2_g[0],
        'w_in': w_in[0], 'q_norm_g': q_norm_g[0], 'k_norm_g': k_norm_g[0],
        'lam_q1': lam_q1[0], 'lam_k1': lam_k1[0], 'lam_q2': lam_q2[0], 'lam_k2': lam_k2[0],
        'subln_g': subln_g[0], 'hy_conv_w': hy_conv_w[0], 'hy_conv_b': hy_conv_b[0],
        'hy_w1': hy_w1[0], 'hy_b1': hy_b1[0], 'hy_f1': hy_f1[0], 'hy_w2': hy_w2[0],
        'hy_b2': hy_b2[0], 'hy_f2': hy_f2[0], 'hy_w3': hy_w3[0], 'hy_skip': hy_skip[0],
        'hy_out_g': hy_out_g[0], 'w_out': w_out[0], 'router_w': router_w[0],
        'router_b': router_b[0], 'exp_w1': exp_w1[0], 'exp_b1': exp_b1[0],
        'exp_w2': exp_w2[0], 'exp_b2': exp_b2[0],
    }
    lam_init = 0.8 - 0.6 * math.exp(-0.3 * 0)
    return _layer(x, ctx, c, c_ctx, p, lam_init)
```

```python
import functools
import math

import jax
import jax.numpy as jnp
from jax import lax
from jax.experimental import pallas as pl
from jax.experimental.pallas import tpu as pltpu

F32 = jnp.float32
BF16 = jnp.bfloat16
I32 = jnp.int32

GRID_W = 64
N_HEADS = 4
N_MOD = 6
SHORT_CONV = 3
HYENA_ORDER = 2
N_BANDS = 8
FEAT_DIM = 1 + 2 * N_BANDS
FILTER_HIDDEN = 64
DECAY_TARGET = 1e-2
FAST_DECAY_PCT = 0.3
SLOW_DECAY_PCT = 1.5
N_EXPERTS = 32
TOP_K = 4
SWIGLU_LIMIT = 7.0
SWIGLU_ALPHA = 1.702
ROPE_BASE = 10000.0
EPS = 1e-6

LANES = 128
V7X_VMEM_LIMIT = 56 * 1024 * 1024

ROW_TILE = 512
ATT_Q_TILE = 256
ATT_KEY_CHUNKS = 17
DFT_TILE = 256
RADIX = 4
EXPERT_ROWS = 512
TOKEN_TILE = 512


def _log2(n):
    assert n > 0 and n & (n - 1) == 0, f"{n} must be a power of two"
    return n.bit_length() - 1


def _cparams(sem, vmem=V7X_VMEM_LIMIT):
    return pltpu.CompilerParams(dimension_semantics=sem, vmem_limit_bytes=vmem)


def _split_bf16(a):
    hi = a.astype(BF16)
    lo = (a - hi.astype(F32)).astype(BF16)
    return hi, lo


def _dot(a, b):
    return jnp.dot(a, b, preferred_element_type=F32)


def _dot_nt(a, b):
    return lax.dot_general(a, b, (((1,), (1,)), ((), ())), preferred_element_type=F32)


def _store_row_tiles(ref, val):
    rows, d = val.shape
    rt = d // LANES
    for c in range(rt):
        ref[pl.ds(c, rows, stride=rt), :] = val[:, c * LANES:(c + 1) * LANES]


def _load_row_tiles(ref, rt):
    rows = ref.shape[0] // rt
    return jnp.concatenate([ref[pl.ds(c, rows, stride=rt), :] for c in range(rt)], axis=1)


def _dot3(a, b):
    ah, al = _split_bf16(a)
    bh, bl = _split_bf16(b)
    return _dot(ah, bh) + (_dot(ah, bl) + _dot(al, bh))


def _mod_kernel(c_ref, w_ref, b_ref, o_ref):
    c = c_ref[...]
    s = c * jax.nn.sigmoid(c)
    o_ref[...] = _dot3(s, w_ref[...]) + b_ref[...]


def _modulation(cc, w_mod, b_mod):
    rows, d = cc.shape
    n = w_mod.shape[1]
    tn = min(n, 1536)
    return pl.pallas_call(
        _mod_kernel,
        grid=(n // tn,),
        in_specs=[pl.BlockSpec((rows, d), lambda j: (0, 0)),
                  pl.BlockSpec((d, tn), lambda j: (0, j)),
                  pl.BlockSpec((1, tn), lambda j: (0, j))],
        out_specs=pl.BlockSpec((rows, tn), lambda j: (0, j)),
        out_shape=jax.ShapeDtypeStruct((rows, n), F32),
        compiler_params=_cparams(("parallel",)),
        name="modulation",
    )(cc, w_mod, b_mod.reshape(1, n))


def _rope_table_kernel(cos_ref, sin_ref, *, qk_dim):
    s, w = cos_ref.shape
    half = qk_dim // 2
    nf = half // 2
    t = lax.broadcasted_iota(I32, (s, w), 0)
    lane = lax.broadcasted_iota(I32, (s, w), 1)
    d = lane & (qk_dim - 1)
    j = d & (nf - 1)
    row = t >> _log2(GRID_W)
    col = t & (GRID_W - 1)
    pos = jnp.where(d < half, row, col).astype(F32)
    inv = jnp.exp(j.astype(F32) * (-math.log(ROPE_BASE) / nf))
    ang = pos * inv
    first = (d & (half - 1)) < nf
    cos_ref[...] = jnp.cos(ang)
    sn = jnp.sin(ang)
    sin_ref[...] = jnp.where(first, -sn, sn)


def _rope_tables(s, qk_dim):
    return pl.pallas_call(
        functools.partial(_rope_table_kernel, qk_dim=qk_dim),
        out_shape=(jax.ShapeDtypeStruct((s, LANES), F32), jax.ShapeDtypeStruct((s, LANES), F32)),
        name="rope_tables",
    )()


def _group_rms(t, gain, qk_dim):
    w = t.shape[1]
    r = lax.broadcasted_iota(I32, (w, w), 0) >> _log2(qk_dim)
    c = lax.broadcasted_iota(I32, (w, w), 1) >> _log2(qk_dim)
    bd = jnp.where(r == c, 1.0 / qk_dim, 0.0).astype(BF16)
    hi, lo = _split_bf16(t * t)
    ms = _dot(hi, bd) + _dot(lo, bd)
    return t * lax.rsqrt(ms + EPS) * gain


def _rope(t, cos, sin_signed, qk_dim):
    w = t.shape[1]
    nf = qk_dim // 4
    lane = lax.broadcasted_iota(I32, t.shape, 1)
    first = (lane & (2 * nf - 1)) < nf
    partner = jnp.where(first, pltpu.roll(t, w - nf, axis=1), pltpu.roll(t, nf, axis=1))
    return t * cos + partner * sin_signed


def _proj_kernel(*refs, latent, qk_cols, v_cols, qk_dim):
    if latent:
        (x_ref, sh_ref, sc_ref, g_ref, w_ref, qg_ref, kg_ref, cos_ref, sin_ref,
         q_out, k_out, v_out, u_out) = refs
    else:
        x_ref, sh_ref, sc_ref, g_ref, w_ref, kg_ref, k_out, v_out = refs
    x = x_ref[0]
    ms = jnp.mean(x * x, axis=-1, keepdims=True)
    h = (x * lax.rsqrt(ms + EPS) * g_ref[...]) * (1.0 + sc_ref[0]) + sh_ref[0]
    proj = _dot(h.astype(BF16), w_ref[...])
    if latent:
        reps = qk_cols // LANES
        cos = jnp.concatenate([cos_ref[...]] * reps, axis=1)
        sin = jnp.concatenate([sin_ref[...]] * reps, axis=1)
        q = _rope(_group_rms(proj[:, :qk_cols], qg_ref[...], qk_dim), cos, sin, qk_dim)
        q_out[0] = (q * (qk_dim ** -0.5 * math.log2(math.e))).astype(BF16)
        k = _rope(_group_rms(proj[:, qk_cols:2 * qk_cols], kg_ref[...], qk_dim), cos, sin, qk_dim)
        k_out[0] = k.astype(BF16)
        v_out[0] = proj[:, 2 * qk_cols:2 * qk_cols + v_cols].astype(BF16)
        u_out[0] = proj[:, 2 * qk_cols + v_cols:].astype(BF16)
    else:
        k = _group_rms(proj[:, :qk_cols], kg_ref[...], qk_dim)
        k_out[0] = k.astype(BF16)
        v_out[0] = proj[:, qk_cols:qk_cols + v_cols].astype(BF16)


def _project_latent(x, sh, sc, g, w_bf, qg, kg, cos_t, sin_t, qk_cols, v_cols, qk_dim):
    b, s, d = x.shape
    n = w_bf.shape[1]
    hy_cols = n - 2 * qk_cols - v_cols
    tm = min(ROW_TILE, s)
    row = lambda bi, i: (bi, i, 0)
    per_b = lambda bi, i: (bi, 0, 0)
    const = lambda bi, i: (0, 0)
    return pl.pallas_call(
        functools.partial(_proj_kernel, latent=True, qk_cols=qk_cols, v_cols=v_cols, qk_dim=qk_dim),
        grid=(b, s // tm),
        in_specs=[pl.BlockSpec((1, tm, d), row),
                  pl.BlockSpec((1, 1, d), per_b), pl.BlockSpec((1, 1, d), per_b),
                  pl.BlockSpec((1, d), const), pl.BlockSpec((d, n), const),
                  pl.BlockSpec((1, qk_cols), const), pl.BlockSpec((1, qk_cols), const),
                  pl.BlockSpec((tm, LANES), lambda bi, i: (i, 0)),
                  pl.BlockSpec((tm, LANES), lambda bi, i: (i, 0))],
        out_specs=[pl.BlockSpec((1, tm, qk_cols), row), pl.BlockSpec((1, tm, qk_cols), row),
                   pl.BlockSpec((1, tm, v_cols), row), pl.BlockSpec((1, tm, hy_cols), row)],
        out_shape=[jax.ShapeDtypeStruct((b, s, qk_cols), BF16), jax.ShapeDtypeStruct((b, s, qk_cols), BF16),
                   jax.ShapeDtypeStruct((b, s, v_cols), BF16), jax.ShapeDtypeStruct((b, s, hy_cols), BF16)],
        compiler_params=_cparams(("parallel", "parallel")),
        name="project_latent",
    )(x, sh, sc, g, w_bf, qg, kg, cos_t, sin_t)


def _project_context(ctx, sh, sc, g, w_bf, kg, qk_cols, v_cols, qk_dim):
    b, lc, d = ctx.shape
    n = w_bf.shape[1]
    tm = min(ROW_TILE, lc)
    row = lambda bi, i: (bi, i, 0)
    shared = lambda bi, i: (0, 0, 0)
    const = lambda bi, i: (0, 0)
    return pl.pallas_call(
        functools.partial(_proj_kernel, latent=False, qk_cols=qk_cols, v_cols=v_cols, qk_dim=qk_dim),
        grid=(b, lc // tm),
        in_specs=[pl.BlockSpec((1, tm, d), row),
                  pl.BlockSpec((1, 1, d), shared), pl.BlockSpec((1, 1, d), shared),
                  pl.BlockSpec((1, d), const), pl.BlockSpec((d, n), const),
                  pl.BlockSpec((1, qk_cols), const)],
        out_specs=[pl.BlockSpec((1, tm, qk_cols), row), pl.BlockSpec((1, tm, v_cols), row)],
        out_shape=[jax.ShapeDtypeStruct((b, lc, qk_cols), BF16), jax.ShapeDtypeStruct((b, lc, v_cols), BF16)],
        compiler_params=_cparams(("parallel", "parallel")),
        name="project_context",
    )(ctx, sh, sc, g, w_bf, kg)


def _key_chunks(kk, n):
    tiles = kk // LANES
    n = min(n, tiles)
    return [(LANES * (i * tiles // n), LANES * ((i + 1) * tiles // n)) for i in range(n)]


def _attn_kernel(q_ref, kc_ref, k_ref, vc_ref, v_ref, lq1, lk1, lq2, lk2, sg_ref, o_ref, kt_ref, v1_ref, *bufs,
                 lam_init, qk_dim, n_tiles):
    lam = (jnp.exp(jnp.sum(lq1[...] * lk1[...], axis=-1, keepdims=True))
           - jnp.exp(jnp.sum(lq2[...] * lk2[...], axis=-1, keepdims=True)) + lam_init)
    j = pl.program_id(2)
    even, odd = bufs[:4], bufs[4:]
    tq = q_ref.shape[1]
    chunks = _key_chunks(kt_ref.shape[1], ATT_KEY_CHUNKS)

    @pl.when(j == 0)
    def _():
        lc = kc_ref.shape[1]
        kt_ref[:, :lc] = kc_ref[0].T
        kt_ref[:, lc:] = k_ref[0].T
        v1_ref[:lc, :LANES] = vc_ref[0]
        v1_ref[lc:, :LANES] = v_ref[0]
        v1_ref[:, LANES:] = jnp.ones((v1_ref.shape[0], LANES), BF16)

    def step(cur, prev, score=True, attend=True):
        s1_w, s2_w, m1_w, m2_w = cur
        s1_r, s2_r, m1_r, m2_r = prev
        if score:
            q = q_ref[0]
            lane = lax.broadcasted_iota(I32, q.shape, 1)
            q1 = jnp.where(lane < qk_dim, q, jnp.zeros_like(q))
            q2 = jnp.where(lane >= qk_dim, q, jnp.zeros_like(q))
            m1 = m2 = jnp.full((tq, 1), -jnp.inf, F32)
        if attend:
            m1p, m2p = m1_r[:, :1], m2_r[:, :1]
            o1 = o2 = jnp.zeros((tq, 2 * LANES), F32)
        for c0, c1 in chunks:
            if score:
                s1 = _dot(q1, kt_ref[:, c0:c1])
                s2 = _dot(q2, kt_ref[:, c0:c1])
                s1_w[:, c0:c1] = s1
                s2_w[:, c0:c1] = s2
                m1 = jnp.maximum(m1, jnp.max(s1, axis=-1, keepdims=True))
                m2 = jnp.maximum(m2, jnp.max(s2, axis=-1, keepdims=True))
            if attend:
                e1 = jnp.exp2(s1_r[:, c0:c1] - m1p)
                e2 = jnp.exp2(s2_r[:, c0:c1] - m2p)
                o1 = o1 + _dot(e1.astype(BF16), v1_ref[c0:c1, :])
                o2 = o2 + _dot(e2.astype(BF16), v1_ref[c0:c1, :])
        if score:
            m1_w[...] = jnp.broadcast_to(m1, m1_w.shape)
            m2_w[...] = jnp.broadcast_to(m2, m2_w.shape)
        if attend:
            o = o1[:, :LANES] / o1[:, LANES:] - o2[:, :LANES] * (lam / o2[:, LANES:])
            ms = jnp.mean(o * o, axis=-1, keepdims=True)
            o_ref[0] = ((o * lax.rsqrt(ms + EPS) * sg_ref[...]) * (1.0 - lam_init)).astype(BF16)

    by_parity = (lambda **kw: step(even, odd, **kw)), (lambda **kw: step(odd, even, **kw))

    @pl.when(j == 0)
    def _():
        by_parity[0](attend=False)

    for parity in range(2):
        @pl.when((j > 0) & (j < n_tiles) & ((j & 1) == parity))
        def _():
            by_parity[parity]()

    @pl.when(j == n_tiles)
    def _():
        by_parity[n_tiles % 2](score=False)


def _diff_attention(q, k_c, k, v_c, v, lq1, lk1, lq2, lk2, subln_g, lam_init, qk_dim):
    b, s, w = q.shape
    lc = k_c.shape[1]
    assert lc % LANES == 0
    kk = lc + s
    tq = min(ATT_Q_TILE, s)
    nq = s // tq
    kv = lambda bi, h, i: (bi, 0, h)
    const = lambda bi, h, i: (0, 0)
    vec = pl.BlockSpec((1, qk_dim), const)
    wide = pltpu.VMEM((tq, kk), F32)
    stat = pltpu.VMEM((tq, LANES), F32)
    per_parity = [wide, wide, stat, stat]
    return pl.pallas_call(
        functools.partial(_attn_kernel, lam_init=lam_init, qk_dim=qk_dim, n_tiles=nq),
        grid=(b, N_HEADS, nq + 1),
        in_specs=[pl.BlockSpec((1, tq, LANES), lambda bi, h, i: (bi, jnp.minimum(i, nq - 1), h)),
                  pl.BlockSpec((1, lc, LANES), kv), pl.BlockSpec((1, s, LANES), kv),
                  pl.BlockSpec((1, lc, LANES), kv), pl.BlockSpec((1, s, LANES), kv), vec, vec, vec, vec,
                  pl.BlockSpec((1, LANES), const)],
        out_specs=pl.BlockSpec((1, tq, LANES), lambda bi, h, i: (bi, jnp.maximum(i - 1, 0), h)),
        out_shape=jax.ShapeDtypeStruct((b, s, w), BF16),
        scratch_shapes=[pltpu.VMEM((LANES, kk), BF16), pltpu.VMEM((kk, 2 * LANES), BF16)] + per_parity + per_parity,
        compiler_params=_cparams(("parallel", "parallel", "arbitrary")),
        name="diff_attention",
    )(q, k_c, k, v_c, v, lq1, lk1, lq2, lk2, subln_g)


def _filter_kernel(w1_ref, b1_ref, f1_ref, w2_ref, b2_ref, f2_ref, w3_ref, o_ref, *, seq, hw):
    tl, n = o_ref.shape
    base = pl.program_id(0) * tl
    quarter = seq // RADIX

    def position(shape):
        p = lax.broadcasted_iota(I32, shape, 0) + base
        return (((p & (quarter - 1)) << _log2(RADIX)) | (p >> _log2(quarter))).astype(F32)

    pos = position((tl, LANES))
    lane = lax.broadcasted_iota(I32, (tl, LANES), 1)
    tn = pos / seq
    band_idx = jnp.where(lane <= N_BANDS, lane - 1, lane - 1 - N_BANDS).astype(F32)
    band = 1e-4 + band_idx * ((N_BANDS - 1 - 1e-4) / (N_BANDS - 1))
    ang = (2.0 * math.pi / seq) * pos * band
    feats = jnp.where(lane == 0, tn,
                      jnp.where(lane <= N_BANDS, jnp.sin(ang),
                                jnp.where(lane < FEAT_DIM, jnp.cos(ang), 0.0)))
    h = jnp.sin(f1_ref[...] * (_dot3(feats, w1_ref[...]) + b1_ref[...]))
    h = jnp.sin(f2_ref[...] * (_dot3(h, w2_ref[...]) + b2_ref[...]))
    h = _dot3(h, w3_ref[...])
    ch = (lax.broadcasted_iota(I32, (tl, n), 1) & ((1 << _log2(hw)) - 1)).astype(F32)
    lo = abs(math.log(DECAY_TARGET) / SLOW_DECAY_PCT)
    hi = abs(math.log(DECAY_TARGET) / FAST_DECAY_PCT)
    delta = lo + ch * ((hi - lo) / (hw - 1))
    o_ref[...] = (h * jnp.exp(-(position((tl, n)) / seq) * delta)).astype(BF16)


def _hyena_filters(seq, hw, w1, b1, f1, w2, b2, f2, w3):
    fh = w2.shape[0]
    n = w3.shape[1]
    w1p = jnp.zeros((LANES, fh), F32).at[:FEAT_DIM].set(w1)
    tl = min(ROW_TILE, seq)
    const = lambda i: (0, 0)
    return pl.pallas_call(
        functools.partial(_filter_kernel, seq=seq, hw=hw),
        grid=(seq // tl,),
        in_specs=[pl.BlockSpec((LANES, fh), const), pl.BlockSpec((1, fh), const), pl.BlockSpec((1, fh), const),
                  pl.BlockSpec((fh, fh), const), pl.BlockSpec((1, fh), const), pl.BlockSpec((1, fh), const),
                  pl.BlockSpec((fh, n), const)],
        out_specs=pl.BlockSpec((tl, n), lambda i: (i, 0)),
        out_shape=jax.ShapeDtypeStruct((seq, n), BF16),
        compiler_params=_cparams(("parallel",)),
        name="hyena_filters",
    )(w1p, b1.reshape(1, fh), f1.reshape(1, fh), w2, b2.reshape(1, fh), f2.reshape(1, fh), w3)


def _dft_kernel(mf_ref, mi_ref, tfc, tfs, tic, tis, *, seq):
    rows, q = tic.shape
    mask = (1 << _log2(4 * seq)) - 1
    unit = math.pi / (2 * seq)
    i_row = lax.broadcasted_iota(I32, (rows, q), 0)
    col = lax.broadcasted_iota(I32, (rows, q), 1)

    @pl.when(pl.program_id(0) == 0)
    def _():
        for r in range(RADIX):
            af = ((2 * i_row * (RADIX * col + r)) & mask).astype(F32) * unit
            tfc[r] = jnp.cos(af)
            tfs[r] = jnp.sin(af)
        ai = (((2 * col + 1) * (RADIX * i_row)) & mask).astype(F32) * unit
        tic[...] = jnp.cos(ai)
        tis[...] = jnp.sin(ai)

    r0 = pl.program_id(0) * rows
    c1 = lax.broadcasted_iota(I32, (1, q), 1)
    for r in range(RADIX):
        bf = (((2 * r0 + 1) * (RADIX * c1 + r)) & mask).astype(F32) * unit
        bi = (((2 * c1 + 1) * (RADIX * r0 + r)) & mask).astype(F32) * unit
        cbf, sbf = jnp.cos(bf), jnp.sin(bf)
        cbi, sbi = jnp.cos(bi), jnp.sin(bi)
        mf_ref[r, 0] = (cbf * tfc[r] - sbf * tfs[r]).astype(BF16)
        mf_ref[r, 1] = (sbf * tfc[r] + cbf * tfs[r]).astype(BF16)
        mi_ref[r, :, :q] = (cbi * tic[...] - sbi * tis[...]).astype(BF16)
        mi_ref[r, :, q:] = (sbi * tic[...] + cbi * tis[...]).astype(BF16)


def _dft_matrices(seq):
    q = seq // RADIX
    rows = min(DFT_TILE, q)
    return pl.pallas_call(
        functools.partial(_dft_kernel, seq=seq),
        grid=(q // rows,),
        out_specs=[pl.BlockSpec((RADIX, 2, rows, q), lambda i: (0, 0, i, 0)),
                   pl.BlockSpec((RADIX, rows, 2 * q), lambda i: (0, i, 0))],
        out_shape=[jax.ShapeDtypeStruct((RADIX, 2, q, q), BF16), jax.ShapeDtypeStruct((RADIX, q, 2 * q), BF16)],
        scratch_shapes=[pltpu.VMEM((RADIX, rows, q), F32), pltpu.VMEM((RADIX, rows, q), F32),
                        pltpu.VMEM((rows, q), F32), pltpu.VMEM((rows, q), F32)],
        compiler_params=_cparams(("arbitrary",)),
        name="dft_matrices",
    )()


def _class_transform(mf_ref, x_of_class):
    tc, ts = [], []
    for r in range(RADIX):
        x = x_of_class(r)
        tc.append(_dot(mf_ref[r, 0], x))
        ts.append(_dot(mf_ref[r, 1], x))
    return [(tc[0] + tc[1] + tc[2] + tc[3], ts[0] + ts[1] + ts[2] + ts[3]),
            (tc[0] - tc[1] + tc[2] - tc[3], ts[1] - ts[0] + ts[3] - ts[2]),
            (tc[0] - ts[1] - tc[2] + ts[3], ts[0] + tc[1] - ts[2] - tc[3]),
            (tc[0] + ts[1] - tc[2] - ts[3], tc[1] - ts[0] + ts[2] - tc[3])]


def _spectrum_kernel(mf_ref, h_ref, g_ref, *, seq, hw):
    q = seq // RADIX
    groups = _class_transform(mf_ref, lambda r: h_ref[r * q:(r + 1) * q, :])
    scale = 1.0 / seq
    for x, (hc, hs) in enumerate(groups):
        g_ref[0, x, 0] = ((hc[:, :hw] + hc[:, hw:]) * scale).astype(g_ref.dtype)
        g_ref[0, x, 1] = ((hs[:, :hw] - hs[:, hw:]) * scale).astype(g_ref.dtype)


def _filter_spectra(mf, hfilt, hw):
    q = mf.shape[2]
    seq = q * RADIX
    rows = min(DFT_TILE, q)
    return pl.pallas_call(
        functools.partial(_spectrum_kernel, seq=seq, hw=hw),
        grid=(HYENA_ORDER, q // rows),
        in_specs=[pl.BlockSpec((RADIX, 2, rows, q), lambda n, i: (0, 0, i, 0)),
                  pl.BlockSpec((seq, 2 * hw), lambda n, i: (0, n))],
        out_specs=pl.BlockSpec((1, RADIX, 2, rows, hw), lambda n, i: (n, 0, 0, i, 0)),
        out_shape=jax.ShapeDtypeStruct((HYENA_ORDER, RADIX, 2, q, hw), BF16),
        compiler_params=_cparams(("parallel", "parallel")),
        name="filter_spectra",
    )(mf, hfilt)


def _short_conv_kernel(u_ref, w_ref, b_ref, o_ref, y_ref):
    s, tc = u_ref.shape[1], u_ref.shape[2]
    q = s // RADIX
    t = lax.broadcasted_iota(I32, (s, LANES), 0)
    for c in range(tc // LANES):
        cs = slice(c * LANES, (c + 1) * LANES)
        u = u_ref[0, :, cs].astype(F32)
        prev = jnp.where(t == 0, 0.0, pltpu.roll(u, 1, axis=0))
        nxt = jnp.where(t == s - 1, 0.0, pltpu.roll(u, s - 1, axis=0))
        y_ref[...] = b_ref[:, cs] + prev * w_ref[0:1, cs] + u * w_ref[1:2, cs] + nxt * w_ref[2:3, cs]
        for r in range(RADIX):
            o_ref[0, r * q:(r + 1) * q, cs] = y_ref[pl.ds(r, q, stride=RADIX), :].astype(o_ref.dtype)


def _short_conv(u, w, bias):
    b, s, c = u.shape
    tc = min(512, c)
    return pl.pallas_call(
        _short_conv_kernel,
        grid=(b, c // tc),
        in_specs=[pl.BlockSpec((1, s, tc), lambda bi, j: (bi, 0, j)),
                  pl.BlockSpec((SHORT_CONV, tc), lambda bi, j: (0, j)),
                  pl.BlockSpec((1, tc), lambda bi, j: (0, j))],
        out_specs=pl.BlockSpec((1, s, tc), lambda bi, j: (bi, 0, j)),
        out_shape=jax.ShapeDtypeStruct((b, s, c), BF16),
        scratch_shapes=[pltpu.VMEM((s, LANES), F32)],
        compiler_params=_cparams(("parallel", "parallel")),
        name="short_conv",
    )(u, w, bias.reshape(1, c))


def _fwd_dft_kernel(mf_ref, z_ref, g_ref, y_ref, zb):
    q = zb.shape[0] // RADIX

    @pl.when(pl.program_id(1) == 0)
    def _():
        zb[...] = z_ref[0].astype(BF16)

    groups = _class_transform(mf_ref, lambda r: zb[r * q:(r + 1) * q, :])
    yc, ys = [], []
    for x, (uc, us) in enumerate(groups):
        gc, gs = g_ref[0, x, 0].astype(F32), g_ref[0, x, 1].astype(F32)
        yc.append(uc * gc - us * gs)
        ys.append(uc * gs + us * gc)
    a, b, c, d = range(RADIX)
    z = [(yc[a] + yc[c] + yc[b] + yc[d], ys[a] + ys[c] - ys[b] - ys[d]),
         (yc[a] + ys[c] - yc[b] + ys[d], ys[a] - yc[c] + ys[b] + yc[d]),
         (yc[a] - yc[c] + yc[b] - yc[d], ys[a] - ys[c] - ys[b] + ys[d]),
         (yc[a] - ys[c] - yc[b] - ys[d], ys[a] + yc[c] + ys[b] - yc[d])]
    for r, (zc, zs) in enumerate(z):
        y_ref[0, r, 0] = zc.astype(BF16)
        y_ref[0, r, 1] = zs.astype(BF16)


def _fwd_dft(mf, z, z_col, g, order, hw):
    b, seq = z.shape[0], z.shape[1]
    q = seq // RADIX
    rows = min(DFT_TILE, q)
    return pl.pallas_call(
        _fwd_dft_kernel,
        grid=(b, q // rows),
        in_specs=[pl.BlockSpec((RADIX, 2, rows, q), lambda bi, i: (0, 0, i, 0)),
                  pl.BlockSpec((1, seq, hw), lambda bi, i: (bi, 0, z_col)),
                  pl.BlockSpec((1, RADIX, 2, rows, hw), lambda bi, i: (order, 0, 0, i, 0))],
        out_specs=pl.BlockSpec((1, RADIX, 2, rows, hw), lambda bi, i: (bi, 0, 0, i, 0)),
        out_shape=jax.ShapeDtypeStruct((b, RADIX, 2, q, hw), BF16),
        scratch_shapes=[pltpu.VMEM((seq, hw), BF16)],
        compiler_params=_cparams(("parallel", "arbitrary")),
        name="hyena_fwd_dft",
    )(mf, z, g)


def _inv_dft_kernel(mi_ref, y_ref, z_ref, gate_ref, skip_ref, *rest, final):
    if final:
        og_ref, o_ref = rest
    else:
        (o_ref,) = rest
    rows = mi_ref.shape[1]
    for r in range(RADIX):
        conv = _dot(mi_ref[r], y_ref[0, r])
        z = gate_ref[0, r].astype(F32) * (conv + z_ref[0, r].astype(F32) * skip_ref[0])
        if final:
            ms = jnp.mean(z * z, axis=-1, keepdims=True)
            zn = z * lax.rsqrt(ms + EPS) * og_ref[...]
            for c in range(zn.shape[1] // LANES):
                o_ref[0, c, pl.ds(r, rows, stride=RADIX), :] = zn[:, c * LANES:(c + 1) * LANES]
        else:
            o_ref[0, r] = z.astype(o_ref.dtype)


def _inv_dft(mi, y, z, z_col, gates, gate_col, skip, order, out_g, hw):
    b, seq = z.shape[0], z.shape[1]
    q = seq // RADIX
    rows = min(DFT_TILE, q)
    final = out_g is not None
    by_class = lambda a: a.reshape(b, RADIX, q, a.shape[2])
    in_specs = [pl.BlockSpec((RADIX, rows, 2 * q), lambda bi, i: (0, i, 0)),
                pl.BlockSpec((1, RADIX, 2 * q, hw), lambda bi, i: (bi, 0, 0, 0)),
                pl.BlockSpec((1, RADIX, rows, hw), lambda bi, i: (bi, 0, i, z_col)),
                pl.BlockSpec((1, RADIX, rows, hw), lambda bi, i: (bi, 0, i, gate_col)),
                pl.BlockSpec((1, 1, hw), lambda bi, i: (order, 0, 0))]
    args = [mi, y.reshape(b, RADIX, 2 * q, hw), by_class(z), by_class(gates), skip.reshape(HYENA_ORDER, 1, hw)]
    if final:
        in_specs.append(pl.BlockSpec((1, hw), lambda bi, i: (0, 0)))
        args.append(out_g.reshape(1, hw))
        out_spec = pl.BlockSpec((1, hw // LANES, RADIX * rows, LANES), lambda bi, i: (bi, 0, i, 0))
        out_shape = jax.ShapeDtypeStruct((b, hw // LANES, seq, LANES), F32)
    else:
        out_spec = pl.BlockSpec((1, RADIX, rows, hw), lambda bi, i: (bi, 0, i, 0))
        out_shape = jax.ShapeDtypeStruct((b, RADIX, q, hw), BF16)
    out = pl.pallas_call(
        functools.partial(_inv_dft_kernel, final=final),
        grid=(b, q // rows),
        in_specs=in_specs,
        out_specs=out_spec,
        out_shape=out_shape,
        compiler_params=_cparams(("parallel", "parallel")),
        name="hyena_inv_dft",
    )(*args)
    return out if final else out.reshape(b, seq, hw)


def _out_kernel(a_ref, hy_ref, x_ref, wo_ref, g1_ref, sh_ref, sc_ref, n2_ref, rw_ref, rb_ref,
                xn_ref, h2_ref, lg_ref):
    aw = a_ref.shape[2]
    hy = jnp.concatenate([hy_ref[0, c] for c in range(hy_ref.shape[1])], axis=1).astype(BF16)
    mix = _dot(a_ref[0], wo_ref[:aw, :]) + _dot(hy, wo_ref[aw:, :])
    xn = x_ref[0] + g1_ref[0] * mix
    xn_ref[0] = xn
    ms = jnp.mean(xn * xn, axis=-1, keepdims=True)
    h2 = (xn * lax.rsqrt(ms + EPS) * n2_ref[...]) * (1.0 + sc_ref[0]) + sh_ref[0]
    _store_row_tiles(h2_ref, h2)
    hh, hl = _split_bf16(h2)
    wh, wl = _split_bf16(rw_ref[...])
    lg_ref[...] = _dot_nt(wh, hh) + (_dot_nt(wh, hl) + _dot_nt(wl, hh)) + rb_ref[...]


def _out_project(attn, hyn, x, wo_bf, g1, sh2, sc2, n2g, rw_t, rb):
    b, s, d = x.shape
    aw, hw = attn.shape[2], hyn.shape[1] * hyn.shape[3]
    ne = rw_t.shape[0]
    tm = min(ROW_TILE, s)
    nt = s // tm
    row = lambda bi, i: (bi, i, 0)
    per_b = lambda bi, i: (bi, 0, 0)
    const = lambda bi, i: (0, 0)
    return pl.pallas_call(
        _out_kernel,
        grid=(b, nt),
        in_specs=[pl.BlockSpec((1, tm, aw), row),
                  pl.BlockSpec((1, hw // LANES, tm, LANES), lambda bi, i: (bi, 0, i, 0)),
                  pl.BlockSpec((1, tm, d), row),
                  pl.BlockSpec((aw + hw, d), const),
                  pl.BlockSpec((1, 1, d), per_b), pl.BlockSpec((1, 1, d), per_b), pl.BlockSpec((1, 1, d), per_b),
                  pl.BlockSpec((1, d), const), pl.BlockSpec((ne, d), const), pl.BlockSpec((ne, 1), const)],
        out_specs=[pl.BlockSpec((1, tm, d), row),
                   pl.BlockSpec((tm * (d // LANES), LANES), lambda bi, i: (bi * nt + i, 0)),
                   pl.BlockSpec((ne, tm), lambda bi, i: (0, bi * nt + i))],
        out_shape=[jax.ShapeDtypeStruct((b, s, d), F32), jax.ShapeDtypeStruct((b * s * (d // LANES), LANES), F32),
                   jax.ShapeDtypeStruct((ne, b * s), F32)],
        compiler_params=_cparams(("parallel", "parallel")),
        name="out_project",
    )(attn, hyn, x, wo_bf, g1, sh2, sc2, n2g, rw_t, rb)


def _route_kernel(lg_ref, gate_ref, lpos_ref, tstart_ref, tcnt_ref, cnt_ref, carry, *, rt):
    ne, tl = lg_ref.shape

    @pl.when(pl.program_id(0) == 0)
    def _():
        carry[...] = jnp.zeros_like(carry)

    l = lg_ref[...]
    rows = lax.broadcasted_iota(I32, (ne, tl), 0).astype(F32)
    vals, sels = [], []
    for k in range(TOP_K):
        m = jnp.max(l, axis=0, keepdims=True)
        ik = jnp.min(jnp.where(l == m, rows, float(ne)), axis=0, keepdims=True)
        sel = rows == ik
        vals.append(m)
        sels.append(sel)
        l = jnp.where(sel, -jnp.inf, l)
    exps = [jnp.exp(v - vals[0]) for v in vals]
    denom = exps[0] + exps[1] + exps[2] + exps[3]
    for k in range(TOP_K):
        gate_ref[k:k + 1, :] = exps[k] / denom
    oh = jnp.zeros((ne, tl), F32)
    for sel in sels:
        oh = oh + jnp.where(sel, 1.0, 0.0)
    r = lax.broadcasted_iota(I32, (tl, tl), 0)
    c = lax.broadcasted_iota(I32, (tl, tl), 1)
    tri = jnp.where(r <= c, 1.0, 0.0).astype(BF16)
    cum = _dot(oh.astype(BF16), tri)
    n_col = jnp.sum(oh, axis=1, keepdims=True)
    er = lax.broadcasted_iota(I32, (ne, LANES), 0)
    ec = lax.broadcasted_iota(I32, (ne, LANES), 1)
    to_lane = lambda col: jnp.sum(jnp.where(er == ec, jnp.broadcast_to(col, (ne, LANES)), 0.0),
                                  axis=0, keepdims=True)
    n_lane = to_lane(n_col)
    off_col = jnp.sum(jnp.where(ec < er, jnp.broadcast_to(n_lane, (ne, LANES)), 0.0), axis=1, keepdims=True)
    slab_pos = cum - oh + off_col
    for k in range(TOP_K):
        pos = jnp.sum(jnp.where(sels[k], slab_pos, 0.0), axis=0, keepdims=True)
        lpos_ref[k:k + 1, :] = (pos * rt).astype(I32)
    tstart_ref[0] = to_lane(carry[:, 0:1]).astype(I32)
    tcnt_ref[0] = n_lane.astype(I32)
    carry[...] = carry[...] + n_col
    cnt_ref[...] = carry[...]


def _route(logits_t, tl, rt):
    ne, t = logits_t.shape
    nt = t // tl
    blk = lambda i: (0, i)
    per_tile = pl.BlockSpec((1, 1, LANES), lambda i: (i, 0, 0))
    return pl.pallas_call(
        functools.partial(_route_kernel, rt=rt),
        grid=(nt,),
        in_specs=[pl.BlockSpec((ne, tl), blk)],
        out_specs=[pl.BlockSpec((TOP_K, tl), blk), pl.BlockSpec((TOP_K, tl), blk), per_tile, per_tile,
                   pl.BlockSpec((ne, LANES), lambda i: (0, 0))],
        out_shape=[jax.ShapeDtypeStruct((TOP_K, t), F32), jax.ShapeDtypeStruct((TOP_K, t), I32),
                   jax.ShapeDtypeStruct((nt, 1, LANES), I32), jax.ShapeDtypeStruct((nt, 1, LANES), I32),
                   jax.ShapeDtypeStruct((ne, LANES), F32)],
        scratch_shapes=[pltpu.VMEM((ne, LANES), F32)],
        compiler_params=_cparams(("arbitrary",)),
        name="moe_route",
    )(logits_t)


def _slots_kernel(cnt_ref, tstart_ref, run_ref, blk_ref, meta_ref, *, rows_per_block):
    ne = cnt_ref.shape[0]
    shift = _log2(rows_per_block)
    cnt = cnt_ref[...].astype(I32)
    padded = ((cnt + (rows_per_block - 1)) >> shift) << shift
    r = lax.broadcasted_iota(I32, (ne, LANES), 0)
    c = lax.broadcasted_iota(I32, (ne, LANES), 1)
    padded_lane = jnp.sum(jnp.where(r == c, padded, 0), axis=0, keepdims=True)
    cnt_lane = jnp.sum(jnp.where(r == c, cnt, 0), axis=0, keepdims=True)
    pend_lane = jnp.sum(jnp.where(r <= c, padded, 0), axis=0, keepdims=True)
    pend_col = jnp.sum(jnp.where(c <= r, jnp.broadcast_to(padded_lane, (ne, LANES)), 0),
                       axis=1, keepdims=True)
    run_ref[...] = tstart_ref[...] + (pend_lane - padded_lane)
    nbp = blk_ref.shape[1]
    j0 = lax.broadcasted_iota(I32, (ne, nbp), 1) * rows_per_block
    be = jnp.sum(jnp.where(jnp.broadcast_to(pend_col, (ne, nbp)) <= j0, 1, 0), axis=0, keepdims=True)
    blk_ref[...] = jnp.minimum(be, ne - 1)
    total = jnp.max(pend_col, axis=0, keepdims=True)
    meta_ref[0:1, :] = pend_lane - padded_lane + cnt_lane
    meta_ref[1:2, :] = padded_lane - cnt_lane
    meta_ref[2:3, :] = jnp.broadcast_to(total >> shift, (1, LANES))
    meta_ref[3:8, :] = jnp.zeros((5, LANES), I32)


def _slots(cnt, tstart, n_blocks, rows_per_block):
    nbp = -(-n_blocks // LANES) * LANES
    return pl.pallas_call(
        functools.partial(_slots_kernel, rows_per_block=rows_per_block),
        out_shape=[jax.ShapeDtypeStruct(tstart.shape, I32), jax.ShapeDtypeStruct((1, nbp), I32),
                   jax.ShapeDtypeStruct((8, LANES), I32)],
        compiler_params=pltpu.CompilerParams(vmem_limit_bytes=V7X_VMEM_LIMIT),
        name="moe_slots",
    )(cnt, tstart)


def _pad_chunks(rows_per_block):
    sizes, s = [], rows_per_block // 2
    while s >= 1:
        sizes.append(s)
        s //= 2
    return sizes


def _rows(start, size, rt):
    return pl.ds(pl.multiple_of(start * rt, rt), size * rt)


def _for_each_run_chunk(run_ref, cnt_ref, tile, ne, max_rows, act):
    sizes = _pad_chunks(2 * max_rows)

    def each(e, off):
        left = cnt_ref[tile * ne + e]
        pos, start = off, run_ref[tile * ne + e]
        for size in sizes:
            hit = (left & size) != 0

            @pl.when(hit)
            def _():
                act(pos, start, size)

            inc = jnp.where(hit, size, 0)
            pos, start = pos + inc, start + inc
        return off + left

    lax.fori_loop(0, ne, each, 0)


def _dispatch_kernel(run_ref, cnt_ref, padlo_ref, npad_ref, nused_ref, lpos_ref, h_ref, buf_ref,
                     slab, zeros, sems, zsem, *, rows_per_block, rt, n_tiles):
    tl = lpos_ref.shape[1]
    ne = padlo_ref.shape[0]
    sizes = _pad_chunks(rows_per_block)
    half = rows_per_block // 2
    i = pl.program_id(0)
    slot = i & 1

    def pad_copy(start, size):
        return pltpu.make_async_copy(zeros.at[_rows(0, size, rt)], buf_ref.at[_rows(start, size, rt)], zsem)

    def run_copies(tile, sl, wait):
        def act(pos, start, size):
            cp = pltpu.make_async_copy(slab.at[sl, _rows(pos, size, rt)], buf_ref.at[_rows(start, size, rt)],
                                       sems.at[sl])
            if wait:
                cp.wait()
            else:
                cp.start()

        _for_each_run_chunk(run_ref, cnt_ref, tile, ne, tl, act)

    @pl.when(i == 0)
    def _():
        zeros[...] = jnp.zeros_like(zeros)

        first, last = 2 * nused_ref[0], buf_ref.shape[0] // (half * rt)
        lax.fori_loop(first, last, lambda j, c: (pad_copy(j * half, half).start(), c)[1], 0)
        lax.fori_loop(first, last, lambda j, c: (pad_copy(j * half, half).wait(), c)[1], 0)

        def each(e, wait):
            start = padlo_ref[e]
            left = npad_ref[e]
            for size in sizes:
                hit = (left & size) != 0

                @pl.when(hit)
                def _():
                    cp = pad_copy(start, size)
                    if wait:
                        cp.wait()
                    else:
                        cp.start()

                start = start + jnp.where(hit, size, 0)

        lax.fori_loop(0, ne, lambda e, c: (each(e, False), c)[1], 0)
        lax.fori_loop(0, ne, lambda e, c: (each(e, True), c)[1], 0)

    def step(sl):
        @pl.when(i >= 2)
        def _():
            run_copies(i - 2, sl, True)

        def fill(t, c):
            row = h_ref[_rows(t, 1, rt), :]
            for k in range(TOP_K):
                slab[sl, pl.ds(pl.multiple_of(lpos_ref[k, t], rt), rt), :] = row
            return c

        lax.fori_loop(0, tl, fill, 0, unroll=8)
        run_copies(i, sl, False)

        @pl.when(i == n_tiles - 1)
        def _():
            if n_tiles >= 2:
                run_copies(i - 1, 1 - sl, True)
            run_copies(i, sl, True)

    for sl in range(2):
        pl.when(slot == sl)(functools.partial(step, sl))


def _dispatch(h2r, lpos, run_start, run_cnt, pad_lo, n_pad, n_used, n_rows, rows_per_block, rt, tl):
    n_tiles = lpos.shape[1] // tl
    return pl.pallas_call(
        functools.partial(_dispatch_kernel, rows_per_block=rows_per_block, rt=rt, n_tiles=n_tiles),
        grid_spec=pltpu.PrefetchScalarGridSpec(
            num_scalar_prefetch=5,
            grid=(n_tiles,),
            in_specs=[pl.BlockSpec((TOP_K, tl), lambda i, *_: (0, i), memory_space=pltpu.SMEM),
                      pl.BlockSpec((tl * rt, LANES), lambda i, *_: (i, 0))],
            out_specs=pl.BlockSpec(memory_space=pl.ANY),
            scratch_shapes=[pltpu.VMEM((2, TOP_K * tl * rt, LANES), F32),
                            pltpu.VMEM((rows_per_block // 2 * rt, LANES), F32),
                            pltpu.SemaphoreType.DMA((2,)), pltpu.SemaphoreType.DMA(())]),
        out_shape=jax.ShapeDtypeStruct((n_rows * rt, LANES), F32),
        compiler_params=_cparams(("arbitrary",)),
        name="moe_dispatch",
    )(run_start, run_cnt, pad_lo, n_pad, n_used, lpos, h2r)


def _expert_kernel(be_ref, nu_ref, x_ref, w1_ref, b1_ref, w2_ref, b2_ref, o_ref, w1b, w2b):
    j = pl.program_id(0)
    active = j < nu_ref[0]

    @pl.when(active & ((j == 0) | (be_ref[j] != be_ref[jnp.maximum(j - 1, 0)])))
    def _():
        w1b[...] = w1_ref[0].astype(BF16)
        w2b[...] = w2_ref[0].astype(BF16)

    @pl.when(active)
    def _():
        de = w2b.shape[0]
        gl = _dot(_load_row_tiles(x_ref, w1b.shape[0] // LANES).astype(BF16), w1b[...]) + b1_ref[0]
        g = jnp.minimum(gl[:, :de], SWIGLU_LIMIT)
        lin = jnp.clip(gl[:, de:], -SWIGLU_LIMIT, SWIGLU_LIMIT)
        glu = g * jax.nn.sigmoid(SWIGLU_ALPHA * g)
        _store_row_tiles(o_ref, _dot(((lin + 1.0) * glu).astype(BF16), w2b[...]) + b2_ref[0])

    @pl.when(pl.program_id(0) >= nu_ref[0])
    def _():
        o_ref[...] = jnp.zeros_like(o_ref)


def _experts(buf, block_e, n_used, w1, b1, w2, b2, rows_per_block):
    ne, d, d2 = w1.shape
    de = w2.shape[1]
    blk_shape = (rows_per_block * (d // LANES), LANES)
    nb = buf.shape[0] // blk_shape[0]
    rowblk = lambda j, be, nu: (jnp.minimum(j, nu[0] - 1), 0)
    by_e = lambda j, be, nu: (be[j], 0, 0)
    return pl.pallas_call(
        _expert_kernel,
        grid_spec=pltpu.PrefetchScalarGridSpec(
            num_scalar_prefetch=2,
            grid=(nb,),
            in_specs=[pl.BlockSpec(blk_shape, rowblk),
                      pl.BlockSpec((1, d, d2), by_e), pl.BlockSpec((1, 1, d2), by_e),
                      pl.BlockSpec((1, de, d), by_e), pl.BlockSpec((1, 1, d), by_e)],
            out_specs=pl.BlockSpec(blk_shape, lambda j, be, nu: (j, 0)),
            scratch_shapes=[pltpu.VMEM((d, d2), BF16), pltpu.VMEM((de, d), BF16)]),
        out_shape=jax.ShapeDtypeStruct(buf.shape, F32),
        compiler_params=_cparams(("arbitrary",)),
        name="moe_experts",
    )(block_e, n_used, buf, w1, b1.reshape(ne, 1, d2), w2, b2.reshape(ne, 1, d))


def _combine_kernel(run_ref, cnt_ref, lpos_ref, gate_ref, xn_ref, g2_ref, ob_ref, o_ref, slab, acc, sems,
                    *, rt, ne, n_tiles):
    tl = lpos_ref.shape[1]
    i = pl.program_id(0) * pl.num_programs(1) + pl.program_id(1)
    slot = i & 1

    def run_copies(tile, sl, wait):
        def act(pos, start, size):
            cp = pltpu.make_async_copy(ob_ref.at[_rows(start, size, rt)], slab.at[sl, _rows(pos, size, rt)],
                                       sems.at[sl])
            if wait:
                cp.wait()
            else:
                cp.start()

        _for_each_run_chunk(run_ref, cnt_ref, tile, ne, tl, act)

    def step(sl):
        @pl.when(i == 0)
        def _():
            run_copies(i, sl, False)

        @pl.when(i + 1 < n_tiles)
        def _():
            run_copies(i + 1, 1 - sl, False)

        run_copies(i, sl, True)

        def row(k, t):
            return slab[sl, pl.ds(pl.multiple_of(lpos_ref[k, t], rt), rt), :]

        def token(t, c):
            a = gate_ref[0, t] * row(0, t)
            for k in range(1, TOP_K):
                a = a + gate_ref[k, t] * row(k, t)
            acc[_rows(t, 1, rt), :] = a
            return c

        lax.fori_loop(0, tl, token, 0, unroll=8)
        o_ref[0] = xn_ref[0] + g2_ref[0] * _load_row_tiles(acc, rt)

    for sl in range(2):
        pl.when(slot == sl)(functools.partial(step, sl))


def _combine(out_buf, lpos, gates, run_start, run_cnt, xn, g2, rt, tl):
    b, s, d = xn.shape
    nt = s // tl
    tok = lambda bi, i, *_: (0, bi * nt + i)
    return pl.pallas_call(
        functools.partial(_combine_kernel, rt=rt, ne=N_EXPERTS, n_tiles=b * nt),
        grid_spec=pltpu.PrefetchScalarGridSpec(
            num_scalar_prefetch=2,
            grid=(b, nt),
            in_specs=[pl.BlockSpec((TOP_K, tl), tok, memory_space=pltpu.SMEM),
                      pl.BlockSpec((TOP_K, tl), tok, memory_space=pltpu.SMEM),
                      pl.BlockSpec((1, tl, d), lambda bi, i, *_: (bi, i, 0)),
                      pl.BlockSpec((1, 1, d), lambda bi, i, *_: (bi, 0, 0)),
                      pl.BlockSpec(memory_space=pl.ANY)],
            out_specs=pl.BlockSpec((1, tl, d), lambda bi, i, *_: (bi, i, 0)),
            scratch_shapes=[pltpu.VMEM((2, TOP_K * tl * rt, LANES), F32), pltpu.VMEM((tl * rt, LANES), F32),
                            pltpu.SemaphoreType.DMA((2,))]),
        out_shape=jax.ShapeDtypeStruct((b, s, d), F32),
        compiler_params=_cparams(("arbitrary", "arbitrary")),
        name="moe_combine",
    )(run_start, run_cnt, lpos, gates, xn, g2, out_buf)


def _moe(h2r, logits_t, xn, g2, w1, b1, w2, b2):
    b, s, d = xn.shape
    t = b * s
    rt = d // LANES
    tl = min(TOKEN_TILE, s)
    assert s % tl == 0
    n_blocks = (t * TOP_K) // EXPERT_ROWS + N_EXPERTS
    n_rows = n_blocks * EXPERT_ROWS
    gates, lpos, tstart, tcnt, cnt = _route(logits_t, tl, rt)
    run, blk, meta = _slots(cnt, tstart, n_blocks, EXPERT_ROWS)
    run_start = run[:, 0, :N_EXPERTS].reshape(-1)
    run_cnt = tcnt[:, 0, :N_EXPERTS].reshape(-1)
    buf = _dispatch(h2r, lpos, run_start, run_cnt, meta[0, :N_EXPERTS], meta[1, :N_EXPERTS], meta[2, :1],
                    n_rows, EXPERT_ROWS, rt, tl)
    out_buf = _experts(buf, blk[0, :n_blocks], meta[2, :1], w1, b1, w2, b2, EXPERT_ROWS)
    return _combine(out_buf, lpos, gates, run_start, run_cnt, xn, g2, rt, tl)


def _layer(x, ctx, c, c_ctx, p, lam_init):
    b, s, d = x.shape
    attn_w = d // 2
    hw = d - attn_w
    v_dim = attn_w // N_HEADS
    qk_dim = v_dim // 2
    qk_cols = N_HEADS * 2 * qk_dim
    v_cols = N_HEADS * v_dim
    assert 2 * qk_dim == LANES and v_dim == LANES and s % GRID_W == 0

    rows = -(-(b + 1) // 8) * 8
    cc = jnp.zeros((rows, d), F32).at[:b].set(c).at[b].set(c_ctx)
    mod = _modulation(cc, p['w_mod'], p['b_mod'])
    mod_x = mod[:b].reshape(b, N_MOD, 1, d)
    sh1, sc1, g1, sh2, sc2, g2 = [mod_x[:, i] for i in range(N_MOD)]
    mod_c = mod[b:b + 1].reshape(1, N_MOD, 1, d)
    csh1, csc1 = mod_c[:, 0], mod_c[:, 1]

    w_in_bf = p['w_in'].astype(BF16)
    qg = jnp.tile(p['q_norm_g'], qk_cols // qk_dim).reshape(1, qk_cols)
    kg = jnp.tile(p['k_norm_g'], qk_cols // qk_dim).reshape(1, qk_cols)
    n1g = p['norm1_g'].reshape(1, d)
    cos_t, sin_t = _rope_tables(s, qk_dim)
    q, k, v, u_hy = _project_latent(x, sh1, sc1, n1g, w_in_bf, qg, kg, cos_t, sin_t, qk_cols, v_cols, qk_dim)
    k_c, v_c = _project_context(ctx, csh1, csc1, n1g, w_in_bf[:, qk_cols:2 * qk_cols + v_cols], kg,
                                qk_cols, v_cols, qk_dim)
    vec = lambda a: a.reshape(1, qk_dim)
    attn = _diff_attention(q, k_c, k, v_c, v, vec(p['lam_q1']), vec(p['lam_k1']), vec(p['lam_q2']),
                           vec(p['lam_k2']), p['subln_g'].reshape(1, v_dim), lam_init, qk_dim)

    hfilt = _hyena_filters(s, hw, p['hy_w1'], p['hy_b1'], p['hy_f1'], p['hy_w2'], p['hy_b2'], p['hy_f2'], p['hy_w3'])
    mf, mi = _dft_matrices(s)
    g_spec = _filter_spectra(mf, hfilt, hw)
    uc = _short_conv(u_hy, p['hy_conv_w'], p['hy_conv_b'])
    y1 = _fwd_dft(mf, uc, 0, g_spec, 0, hw)
    z1 = _inv_dft(mi, y1, uc, 0, uc, 1, p['hy_skip'], 0, None, hw)
    y2 = _fwd_dft(mf, z1, 0, g_spec, 1, hw)
    hyn = _inv_dft(mi, y2, z1, 0, uc, 2, p['hy_skip'], 1, p['hy_out_g'], hw)

    xn, h2, logits_t = _out_project(attn, hyn, x, p['w_out'].astype(BF16), g1, sh2, sc2,
                                    p['norm2_g'].reshape(1, d), p['router_w'].T,
                                    p['router_b'].reshape(N_EXPERTS, 1))
    return _moe(h2, logits_t, xn, g2, p['exp_w1'], p['exp_b1'], p['exp_w2'], p['exp_b2'])


def kernel(x, c, ctx, c_ctx, w_mod, b_mod, norm1_g, norm2_g, w_in, q_norm_g, k_norm_g, lam_q1, lam_k1, lam_q2, lam_k2, subln_g, hy_conv_w, hy_conv_b, hy_w1, hy_b1, hy_f1, hy_w2, hy_b2, hy_f2, hy_w3, hy_skip, hy_out_g, w_out, router_w, router_b, exp_w1, exp_b1, exp_w2, exp_b2):
    depth = w_mod.shape[0]
    assert depth == 1, "context-token update between layers is not implemented"
    p = {
        'w_mod': w_mod[0], 'b_mod': b_mod[0], 'norm1_g': norm1_g[0], 'norm2_g': norm2_g[0],
        'w_in': w_in[0], 'q_norm_g': q_norm_g[0], 'k_norm_g': k_norm_g[0],
        'lam_q1': lam_q1[0], 'lam_k1': lam_k1[0], 'lam_q2': lam_q2[0], 'lam_k2': lam_k2[0],
        'subln_g': subln_g[0], 'hy_conv_w': hy_conv_w[0], 'hy_conv_b': hy_conv_b[0],
        'hy_w1': hy_w1[0], 'hy_b1': hy_b1[0], 'hy_f1': hy_f1[0], 'hy_w2': hy_w2[0],
        'hy_b2': hy_b2[0], 'hy_f2': hy_f2[0], 'hy_w3': hy_w3[0], 'hy_skip': hy_skip[0],
        'hy_out_g': hy_out_g[0], 'w_out': w_out[0], 'router_w': router_w[0],
        'router_b': router_b[0], 'exp_w1': exp_w1[0], 'exp_b1': exp_b1[0],
        'exp_w2': exp_w2[0], 'exp_b2': exp_b2[0],
    }
    lam_init = 0.8 - 0.6 * math.exp(-0.3 * 0)
    return _layer(x, ctx, c, c_ctx, p, lam_init)
```

```python
import functools
import math

import jax
import jax.numpy as jnp
from jax import lax
from jax.experimental import pallas as pl
from jax.experimental.pallas import tpu as pltpu

F32 = jnp.float32
BF16 = jnp.bfloat16
I32 = jnp.int32

GRID_W = 64
N_HEADS = 4
N_MOD = 6
SHORT_CONV = 3
HYENA_ORDER = 2
N_BANDS = 8
FEAT_DIM = 1 + 2 * N_BANDS
FILTER_HIDDEN = 64
DECAY_TARGET = 1e-2
FAST_DECAY_PCT = 0.3
SLOW_DECAY_PCT = 1.5
N_EXPERTS = 32
TOP_K = 4
SWIGLU_LIMIT = 7.0
SWIGLU_ALPHA = 1.702
ROPE_BASE = 10000.0
EPS = 1e-6

LANES = 128
V7X_VMEM_LIMIT = 56 * 1024 * 1024

ROW_TILE = 512
ATT_Q_TILE = 256
ATT_KEY_CHUNKS = 17
DFT_TILE = 256
RADIX = 4
EXPERT_ROWS = 512
TOKEN_TILE = 512


def _log2(n):
    assert n > 0 and n & (n - 1) == 0, f"{n} must be a power of two"
    return n.bit_length() - 1


def _cparams(sem, vmem=V7X_VMEM_LIMIT):
    return pltpu.CompilerParams(dimension_semantics=sem, vmem_limit_bytes=vmem)


def _split_bf16(a):
    hi = a.astype(BF16)
    lo = (a - hi.astype(F32)).astype(BF16)
    return hi, lo


def _dot(a, b):
    return jnp.dot(a, b, preferred_element_type=F32)


def _dot_nt(a, b):
    return lax.dot_general(a, b, (((1,), (1,)), ((), ())), preferred_element_type=F32)


def _store_row_tiles(ref, val):
    rows, d = val.shape
    rt = d // LANES
    for c in range(rt):
        ref[pl.ds(c, rows, stride=rt), :] = val[:, c * LANES:(c + 1) * LANES]


def _load_row_tiles(ref, rt):
    rows = ref.shape[0] // rt
    return jnp.concatenate([ref[pl.ds(c, rows, stride=rt), :] for c in range(rt)], axis=1)


def _dot3(a, b):
    ah, al = _split_bf16(a)
    bh, bl = _split_bf16(b)
    return _dot(ah, bh) + (_dot(ah, bl) + _dot(al, bh))


def _mod_kernel(c_ref, w_ref, b_ref, o_ref):
    c = c_ref[...]
    s = c * jax.nn.sigmoid(c)
    o_ref[...] = _dot3(s, w_ref[...]) + b_ref[...]


def _modulation(cc, w_mod, b_mod):
    rows, d = cc.shape
    n = w_mod.shape[1]
    tn = min(n, 1536)
    return pl.pallas_call(
        _mod_kernel,
        grid=(n // tn,),
        in_specs=[pl.BlockSpec((rows, d), lambda j: (0, 0)),
                  pl.BlockSpec((d, tn), lambda j: (0, j)),
                  pl.BlockSpec((1, tn), lambda j: (0, j))],
        out_specs=pl.BlockSpec((rows, tn), lambda j: (0, j)),
        out_shape=jax.ShapeDtypeStruct((rows, n), F32),
        compiler_params=_cparams(("parallel",)),
        name="modulation",
    )(cc, w_mod, b_mod.reshape(1, n))


def _rope_table_kernel(cos_ref, sin_ref, *, qk_dim):
    s, w = cos_ref.shape
    half = qk_dim // 2
    nf = half // 2
    t = lax.broadcasted_iota(I32, (s, w), 0)
    lane = lax.broadcasted_iota(I32, (s, w), 1)
    d = lane & (qk_dim - 1)
    j = d & (nf - 1)
    row = t >> _log2(GRID_W)
    col = t & (GRID_W - 1)
    pos = jnp.where(d < half, row, col).astype(F32)
    inv = jnp.exp(j.astype(F32) * (-math.log(ROPE_BASE) / nf))
    ang = pos * inv
    first = (d & (half - 1)) < nf
    cos_ref[...] = jnp.cos(ang)
    sn = jnp.sin(ang)
    sin_ref[...] = jnp.where(first, -sn, sn)


def _rope_tables(s, qk_dim):
    return pl.pallas_call(
        functools.partial(_rope_table_kernel, qk_dim=qk_dim),
        out_shape=(jax.ShapeDtypeStruct((s, LANES), F32), jax.ShapeDtypeStruct((s, LANES), F32)),
        name="rope_tables",
    )()


def _group_rms(t, gain, qk_dim):
    w = t.shape[1]
    r = lax.broadcasted_iota(I32, (w, w), 0) >> _log2(qk_dim)
    c = lax.broadcasted_iota(I32, (w, w), 1) >> _log2(qk_dim)
    bd = jnp.where(r == c, 1.0 / qk_dim, 0.0).astype(BF16)
    hi, lo = _split_bf16(t * t)
    ms = _dot(hi, bd) + _dot(lo, bd)
    return t * lax.rsqrt(ms + EPS) * gain


def _rope(t, cos, sin_signed, qk_dim):
    w = t.shape[1]
    nf = qk_dim // 4
    lane = lax.broadcasted_iota(I32, t.shape, 1)
    first = (lane & (2 * nf - 1)) < nf
    partner = jnp.where(first, pltpu.roll(t, w - nf, axis=1), pltpu.roll(t, nf, axis=1))
    return t * cos + partner * sin_signed


def _proj_kernel(*refs, latent, qk_cols, v_cols, qk_dim):
    if latent:
        (x_ref, sh_ref, sc_ref, g_ref, w_ref, qg_ref, kg_ref, cos_ref, sin_ref,
         q_out, k_out, v_out, u_out) = refs
    else:
        x_ref, sh_ref, sc_ref, g_ref, w_ref, kg_ref, k_out, v_out = refs
    x = x_ref[0]
    ms = jnp.mean(x * x, axis=-1, keepdims=True)
    h = (x * lax.rsqrt(ms + EPS) * g_ref[...]) * (1.0 + sc_ref[0]) + sh_ref[0]
    proj = _dot(h.astype(BF16), w_ref[...])
    if latent:
        reps = qk_cols // LANES
        cos = jnp.concatenate([cos_ref[...]] * reps, axis=1)
        sin = jnp.concatenate([sin_ref[...]] * reps, axis=1)
        q = _rope(_group_rms(proj[:, :qk_cols], qg_ref[...], qk_dim), cos, sin, qk_dim)
        q_out[0] = (q * (qk_dim ** -0.5 * math.log2(math.e))).astype(BF16)
        k = _rope(_group_rms(proj[:, qk_cols:2 * qk_cols], kg_ref[...], qk_dim), cos, sin, qk_dim)
        k_out[0] = k.astype(BF16)
        v_out[0] = proj[:, 2 * qk_cols:2 * qk_cols + v_cols].astype(BF16)
        u_out[0] = proj[:, 2 * qk_cols + v_cols:].astype(BF16)
    else:
        k = _group_rms(proj[:, :qk_cols], kg_ref[...], qk_dim)
        k_out[0] = k.astype(BF16)
        v_out[0] = proj[:, qk_cols:qk_cols + v_cols].astype(BF16)


def _project_latent(x, sh, sc, g, w_bf, qg, kg, cos_t, sin_t, qk_cols, v_cols, qk_dim):
    b, s, d = x.shape
    n = w_bf.shape[1]
    hy_cols = n - 2 * qk_cols - v_cols
    tm = min(ROW_TILE, s)
    row = lambda bi, i: (bi, i, 0)
    per_b = lambda bi, i: (bi, 0, 0)
    const = lambda bi, i: (0, 0)
    return pl.pallas_call(
        functools.partial(_proj_kernel, latent=True, qk_cols=qk_cols, v_cols=v_cols, qk_dim=qk_dim),
        grid=(b, s // tm),
        in_specs=[pl.BlockSpec((1, tm, d), row),
                  pl.BlockSpec((1, 1, d), per_b), pl.BlockSpec((1, 1, d), per_b),
                  pl.BlockSpec((1, d), const), pl.BlockSpec((d, n), const),
                  pl.BlockSpec((1, qk_cols), const), pl.BlockSpec((1, qk_cols), const),
                  pl.BlockSpec((tm, LANES), lambda bi, i: (i, 0)),
                  pl.BlockSpec((tm, LANES), lambda bi, i: (i, 0))],
        out_specs=[pl.BlockSpec((1, tm, qk_cols), row), pl.BlockSpec((1, tm, qk_cols), row),
                   pl.BlockSpec((1, tm, v_cols), row), pl.BlockSpec((1, tm, hy_cols), row)],
        out_shape=[jax.ShapeDtypeStruct((b, s, qk_cols), BF16), jax.ShapeDtypeStruct((b, s, qk_cols), BF16),
                   jax.ShapeDtypeStruct((b, s, v_cols), BF16), jax.ShapeDtypeStruct((b, s, hy_cols), BF16)],
        compiler_params=_cparams(("parallel", "parallel")),
        name="project_latent",
    )(x, sh, sc, g, w_bf, qg, kg, cos_t, sin_t)


def _project_context(ctx, sh, sc, g, w_bf, kg, qk_cols, v_cols, qk_dim):
    b, lc, d = ctx.shape
    n = w_bf.shape[1]
    tm = min(ROW_TILE, lc)
    row = lambda bi, i: (bi, i, 0)
    shared = lambda bi, i: (0, 0, 0)
    const = lambda bi, i: (0, 0)
    return pl.pallas_call(
        functools.partial(_proj_kernel, latent=False, qk_cols=qk_cols, v_cols=v_cols, qk_dim=qk_dim),
        grid=(b, lc // tm),
        in_specs=[pl.BlockSpec((1, tm, d), row),
                  pl.BlockSpec((1, 1, d), shared), pl.BlockSpec((1, 1, d), shared),
                  pl.BlockSpec((1, d), const), pl.BlockSpec((d, n), const),
                  pl.BlockSpec((1, qk_cols), const)],
        out_specs=[pl.BlockSpec((1, tm, qk_cols), row), pl.BlockSpec((1, tm, v_cols), row)],
        out_shape=[jax.ShapeDtypeStruct((b, lc, qk_cols), BF16), jax.ShapeDtypeStruct((b, lc, v_cols), BF16)],
        compiler_params=_cparams(("parallel", "parallel")),
        name="project_context",
    )(ctx, sh, sc, g, w_bf, kg)


def _key_chunks(kk, n):
    tiles = kk // LANES
    n = min(n, tiles)
    return [(LANES * (i * tiles // n), LANES * ((i + 1) * tiles // n)) for i in range(n)]


def _attn_kernel(q_ref, kc_ref, k_ref, vc_ref, v_ref, lq1, lk1, lq2, lk2, sg_ref, o_ref, kt_ref, v1_ref, *bufs,
                 lam_init, qk_dim, n_tiles):
    lam = (jnp.exp(jnp.sum(lq1[...] * lk1[...], axis=-1, keepdims=True))
           - jnp.exp(jnp.sum(lq2[...] * lk2[...], axis=-1, keepdims=True)) + lam_init)
    j = pl.program_id(2)
    even, odd = bufs[:4], bufs[4:]
    tq = q_ref.shape[1]
    chunks = _key_chunks(kt_ref.shape[1], ATT_KEY_CHUNKS)

    @pl.when(j == 0)
    def _():
        lc = kc_ref.shape[1]
        kt_ref[:, :lc] = kc_ref[0].T
        kt_ref[:, lc:] = k_ref[0].T
        v1_ref[:lc, :LANES] = vc_ref[0]
        v1_ref[lc:, :LANES] = v_ref[0]
        v1_ref[:, LANES:] = jnp.ones((v1_ref.shape[0], LANES), BF16)

    def step(cur, prev, score=True, attend=True):
        s1_w, s2_w, m1_w, m2_w = cur
        s1_r, s2_r, m1_r, m2_r = prev
        if score:
            q = q_ref[0]
            lane = lax.broadcasted_iota(I32, q.shape, 1)
            q1 = jnp.where(lane < qk_dim, q, jnp.zeros_like(q))
            q2 = jnp.where(lane >= qk_dim, q, jnp.zeros_like(q))
            m1 = m2 = jnp.full((tq, 1), -jnp.inf, F32)
        if attend:
            m1p, m2p = m1_r[:, :1], m2_r[:, :1]
            o1 = o2 = jnp.zeros((tq, 2 * LANES), F32)
        for c0, c1 in chunks:
            if score:
                s1 = _dot(q1, kt_ref[:, c0:c1])
                s2 = _dot(q2, kt_ref[:, c0:c1])
                s1_w[:, c0:c1] = s1
                s2_w[:, c0:c1] = s2
                m1 = jnp.maximum(m1, jnp.max(s1, axis=-1, keepdims=True))
                m2 = jnp.maximum(m2, jnp.max(s2, axis=-1, keepdims=True))
            if attend:
                e1 = jnp.exp2(s1_r[:, c0:c1] - m1p)
                e2 = jnp.exp2(s2_r[:, c0:c1] - m2p)
                o1 = o1 + _dot(e1.astype(BF16), v1_ref[c0:c1, :])
                o2 = o2 + _dot(e2.astype(BF16), v1_ref[c0:c1, :])
        if score:
            m1_w[...] = jnp.broadcast_to(m1, m1_w.shape)
            m2_w[...] = jnp.broadcast_to(m2, m2_w.shape)
        if attend:
            o = o1[:, :LANES] / o1[:, LANES:] - o2[:, :LANES] * (lam / o2[:, LANES:])
            ms = jnp.mean(o * o, axis=-1, keepdims=True)
            o_ref[0] = ((o * lax.rsqrt(ms + EPS) * sg_ref[...]) * (1.0 - lam_init)).astype(BF16)

    by_parity = (lambda **kw: step(even, odd, **kw)), (lambda **kw: step(odd, even, **kw))

    @pl.when(j == 0)
    def _():
        by_parity[0](attend=False)

    for parity in range(2):
        @pl.when((j > 0) & (j < n_tiles) & ((j & 1) == parity))
        def _():
            by_parity[parity]()

    @pl.when(j == n_tiles)
    def _():
        by_parity[n_tiles % 2](score=False)


def _diff_attention(q, k_c, k, v_c, v, lq1, lk1, lq2, lk2, subln_g, lam_init, qk_dim):
    b, s, w = q.shape
    lc = k_c.shape[1]
    assert lc % LANES == 0
    kk = lc + s
    tq = min(ATT_Q_TILE, s)
    nq = s // tq
    kv = lambda bi, h, i: (bi, 0, h)
    const = lambda bi, h, i: (0, 0)
    vec = pl.BlockSpec((1, qk_dim), const)
    wide = pltpu.VMEM((tq, kk), F32)
    stat = pltpu.VMEM((tq, LANES), F32)
    per_parity = [wide, wide, stat, stat]
    return pl.pallas_call(
        functools.partial(_attn_kernel, lam_init=lam_init, qk_dim=qk_dim, n_tiles=nq),
        grid=(b, N_HEADS, nq + 1),
        in_specs=[pl.BlockSpec((1, tq, LANES), lambda bi, h, i: (bi, jnp.minimum(i, nq - 1), h)),
                  pl.BlockSpec((1, lc, LANES), kv), pl.BlockSpec((1, s, LANES), kv),
                  pl.BlockSpec((1, lc, LANES), kv), pl.BlockSpec((1, s, LANES), kv), vec, vec, vec, vec,
                  pl.BlockSpec((1, LANES), const)],
        out_specs=pl.BlockSpec((1, tq, LANES), lambda bi, h, i: (bi, jnp.maximum(i - 1, 0), h)),
        out_shape=jax.ShapeDtypeStruct((b, s, w), BF16),
        scratch_shapes=[pltpu.VMEM((LANES, kk), BF16), pltpu.VMEM((kk, 2 * LANES), BF16)] + per_parity + per_parity,
        compiler_params=_cparams(("parallel", "parallel", "arbitrary")),
        name="diff_attention",
    )(q, k_c, k, v_c, v, lq1, lk1, lq2, lk2, subln_g)


def _filter_kernel(w1_ref, b1_ref, f1_ref, w2_ref, b2_ref, f2_ref, w3_ref, o_ref, *, seq, hw):
    tl, n = o_ref.shape
    base = pl.program_id(0) * tl
    quarter = seq // RADIX

    def position(shape):
        p = lax.broadcasted_iota(I32, shape, 0) + base
        return (((p & (quarter - 1)) << _log2(RADIX)) | (p >> _log2(quarter))).astype(F32)

    pos = position((tl, LANES))
    lane = lax.broadcasted_iota(I32, (tl, LANES), 1)
    tn = pos / seq
    band_idx = jnp.where(lane <= N_BANDS, lane - 1, lane - 1 - N_BANDS).astype(F32)
    band = 1e-4 + band_idx * ((N_BANDS - 1 - 1e-4) / (N_BANDS - 1))
    ang = (2.0 * math.pi / seq) * pos * band
    feats = jnp.where(lane == 0, tn,
                      jnp.where(lane <= N_BANDS, jnp.sin(ang),
                                jnp.where(lane < FEAT_DIM, jnp.cos(ang), 0.0)))
    h = jnp.sin(f1_ref[...] * (_dot3(feats, w1_ref[...]) + b1_ref[...]))
    h = jnp.sin(f2_ref[...] * (_dot3(h, w2_ref[...]) + b2_ref[...]))
    h = _dot3(h, w3_ref[...])
    ch = (lax.broadcasted_iota(I32, (tl, n), 1) & ((1 << _log2(hw)) - 1)).astype(F32)
    lo = abs(math.log(DECAY_TARGET) / SLOW_DECAY_PCT)
    hi = abs(math.log(DECAY_TARGET) / FAST_DECAY_PCT)
    delta = lo + ch * ((hi - lo) / (hw - 1))
    o_ref[...] = (h * jnp.exp(-(position((tl, n)) / seq) * delta)).astype(BF16)


def _hyena_filters(seq, hw, w1, b1, f1, w2, b2, f2, w3):
    fh = w2.shape[0]
    n = w3.shape[1]
    w1p = jnp.zeros((LANES, fh), F32).at[:FEAT_DIM].set(w1)
    tl = min(ROW_TILE, seq)
    const = lambda i: (0, 0)
    return pl.pallas_call(
        functools.partial(_filter_kernel, seq=seq, hw=hw),
        grid=(seq // tl,),
        in_specs=[pl.BlockSpec((LANES, fh), const), pl.BlockSpec((1, fh), const), pl.BlockSpec((1, fh), const),
                  pl.BlockSpec((fh, fh), const), pl.BlockSpec((1, fh), const), pl.BlockSpec((1, fh), const),
                  pl.BlockSpec((fh, n), const)],
        out_specs=pl.BlockSpec((tl, n), lambda i: (i, 0)),
        out_shape=jax.ShapeDtypeStruct((seq, n), BF16),
        compiler_params=_cparams(("parallel",)),
        name="hyena_filters",
    )(w1p, b1.reshape(1, fh), f1.reshape(1, fh), w2, b2.reshape(1, fh), f2.reshape(1, fh), w3)


def _dft_kernel(mf_ref, mi_ref, tfc, tfs, tic, tis, *, seq):
    rows, q = tic.shape
    mask = (1 << _log2(4 * seq)) - 1
    unit = math.pi / (2 * seq)
    i_row = lax.broadcasted_iota(I32, (rows, q), 0)
    col = lax.broadcasted_iota(I32, (rows, q), 1)

    @pl.when(pl.program_id(0) == 0)
    def _():
        for r in range(RADIX):
            af = ((2 * i_row * (RADIX * col + r)) & mask).astype(F32) * unit
            tfc[r] = jnp.cos(af)
            tfs[r] = jnp.sin(af)
        ai = (((2 * col + 1) * (RADIX * i_row)) & mask).astype(F32) * unit
        tic[...] = jnp.cos(ai)
        tis[...] = jnp.sin(ai)

    r0 = pl.program_id(0) * rows
    c1 = lax.broadcasted_iota(I32, (1, q), 1)
    for r in range(RADIX):
        bf = (((2 * r0 + 1) * (RADIX * c1 + r)) & mask).astype(F32) * unit
        bi = (((2 * c1 + 1) * (RADIX * r0 + r)) & mask).astype(F32) * unit
        cbf, sbf = jnp.cos(bf), jnp.sin(bf)
        cbi, sbi = jnp.cos(bi), jnp.sin(bi)
        mf_ref[r, 0] = (cbf * tfc[r] - sbf * tfs[r]).astype(BF16)
        mf_ref[r, 1] = (sbf * tfc[r] + cbf * tfs[r]).astype(BF16)
        mi_ref[r, :, :q] = (cbi * tic[...] - sbi * tis[...]).astype(BF16)
        mi_ref[r, :, q:] = (sbi * tic[...] + cbi * tis[...]).astype(BF16)


def _dft_matrices(seq):
    q = seq // RADIX
    rows = min(DFT_TILE, q)
    return pl.pallas_call(
        functools.partial(_dft_kernel, seq=seq),
        grid=(q // rows,),
        out_specs=[pl.BlockSpec((RADIX, 2, rows, q), lambda i: (0, 0, i, 0)),
                   pl.BlockSpec((RADIX, rows, 2 * q), lambda i: (0, i, 0))],
        out_shape=[jax.ShapeDtypeStruct((RADIX, 2, q, q), BF16), jax.ShapeDtypeStruct((RADIX, q, 2 * q), BF16)],
        scratch_shapes=[pltpu.VMEM((RADIX, rows, q), F32), pltpu.VMEM((RADIX, rows, q), F32),
                        pltpu.VMEM((rows, q), F32), pltpu.VMEM((rows, q), F32)],
        compiler_params=_cparams(("arbitrary",)),
        name="dft_matrices",
    )()


def _class_transform(mf_ref, x_of_class):
    tc, ts = [], []
    for r in range(RADIX):
        x = x_of_class(r)
        tc.append(_dot(mf_ref[r, 0], x))
        ts.append(_dot(mf_ref[r, 1], x))
    return [(tc[0] + tc[1] + tc[2] + tc[3], ts[0] + ts[1] + ts[2] + ts[3]),
            (tc[0] - tc[1] + tc[2] - tc[3], ts[1] - ts[0] + ts[3] - ts[2]),
            (tc[0] - ts[1] - tc[2] + ts[3], ts[0] + tc[1] - ts[2] - tc[3]),
            (tc[0] + ts[1] - tc[2] - ts[3], tc[1] - ts[0] + ts[2] - tc[3])]


def _spectrum_kernel(mf_ref, h_ref, g_ref, *, seq, hw):
    q = seq // RADIX
    groups = _class_transform(mf_ref, lambda r: h_ref[r * q:(r + 1) * q, :])
    scale = 1.0 / seq
    for x, (hc, hs) in enumerate(groups):
        g_ref[0, x, 0] = ((hc[:, :hw] + hc[:, hw:]) * scale).astype(g_ref.dtype)
        g_ref[0, x, 1] = ((hs[:, :hw] - hs[:, hw:]) * scale).astype(g_ref.dtype)


def _filter_spectra(mf, hfilt, hw):
    q = mf.shape[2]
    seq = q * RADIX
    rows = min(DFT_TILE, q)
    return pl.pallas_call(
        functools.partial(_spectrum_kernel, seq=seq, hw=hw),
        grid=(HYENA_ORDER, q // rows),
        in_specs=[pl.BlockSpec((RADIX, 2, rows, q), lambda n, i: (0, 0, i, 0)),
                  pl.BlockSpec((seq, 2 * hw), lambda n, i: (0, n))],
        out_specs=pl.BlockSpec((1, RADIX, 2, rows, hw), lambda n, i: (n, 0, 0, i, 0)),
        out_shape=jax.ShapeDtypeStruct((HYENA_ORDER, RADIX, 2, q, hw), BF16),
        compiler_params=_cparams(("parallel", "parallel")),
        name="filter_spectra",
    )(mf, hfilt)


def _short_conv_kernel(u_ref, w_ref, b_ref, o_ref, y_ref):
    s, tc = u_ref.shape[1], u_ref.shape[2]
    q = s // RADIX
    t = lax.broadcasted_iota(I32, (s, LANES), 0)
    for c in range(tc // LANES):
        cs = slice(c * LANES, (c + 1) * LANES)
        u = u_ref[0, :, cs].astype(F32)
        prev = jnp.where(t == 0, 0.0, pltpu.roll(u, 1, axis=0))
        nxt = jnp.where(t == s - 1, 0.0, pltpu.roll(u, s - 1, axis=0))
        y_ref[...] = b_ref[:, cs] + prev * w_ref[0:1, cs] + u * w_ref[1:2, cs] + nxt * w_ref[2:3, cs]
        for r in range(RADIX):
            o_ref[0, r * q:(r + 1) * q, cs] = y_ref[pl.ds(r, q, stride=RADIX), :].astype(o_ref.dtype)


def _short_conv(u, w, bias):
    b, s, c = u.shape
    tc = min(512, c)
    return pl.pallas_call(
        _short_conv_kernel,
        grid=(b, c // tc),
        in_specs=[pl.BlockSpec((1, s, tc), lambda bi, j: (bi, 0, j)),
                  pl.BlockSpec((SHORT_CONV, tc), lambda bi, j: (0, j)),
                  pl.BlockSpec((1, tc), lambda bi, j: (0, j))],
        out_specs=pl.BlockSpec((1, s, tc), lambda bi, j: (bi, 0, j)),
        out_shape=jax.ShapeDtypeStruct((b, s, c), BF16),
        scratch_shapes=[pltpu.VMEM((s, LANES), F32)],
        compiler_params=_cparams(("parallel", "parallel")),
        name="short_conv",
    )(u, w, bias.reshape(1, c))


def _fwd_dft_kernel(mf_ref, z_ref, g_ref, y_ref, zb):
    q = zb.shape[0] // RADIX

    @pl.when(pl.program_id(1) == 0)
    def _():
        zb[...] = z_ref[0].astype(BF16)

    groups = _class_transform(mf_ref, lambda r: zb[r * q:(r + 1) * q, :])
    yc, ys = [], []
    for x, (uc, us) in enumerate(groups):
        gc, gs = g_ref[0, x, 0].astype(F32), g_ref[0, x, 1].astype(F32)
        yc.append(uc * gc - us * gs)
        ys.append(uc * gs + us * gc)
    a, b, c, d = range(RADIX)
    z = [(yc[a] + yc[c] + yc[b] + yc[d], ys[a] + ys[c] - ys[b] - ys[d]),
         (yc[a] + ys[c] - yc[b] + ys[d], ys[a] - yc[c] + ys[b] + yc[d]),
         (yc[a] - yc[c] + yc[b] - yc[d], ys[a] - ys[c] - ys[b] + ys[d]),
         (yc[a] - ys[c] - yc[b] - ys[d], ys[a] + yc[c] + ys[b] - yc[d])]
    for r, (zc, zs) in enumerate(z):
        y_ref[0, r, 0] = zc.astype(BF16)
        y_ref[0, r, 1] = zs.astype(BF16)


def _fwd_dft(mf, z, z_col, g, order, hw):
    b, seq = z.shape[0], z.shape[1]
    q = seq // RADIX
    rows = min(DFT_TILE, q)
    return pl.pallas_call(
        _fwd_dft_kernel,
        grid=(b, q // rows),
        in_specs=[pl.BlockSpec((RADIX, 2, rows, q), lambda bi, i: (0, 0, i, 0)),
                  pl.BlockSpec((1, seq, hw), lambda bi, i: (bi, 0, z_col)),
                  pl.BlockSpec((1, RADIX, 2, rows, hw), lambda bi, i: (order, 0, 0, i, 0))],
        out_specs=pl.BlockSpec((1, RADIX, 2, rows, hw), lambda bi, i: (bi, 0, 0, i, 0)),
        out_shape=jax.ShapeDtypeStruct((b, RADIX, 2, q, hw), BF16),
        scratch_shapes=[pltpu.VMEM((seq, hw), BF16)],
        compiler_params=_cparams(("parallel", "arbitrary")),
        name="hyena_fwd_dft",
    )(mf, z, g)


def _inv_dft_kernel(mi_ref, y_ref, z_ref, gate_ref, skip_ref, *rest, final):
    if final:
        og_ref, o_ref = rest
    else:
        (o_ref,) = rest
    rows = mi_ref.shape[1]
    for r in range(RADIX):
        conv = _dot(mi_ref[r], y_ref[0, r])
        z = gate_ref[0, r].astype(F32) * (conv + z_ref[0, r].astype(F32) * skip_ref[0])
        if final:
            ms = jnp.mean(z * z, axis=-1, keepdims=True)
            zn = z * lax.rsqrt(ms + EPS) * og_ref[...]
            for c in range(zn.shape[1] // LANES):
                o_ref[0, c, pl.ds(r, rows, stride=RADIX), :] = zn[:, c * LANES:(c + 1) * LANES]
        else:
            o_ref[0, r] = z.astype(o_ref.dtype)


def _inv_dft(mi, y, z, z_col, gates, gate_col, skip, order, out_g, hw):
    b, seq = z.shape[0], z.shape[1]
    q = seq // RADIX
    rows = min(DFT_TILE, q)
    final = out_g is not None
    by_class = lambda a: a.reshape(b, RADIX, q, a.shape[2])
    in_specs = [pl.BlockSpec((RADIX, rows, 2 * q), lambda bi, i: (0, i, 0)),
                pl.BlockSpec((1, RADIX, 2 * q, hw), lambda bi, i: (bi, 0, 0, 0)),
                pl.BlockSpec((1, RADIX, rows, hw), lambda bi, i: (bi, 0, i, z_col)),
                pl.BlockSpec((1, RADIX, rows, hw), lambda bi, i: (bi, 0, i, gate_col)),
                pl.BlockSpec((1, 1, hw), lambda bi, i: (order, 0, 0))]
    args = [mi, y.reshape(b, RADIX, 2 * q, hw), by_class(z), by_class(gates), skip.reshape(HYENA_ORDER, 1, hw)]
    if final:
        in_specs.append(pl.BlockSpec((1, hw), lambda bi, i: (0, 0)))
        args.append(out_g.reshape(1, hw))
        out_spec = pl.BlockSpec((1, hw // LANES, RADIX * rows, LANES), lambda bi, i: (bi, 0, i, 0))
        out_shape = jax.ShapeDtypeStruct((b, hw // LANES, seq, LANES), F32)
    else:
        out_spec = pl.BlockSpec((1, RADIX, rows, hw), lambda bi, i: (bi, 0, i, 0))
        out_shape = jax.ShapeDtypeStruct((b, RADIX, q, hw), BF16)
    out = pl.pallas_call(
        functools.partial(_inv_dft_kernel, final=final),
        grid=(b, q // rows),
        in_specs=in_specs,
        out_specs=out_spec,
        out_shape=out_shape,
        compiler_params=_cparams(("parallel", "parallel")),
        name="hyena_inv_dft",
    )(*args)
    return out if final else out.reshape(b, seq, hw)


def _out_kernel(a_ref, hy_ref, x_ref, wo_ref, g1_ref, sh_ref, sc_ref, n2_ref, rw_ref, rb_ref,
                xn_ref, h2_ref, lg_ref):
    aw = a_ref.shape[2]
    hy = jnp.concatenate([hy_ref[0, c] for c in range(hy_ref.shape[1])], axis=1).astype(BF16)
    mix = _dot(a_ref[0], wo_ref[:aw, :]) + _dot(hy, wo_ref[aw:, :])
    xn = x_ref[0] + g1_ref[0] * mix
    xn_ref[0] = xn
    ms = jnp.mean(xn * xn, axis=-1, keepdims=True)
    h2 = (xn * lax.rsqrt(ms + EPS) * n2_ref[...]) * (1.0 + sc_ref[0]) + sh_ref[0]
    _store_row_tiles(h2_ref, h2)
    hh, hl = _split_bf16(h2)
    wh, wl = _split_bf16(rw_ref[...])
    lg_ref[...] = _dot_nt(wh, hh) + (_dot_nt(wh, hl) + _dot_nt(wl, hh)) + rb_ref[...]


def _out_project(attn, hyn, x, wo_bf, g1, sh2, sc2, n2g, rw_t, rb):
    b, s, d = x.shape
    aw, hw = attn.shape[2], hyn.shape[1] * hyn.shape[3]
    ne = rw_t.shape[0]
    tm = min(ROW_TILE, s)
    nt = s // tm
    row = lambda bi, i: (bi, i, 0)
    per_b = lambda bi, i: (bi, 0, 0)
    const = lambda bi, i: (0, 0)
    return pl.pallas_call(
        _out_kernel,
        grid=(b, nt),
        in_specs=[pl.BlockSpec((1, tm, aw), row),
                  pl.BlockSpec((1, hw // LANES, tm, LANES), lambda bi, i: (bi, 0, i, 0)),
                  pl.BlockSpec((1, tm, d), row),
                  pl.BlockSpec((aw + hw, d), const),
                  pl.BlockSpec((1, 1, d), per_b), pl.BlockSpec((1, 1, d), per_b), pl.BlockSpec((1, 1, d), per_b),
                  pl.BlockSpec((1, d), const), pl.BlockSpec((ne, d), const), pl.BlockSpec((ne, 1), const)],
        out_specs=[pl.BlockSpec((1, tm, d), row),
                   pl.BlockSpec((tm * (d // LANES), LANES), lambda bi, i: (bi * nt + i, 0)),
                   pl.BlockSpec((ne, tm), lambda bi, i: (0, bi * nt + i))],
        out_shape=[jax.ShapeDtypeStruct((b, s, d), F32), jax.ShapeDtypeStruct((b * s * (d // LANES), LANES), F32),
                   jax.ShapeDtypeStruct((ne, b * s), F32)],
        compiler_params=_cparams(("parallel", "parallel")),
        name="out_project",
    )(attn, hyn, x, wo_bf, g1, sh2, sc2, n2g, rw_t, rb)


def _route_kernel(lg_ref, gate_ref, lpos_ref, tstart_ref, tcnt_ref, cnt_ref, carry, *, rt):
    ne, tl = lg_ref.shape

    @pl.when(pl.program_id(0) == 0)
    def _():
        carry[...] = jnp.zeros_like(carry)

    l = lg_ref[...]
    rows = lax.broadcasted_iota(I32, (ne, tl), 0).astype(F32)
    vals, sels = [], []
    for k in range(TOP_K):
        m = jnp.max(l, axis=0, keepdims=True)
        ik = jnp.min(jnp.where(l == m, rows, float(ne)), axis=0, keepdims=True)
        sel = rows == ik
        vals.append(m)
        sels.append(sel)
        l = jnp.where(sel, -jnp.inf, l)
    exps = [jnp.exp(v - vals[0]) for v in vals]
    denom = exps[0] + exps[1] + exps[2] + exps[3]
    for k in range(TOP_K):
        gate_ref[k:k + 1, :] = exps[k] / denom
    oh = jnp.zeros((ne, tl), F32)
    for sel in sels:
        oh = oh + jnp.where(sel, 1.0, 0.0)
    r = lax.broadcasted_iota(I32, (tl, tl), 0)
    c = lax.broadcasted_iota(I32, (tl, tl), 1)
    tri = jnp.where(r <= c, 1.0, 0.0).astype(BF16)
    cum = _dot(oh.astype(BF16), tri)
    n_col = jnp.sum(oh, axis=1, keepdims=True)
    er = lax.broadcasted_iota(I32, (ne, LANES), 0)
    ec = lax.broadcasted_iota(I32, (ne, LANES), 1)
    to_lane = lambda col: jnp.sum(jnp.where(er == ec, jnp.broadcast_to(col, (ne, LANES)), 0.0),
                                  axis=0, keepdims=True)
    n_lane = to_lane(n_col)
    off_col = jnp.sum(jnp.where(ec < er, jnp.broadcast_to(n_lane, (ne, LANES)), 0.0), axis=1, keepdims=True)
    slab_pos = cum - oh + off_col
    for k in range(TOP_K):
        pos = jnp.sum(jnp.where(sels[k], slab_pos, 0.0), axis=0, keepdims=True)
        lpos_ref[k:k + 1, :] = (pos * rt).astype(I32)
    tstart_ref[0] = to_lane(carry[:, 0:1]).astype(I32)
    tcnt_ref[0] = n_lane.astype(I32)
    carry[...] = carry[...] + n_col
    cnt_ref[...] = carry[...]


def _route(logits_t, tl, rt):
    ne, t = logits_t.shape
    nt = t // tl
    blk = lambda i: (0, i)
    per_tile = pl.BlockSpec((1, 1, LANES), lambda i: (i, 0, 0))
    return pl.pallas_call(
        functools.partial(_route_kernel, rt=rt),
        grid=(nt,),
        in_specs=[pl.BlockSpec((ne, tl), blk)],
        out_specs=[pl.BlockSpec((TOP_K, tl), blk), pl.BlockSpec((TOP_K, tl), blk), per_tile, per_tile,
                   pl.BlockSpec((ne, LANES), lambda i: (0, 0))],
        out_shape=[jax.ShapeDtypeStruct((TOP_K, t), F32), jax.ShapeDtypeStruct((TOP_K, t), I32),
                   jax.ShapeDtypeStruct((nt, 1, LANES), I32), jax.ShapeDtypeStruct((nt, 1, LANES), I32),
                   jax.ShapeDtypeStruct((ne, LANES), F32)],
        scratch_shapes=[pltpu.VMEM((ne, LANES), F32)],
        compiler_params=_cparams(("arbitrary",)),
        name="moe_route",
    )(logits_t)


def _slots_kernel(cnt_ref, tstart_ref, run_ref, blk_ref, meta_ref, *, rows_per_block):
    ne = cnt_ref.shape[0]
    shift = _log2(rows_per_block)
    cnt = cnt_ref[...].astype(I32)
    padded = ((cnt + (rows_per_block - 1)) >> shift) << shift
    r = lax.broadcasted_iota(I32, (ne, LANES), 0)
    c = lax.broadcasted_iota(I32, (ne, LANES), 1)
    padded_lane = jnp.sum(jnp.where(r == c, padded, 0), axis=0, keepdims=True)
    cnt_lane = jnp.sum(jnp.where(r == c, cnt, 0), axis=0, keepdims=True)
    pend_lane = jnp.sum(jnp.where(r <= c, padded, 0), axis=0, keepdims=True)
    pend_col = jnp.sum(jnp.where(c <= r, jnp.broadcast_to(padded_lane, (ne, LANES)), 0),
                       axis=1, keepdims=True)
    run_ref[...] = tstart_ref[...] + (pend_lane - padded_lane)
    nbp = blk_ref.shape[1]
    j0 = lax.broadcasted_iota(I32, (ne, nbp), 1) * rows_per_block
    be = jnp.sum(jnp.where(jnp.broadcast_to(pend_col, (ne, nbp)) <= j0, 1, 0), axis=0, keepdims=True)
    blk_ref[...] = jnp.minimum(be, ne - 1)
    total = jnp.max(pend_col, axis=0, keepdims=True)
    meta_ref[0:1, :] = pend_lane - padded_lane + cnt_lane
    meta_ref[1:2, :] = padded_lane - cnt_lane
    meta_ref[2:3, :] = jnp.broadcast_to(total >> shift, (1, LANES))
    meta_ref[3:8, :] = jnp.zeros((5, LANES), I32)


def _slots(cnt, tstart, n_blocks, rows_per_block):
    nbp = -(-n_blocks // LANES) * LANES
    return pl.pallas_call(
        functools.partial(_slots_kernel, rows_per_block=rows_per_block),
        out_shape=[jax.ShapeDtypeStruct(tstart.shape, I32), jax.ShapeDtypeStruct((1, nbp), I32),
                   jax.ShapeDtypeStruct((8, LANES), I32)],
        compiler_params=pltpu.CompilerParams(vmem_limit_bytes=V7X_VMEM_LIMIT),
        name="moe_slots",
    )(cnt, tstart)


def _pad_chunks(rows_per_block):
    sizes, s = [], rows_per_block // 2
    while s >= 1:
        sizes.append(s)
        s //= 2
    return sizes


def _rows(start, size, rt):
    return pl.ds(pl.multiple_of(start * rt, rt), size * rt)


def _for_each_run_chunk(run_ref, cnt_ref, tile, ne, max_rows, act):
    sizes = _pad_chunks(2 * max_rows)

    def each(e, off):
        left = cnt_ref[tile * ne + e]
        pos, start = off, run_ref[tile * ne + e]
        for size in sizes:
            hit = (left & size) != 0

            @pl.when(hit)
            def _():
                act(pos, start, size)

            inc = jnp.where(hit, size, 0)
            pos, start = pos + inc, start + inc
        return off + left

    lax.fori_loop(0, ne, each, 0)


def _dispatch_kernel(run_ref, cnt_ref, padlo_ref, npad_ref, nused_ref, lpos_ref, h_ref, buf_ref,
                     slab, zeros, sems, zsem, *, rows_per_block, rt, n_tiles):
    tl = lpos_ref.shape[1]
    ne = padlo_ref.shape[0]
    sizes = _pad_chunks(rows_per_block)
    half = rows_per_block // 2
    i = pl.program_id(0)
    slot = i & 1

    def pad_copy(start, size):
        return pltpu.make_async_copy(zeros.at[_rows(0, size, rt)], buf_ref.at[_rows(start, size, rt)], zsem)

    def start_runs(tile, sl):
        def act(pos, start, size):
            pltpu.make_async_copy(slab.at[sl, _rows(pos, size, rt)], buf_ref.at[_rows(start, size, rt)],
                                  sems.at[sl]).start()

        _for_each_run_chunk(run_ref, cnt_ref, tile, ne, tl, act)

    def wait_runs(sl):
        pltpu.make_async_copy(slab.at[sl], buf_ref.at[_rows(0, TOP_K * tl, rt)], sems.at[sl]).wait()

    @pl.when(i == 0)
    def _():
        zeros[...] = jnp.zeros_like(zeros)

        first, last = 2 * nused_ref[0], buf_ref.shape[0] // (half * rt)
        lax.fori_loop(first, last, lambda j, c: (pad_copy(j * half, half).start(), c)[1], 0)
        lax.fori_loop(first, last, lambda j, c: (pad_copy(j * half, half).wait(), c)[1], 0)

        def each(e, wait):
            start = padlo_ref[e]
            left = npad_ref[e]
            for size in sizes:
                hit = (left & size) != 0

                @pl.when(hit)
                def _():
                    cp = pad_copy(start, size)
                    if wait:
                        cp.wait()
                    else:
                        cp.start()

                start = start + jnp.where(hit, size, 0)

        lax.fori_loop(0, ne, lambda e, c: (each(e, False), c)[1], 0)
        lax.fori_loop(0, ne, lambda e, c: (each(e, True), c)[1], 0)

    def step(sl):
        @pl.when(i >= 2)
        def _():
            wait_runs(sl)

        def fill(t, c):
            row = h_ref[_rows(t, 1, rt), :]
            for k in range(TOP_K):
                slab[sl, pl.ds(pl.multiple_of(lpos_ref[k, t], rt), rt), :] = row
            return c

        lax.fori_loop(0, tl, fill, 0, unroll=8)
        start_runs(i, sl)

        @pl.when(i == n_tiles - 1)
        def _():
            if n_tiles >= 2:
                wait_runs(1 - sl)
            wait_runs(sl)

    for sl in range(2):
        pl.when(slot == sl)(functools.partial(step, sl))


def _dispatch(h2r, lpos, run_start, run_cnt, pad_lo, n_pad, n_used, n_rows, rows_per_block, rt, tl):
    n_tiles = lpos.shape[1] // tl
    return pl.pallas_call(
        functools.partial(_dispatch_kernel, rows_per_block=rows_per_block, rt=rt, n_tiles=n_tiles),
        grid_spec=pltpu.PrefetchScalarGridSpec(
            num_scalar_prefetch=5,
            grid=(n_tiles,),
            in_specs=[pl.BlockSpec((TOP_K, tl), lambda i, *_: (0, i), memory_space=pltpu.SMEM),
                      pl.BlockSpec((tl * rt, LANES), lambda i, *_: (i, 0))],
            out_specs=pl.BlockSpec(memory_space=pl.ANY),
            scratch_shapes=[pltpu.VMEM((2, TOP_K * tl * rt, LANES), F32),
                            pltpu.VMEM((rows_per_block // 2 * rt, LANES), F32),
                            pltpu.SemaphoreType.DMA((2,)), pltpu.SemaphoreType.DMA(())]),
        out_shape=jax.ShapeDtypeStruct((n_rows * rt, LANES), F32),
        compiler_params=_cparams(("arbitrary",)),
        name="moe_dispatch",
    )(run_start, run_cnt, pad_lo, n_pad, n_used, lpos, h2r)


def _expert_kernel(be_ref, nu_ref, x_ref, w1_ref, b1_ref, w2_ref, b2_ref, o_ref, w1b, w2b):
    j = pl.program_id(0)
    active = j < nu_ref[0]

    @pl.when(active & ((j == 0) | (be_ref[j] != be_ref[jnp.maximum(j - 1, 0)])))
    def _():
        w1b[...] = w1_ref[0].astype(BF16)
        w2b[...] = w2_ref[0].astype(BF16)

    @pl.when(active)
    def _():
        de = w2b.shape[0]
        gl = _dot(_load_row_tiles(x_ref, w1b.shape[0] // LANES).astype(BF16), w1b[...]) + b1_ref[0]
        g = jnp.minimum(gl[:, :de], SWIGLU_LIMIT)
        lin = jnp.clip(gl[:, de:], -SWIGLU_LIMIT, SWIGLU_LIMIT)
        glu = g * jax.nn.sigmoid(SWIGLU_ALPHA * g)
        _store_row_tiles(o_ref, _dot(((lin + 1.0) * glu).astype(BF16), w2b[...]) + b2_ref[0])

    @pl.when(pl.program_id(0) >= nu_ref[0])
    def _():
        o_ref[...] = jnp.zeros_like(o_ref)


def _experts(buf, block_e, n_used, w1, b1, w2, b2, rows_per_block):
    ne, d, d2 = w1.shape
    de = w2.shape[1]
    blk_shape = (rows_per_block * (d // LANES), LANES)
    nb = buf.shape[0] // blk_shape[0]
    rowblk = lambda j, be, nu: (jnp.minimum(j, nu[0] - 1), 0)
    by_e = lambda j, be, nu: (be[j], 0, 0)
    return pl.pallas_call(
        _expert_kernel,
        grid_spec=pltpu.PrefetchScalarGridSpec(
            num_scalar_prefetch=2,
            grid=(nb,),
            in_specs=[pl.BlockSpec(blk_shape, rowblk),
                      pl.BlockSpec((1, d, d2), by_e), pl.BlockSpec((1, 1, d2), by_e),
                      pl.BlockSpec((1, de, d), by_e), pl.BlockSpec((1, 1, d), by_e)],
            out_specs=pl.BlockSpec(blk_shape, lambda j, be, nu: (j, 0)),
            scratch_shapes=[pltpu.VMEM((d, d2), BF16), pltpu.VMEM((de, d), BF16)]),
        out_shape=jax.ShapeDtypeStruct(buf.shape, F32),
        compiler_params=_cparams(("arbitrary",)),
        name="moe_experts",
    )(block_e, n_used, buf, w1, b1.reshape(ne, 1, d2), w2, b2.reshape(ne, 1, d))


def _combine_kernel(run_ref, cnt_ref, lpos_ref, gate_ref, xn_ref, g2_ref, ob_ref, o_ref, slab, acc, sems,
                    *, rt, ne, n_tiles):
    tl = lpos_ref.shape[1]
    i = pl.program_id(0) * pl.num_programs(1) + pl.program_id(1)
    slot = i & 1

    def start_runs(tile, sl):
        def act(pos, start, size):
            pltpu.make_async_copy(ob_ref.at[_rows(start, size, rt)], slab.at[sl, _rows(pos, size, rt)],
                                  sems.at[sl]).start()

        _for_each_run_chunk(run_ref, cnt_ref, tile, ne, tl, act)

    def step(sl):
        @pl.when(i == 0)
        def _():
            start_runs(i, sl)

        @pl.when(i + 1 < n_tiles)
        def _():
            start_runs(i + 1, 1 - sl)

        pltpu.make_async_copy(ob_ref.at[_rows(0, TOP_K * tl, rt)], slab.at[sl], sems.at[sl]).wait()

        def row(k, t):
            return slab[sl, pl.ds(pl.multiple_of(lpos_ref[k, t], rt), rt), :]

        def token(t, c):
            a = gate_ref[0, t] * row(0, t)
            for k in range(1, TOP_K):
                a = a + gate_ref[k, t] * row(k, t)
            acc[_rows(t, 1, rt), :] = a
            return c

        lax.fori_loop(0, tl, token, 0, unroll=8)
        o_ref[0] = xn_ref[0] + g2_ref[0] * _load_row_tiles(acc, rt)

    for sl in range(2):
        pl.when(slot == sl)(functools.partial(step, sl))


def _combine(out_buf, lpos, gates, run_start, run_cnt, xn, g2, rt, tl):
    b, s, d = xn.shape
    nt = s // tl
    tok = lambda bi, i, *_: (0, bi * nt + i)
    return pl.pallas_call(
        functools.partial(_combine_kernel, rt=rt, ne=N_EXPERTS, n_tiles=b * nt),
        grid_spec=pltpu.PrefetchScalarGridSpec(
            num_scalar_prefetch=2,
            grid=(b, nt),
            in_specs=[pl.BlockSpec((TOP_K, tl), tok, memory_space=pltpu.SMEM),
                      pl.BlockSpec((TOP_K, tl), tok, memory_space=pltpu.SMEM),
                      pl.BlockSpec((1, tl, d), lambda bi, i, *_: (bi, i, 0)),
                      pl.BlockSpec((1, 1, d), lambda bi, i, *_: (bi, 0, 0)),
                      pl.BlockSpec(memory_space=pl.ANY)],
            out_specs=pl.BlockSpec((1, tl, d), lambda bi, i, *_: (bi, i, 0)),
            scratch_shapes=[pltpu.VMEM((2, TOP_K * tl * rt, LANES), F32), pltpu.VMEM((tl * rt, LANES), F32),
                            pltpu.SemaphoreType.DMA((2,))]),
        out_shape=jax.ShapeDtypeStruct((b, s, d), F32),
        compiler_params=_cparams(("arbitrary", "arbitrary")),
        name="moe_combine",
    )(run_start, run_cnt, lpos, gates, xn, g2, out_buf)


def _moe(h2r, logits_t, xn, g2, w1, b1, w2, b2):
    b, s, d = xn.shape
    t = b * s
    rt = d // LANES
    tl = min(TOKEN_TILE, s)
    assert s % tl == 0
    n_blocks = (t * TOP_K) // EXPERT_ROWS + N_EXPERTS
    n_rows = n_blocks * EXPERT_ROWS
    gates, lpos, tstart, tcnt, cnt = _route(logits_t, tl, rt)
    run, blk, meta = _slots(cnt, tstart, n_blocks, EXPERT_ROWS)
    run_start = run[:, 0, :N_EXPERTS].reshape(-1)
    run_cnt = tcnt[:, 0, :N_EXPERTS].reshape(-1)
    buf = _dispatch(h2r, lpos, run_start, run_cnt, meta[0, :N_EXPERTS], meta[1, :N_EXPERTS], meta[2, :1],
                    n_rows, EXPERT_ROWS, rt, tl)
    out_buf = _experts(buf, blk[0, :n_blocks], meta[2, :1], w1, b1, w2, b2, EXPERT_ROWS)
    return _combine(out_buf, lpos, gates, run_start, run_cnt, xn, g2, rt, tl)


def _layer(x, ctx, c, c_ctx, p, lam_init):
    b, s, d = x.shape
    attn_w = d // 2
    hw = d - attn_w
    v_dim = attn_w // N_HEADS
    qk_dim = v_dim // 2
    qk_cols = N_HEADS * 2 * qk_dim
    v_cols = N_HEADS * v_dim
    assert 2 * qk_dim == LANES and v_dim == LANES and s % GRID_W == 0

    rows = -(-(b + 1) // 8) * 8
    cc = jnp.zeros((rows, d), F32).at[:b].set(c).at[b].set(c_ctx)
    mod = _modulation(cc, p['w_mod'], p['b_mod'])
    mod_x = mod[:b].reshape(b, N_MOD, 1, d)
    sh1, sc1, g1, sh2, sc2, g2 = [mod_x[:, i] for i in range(N_MOD)]
    mod_c = mod[b:b + 1].reshape(1, N_MOD, 1, d)
    csh1, csc1 = mod_c[:, 0], mod_c[:, 1]

    w_in_bf = p['w_in'].astype(BF16)
    qg = jnp.tile(p['q_norm_g'], qk_cols // qk_dim).reshape(1, qk_cols)
    kg = jnp.tile(p['k_norm_g'], qk_cols // qk_dim).reshape(1, qk_cols)
    n1g = p['norm1_g'].reshape(1, d)
    cos_t, sin_t = _rope_tables(s, qk_dim)
    q, k, v, u_hy = _project_latent(x, sh1, sc1, n1g, w_in_bf, qg, kg, cos_t, sin_t, qk_cols, v_cols, qk_dim)
    k_c, v_c = _project_context(ctx, csh1, csc1, n1g, w_in_bf[:, qk_cols:2 * qk_cols + v_cols], kg,
                                qk_cols, v_cols, qk_dim)
    vec = lambda a: a.reshape(1, qk_dim)
    attn = _diff_attention(q, k_c, k, v_c, v, vec(p['lam_q1']), vec(p['lam_k1']), vec(p['lam_q2']),
                           vec(p['lam_k2']), p['subln_g'].reshape(1, v_dim), lam_init, qk_dim)

    hfilt = _hyena_filters(s, hw, p['hy_w1'], p['hy_b1'], p['hy_f1'], p['hy_w2'], p['hy_b2'], p['hy_f2'], p['hy_w3'])
    mf, mi = _dft_matrices(s)
    g_spec = _filter_spectra(mf, hfilt, hw)
    uc = _short_conv(u_hy, p['hy_conv_w'], p['hy_conv_b'])
    y1 = _fwd_dft(mf, uc, 0, g_spec, 0, hw)
    z1 = _inv_dft(mi, y1, uc, 0, uc, 1, p['hy_skip'], 0, None, hw)
    y2 = _fwd_dft(mf, z1, 0, g_spec, 1, hw)
    hyn = _inv_dft(mi, y2, z1, 0, uc, 2, p['hy_skip'], 1, p['hy_out_g'], hw)

    xn, h2, logits_t = _out_project(attn, hyn, x, p['w_out'].astype(BF16), g1, sh2, sc2,
                                    p['norm2_g'].reshape(1, d), p['router_w'].T,
                                    p['router_b'].reshape(N_EXPERTS, 1))
    return _moe(h2, logits_t, xn, g2, p['exp_w1'], p['exp_b1'], p['exp_w2'], p['exp_b2'])


def kernel(x, c, ctx, c_ctx, w_mod, b_mod, norm1_g, norm2_g, w_in, q_norm_g, k_norm_g, lam_q1, lam_k1, lam_q2, lam_k2, subln_g, hy_conv_w, hy_conv_b, hy_w1, hy_b1, hy_f1, hy_w2, hy_b2, hy_f2, hy_w3, hy_skip, hy_out_g, w_out, router_w, router_b, exp_w1, exp_b1, exp_w2, exp_b2):
    depth = w_mod.shape[0]
    assert depth == 1, "context-token update between layers is not implemented"
    p = {
        'w_mod': w_mod[0], 'b_mod': b_mod[0], 'norm1_g': norm1_g[0], 'norm2_g': norm2_g[0],
        'w_in': w_in[0], 'q_norm_g': q_norm_g[0], 'k_norm_g': k_norm_g[0],
        'lam_q1': lam_q1[0], 'lam_k1': lam_k1[0], 'lam_q2': lam_q2[0], 'lam_k2': lam_k2[0],
        'subln_g': subln_g[0], 'hy_conv_w': hy_conv_w[0], 'hy_conv_b': hy_conv_b[0],
        'hy_w1': hy_w1[0], 'hy_b1': hy_b1[0], 'hy_f1': hy_f1[0], 'hy_w2': hy_w2[0],
        'hy_b2': hy_b2[0], 'hy_f2': hy_f2[0], 'hy_w3': hy_w3[0], 'hy_skip': hy_skip[0],
        'hy_out_g': hy_out_g[0], 'w_out': w_out[0], 'router_w': router_w[0],
        'router_b': router_b[0], 'exp_w1': exp_w1[0], 'exp_b1': exp_b1[0],
        'exp_w2': exp_w2[0], 'exp_b2': exp_b2[0],
    }
    lam_init = 0.8 - 0.6 * math.exp(-0.3 * 0)
    return _layer(x, ctx, c, c_ctx, p, lam_init)
```

```python
import functools
import math

import jax
import jax.numpy as jnp
from jax import lax
from jax.experimental import pallas as pl
from jax.experimental.pallas import tpu as pltpu

F32 = jnp.float32
BF16 = jnp.bfloat16
I32 = jnp.int32

GRID_W = 64
N_HEADS = 4
N_MOD = 6
SHORT_CONV = 3
HYENA_ORDER = 2
N_BANDS = 8
FEAT_DIM = 1 + 2 * N_BANDS
FILTER_HIDDEN = 64
DECAY_TARGET = 1e-2
FAST_DECAY_PCT = 0.3
SLOW_DECAY_PCT = 1.5
N_EXPERTS = 32
TOP_K = 4
SWIGLU_LIMIT = 7.0
SWIGLU_ALPHA = 1.702
ROPE_BASE = 10000.0
EPS = 1e-6

LANES = 128
V7X_VMEM_LIMIT = 56 * 1024 * 1024

ROW_TILE = 512
ATT_Q_TILE = 256
ATT_KEY_CHUNKS = 17
DFT_TILE = 256
RADIX = 4
EXPERT_ROWS = 512
TOKEN_TILE = 512


def _log2(n):
    assert n > 0 and n & (n - 1) == 0, f"{n} must be a power of two"
    return n.bit_length() - 1


def _cparams(sem, vmem=V7X_VMEM_LIMIT):
    return pltpu.CompilerParams(dimension_semantics=sem, vmem_limit_bytes=vmem)


def _split_bf16(a):
    hi = a.astype(BF16)
    lo = (a - hi.astype(F32)).astype(BF16)
    return hi, lo


def _dot(a, b):
    return jnp.dot(a, b, preferred_element_type=F32)


def _dot_nt(a, b):
    return lax.dot_general(a, b, (((1,), (1,)), ((), ())), preferred_element_type=F32)


def _store_row_tiles(ref, val):
    rows, d = val.shape
    rt = d // LANES
    for c in range(rt):
        ref[pl.ds(c, rows, stride=rt), :] = val[:, c * LANES:(c + 1) * LANES]


def _load_row_tiles(ref, rt):
    rows = ref.shape[0] // rt
    return jnp.concatenate([ref[pl.ds(c, rows, stride=rt), :] for c in range(rt)], axis=1)


def _dot3(a, b):
    ah, al = _split_bf16(a)
    bh, bl = _split_bf16(b)
    return _dot(ah, bh) + (_dot(ah, bl) + _dot(al, bh))


def _mod_kernel(c_ref, w_ref, b_ref, o_ref):
    c = c_ref[...]
    s = c * jax.nn.sigmoid(c)
    o_ref[...] = _dot3(s, w_ref[...]) + b_ref[...]


def _modulation(cc, w_mod, b_mod):
    rows, d = cc.shape
    n = w_mod.shape[1]
    tn = min(n, 1536)
    return pl.pallas_call(
        _mod_kernel,
        grid=(n // tn,),
        in_specs=[pl.BlockSpec((rows, d), lambda j: (0, 0)),
                  pl.BlockSpec((d, tn), lambda j: (0, j)),
                  pl.BlockSpec((1, tn), lambda j: (0, j))],
        out_specs=pl.BlockSpec((rows, tn), lambda j: (0, j)),
        out_shape=jax.ShapeDtypeStruct((rows, n), F32),
        compiler_params=_cparams(("parallel",)),
        name="modulation",
    )(cc, w_mod, b_mod.reshape(1, n))


def _rope_table_kernel(cos_ref, sin_ref, *, qk_dim):
    s, w = cos_ref.shape
    half = qk_dim // 2
    nf = half // 2
    t = lax.broadcasted_iota(I32, (s, w), 0)
    lane = lax.broadcasted_iota(I32, (s, w), 1)
    d = lane & (qk_dim - 1)
    j = d & (nf - 1)
    row = t >> _log2(GRID_W)
    col = t & (GRID_W - 1)
    pos = jnp.where(d < half, row, col).astype(F32)
    inv = jnp.exp(j.astype(F32) * (-math.log(ROPE_BASE) / nf))
    ang = pos * inv
    first = (d & (half - 1)) < nf
    cos_ref[...] = jnp.cos(ang)
    sn = jnp.sin(ang)
    sin_ref[...] = jnp.where(first, -sn, sn)


def _rope_tables(s, qk_dim):
    return pl.pallas_call(
        functools.partial(_rope_table_kernel, qk_dim=qk_dim),
        out_shape=(jax.ShapeDtypeStruct((s, LANES), F32), jax.ShapeDtypeStruct((s, LANES), F32)),
        name="rope_tables",
    )()


def _group_rms(t, gain, qk_dim):
    w = t.shape[1]
    r = lax.broadcasted_iota(I32, (w, w), 0) >> _log2(qk_dim)
    c = lax.broadcasted_iota(I32, (w, w), 1) >> _log2(qk_dim)
    bd = jnp.where(r == c, 1.0 / qk_dim, 0.0).astype(BF16)
    hi, lo = _split_bf16(t * t)
    ms = _dot(hi, bd) + _dot(lo, bd)
    return t * lax.rsqrt(ms + EPS) * gain


def _rope(t, cos, sin_signed, qk_dim):
    w = t.shape[1]
    nf = qk_dim // 4
    lane = lax.broadcasted_iota(I32, t.shape, 1)
    first = (lane & (2 * nf - 1)) < nf
    partner = jnp.where(first, pltpu.roll(t, w - nf, axis=1), pltpu.roll(t, nf, axis=1))
    return t * cos + partner * sin_signed


def _proj_kernel(*refs, latent, qk_cols, v_cols, qk_dim):
    if latent:
        (x_ref, sh_ref, sc_ref, g_ref, w_ref, qg_ref, kg_ref, cos_ref, sin_ref,
         q_out, k_out, v_out, u_out) = refs
    else:
        x_ref, sh_ref, sc_ref, g_ref, w_ref, kg_ref, k_out, v_out = refs
    x = x_ref[0]
    ms = jnp.mean(x * x, axis=-1, keepdims=True)
    h = (x * lax.rsqrt(ms + EPS) * g_ref[...]) * (1.0 + sc_ref[0]) + sh_ref[0]
    proj = _dot(h.astype(BF16), w_ref[...])
    if latent:
        reps = qk_cols // LANES
        cos = jnp.concatenate([cos_ref[...]] * reps, axis=1)
        sin = jnp.concatenate([sin_ref[...]] * reps, axis=1)
        q = _rope(_group_rms(proj[:, :qk_cols], qg_ref[...], qk_dim), cos, sin, qk_dim)
        q_out[0] = (q * (qk_dim ** -0.5 * math.log2(math.e))).astype(BF16)
        k = _rope(_group_rms(proj[:, qk_cols:2 * qk_cols], kg_ref[...], qk_dim), cos, sin, qk_dim)
        k_out[0] = k.astype(BF16)
        v_out[0] = proj[:, 2 * qk_cols:2 * qk_cols + v_cols].astype(BF16)
        u_out[0] = proj[:, 2 * qk_cols + v_cols:].astype(BF16)
    else:
        k = _group_rms(proj[:, :qk_cols], kg_ref[...], qk_dim)
        k_out[0] = k.astype(BF16)
        v_out[0] = proj[:, qk_cols:qk_cols + v_cols].astype(BF16)


def _project_latent(x, sh, sc, g, w_bf, qg, kg, cos_t, sin_t, qk_cols, v_cols, qk_dim):
    b, s, d = x.shape
    n = w_bf.shape[1]
    hy_cols = n - 2 * qk_cols - v_cols
    tm = min(ROW_TILE, s)
    row = lambda bi, i: (bi, i, 0)
    per_b = lambda bi, i: (bi, 0, 0)
    const = lambda bi, i: (0, 0)
    return pl.pallas_call(
        functools.partial(_proj_kernel, latent=True, qk_cols=qk_cols, v_cols=v_cols, qk_dim=qk_dim),
        grid=(b, s // tm),
        in_specs=[pl.BlockSpec((1, tm, d), row),
                  pl.BlockSpec((1, 1, d), per_b), pl.BlockSpec((1, 1, d), per_b),
                  pl.BlockSpec((1, d), const), pl.BlockSpec((d, n), const),
                  pl.BlockSpec((1, qk_cols), const), pl.BlockSpec((1, qk_cols), const),
                  pl.BlockSpec((tm, LANES), lambda bi, i: (i, 0)),
                  pl.BlockSpec((tm, LANES), lambda bi, i: (i, 0))],
        out_specs=[pl.BlockSpec((1, tm, qk_cols), row), pl.BlockSpec((1, tm, qk_cols), row),
                   pl.BlockSpec((1, tm, v_cols), row), pl.BlockSpec((1, tm, hy_cols), row)],
        out_shape=[jax.ShapeDtypeStruct((b, s, qk_cols), BF16), jax.ShapeDtypeStruct((b, s, qk_cols), BF16),
                   jax.ShapeDtypeStruct((b, s, v_cols), BF16), jax.ShapeDtypeStruct((b, s, hy_cols), BF16)],
        compiler_params=_cparams(("parallel", "parallel")),
        name="project_latent",
    )(x, sh, sc, g, w_bf, qg, kg, cos_t, sin_t)


def _project_context(ctx, sh, sc, g, w_bf, kg, qk_cols, v_cols, qk_dim):
    b, lc, d = ctx.shape
    n = w_bf.shape[1]
    tm = min(ROW_TILE, lc)
    row = lambda bi, i: (bi, i, 0)
    shared = lambda bi, i: (0, 0, 0)
    const = lambda bi, i: (0, 0)
    return pl.pallas_call(
        functools.partial(_proj_kernel, latent=False, qk_cols=qk_cols, v_cols=v_cols, qk_dim=qk_dim),
        grid=(b, lc // tm),
        in_specs=[pl.BlockSpec((1, tm, d), row),
                  pl.BlockSpec((1, 1, d), shared), pl.BlockSpec((1, 1, d), shared),
                  pl.BlockSpec((1, d), const), pl.BlockSpec((d, n), const),
                  pl.BlockSpec((1, qk_cols), const)],
        out_specs=[pl.BlockSpec((1, tm, qk_cols), row), pl.BlockSpec((1, tm, v_cols), row)],
        out_shape=[jax.ShapeDtypeStruct((b, lc, qk_cols), BF16), jax.ShapeDtypeStruct((b, lc, v_cols), BF16)],
        compiler_params=_cparams(("parallel", "parallel")),
        name="project_context",
    )(ctx, sh, sc, g, w_bf, kg)


def _key_chunks(kk, n):
    tiles = kk // LANES
    n = min(n, tiles)
    return [(LANES * (i * tiles // n), LANES * ((i + 1) * tiles // n)) for i in range(n)]


def _attn_kernel(q_ref, kc_ref, k_ref, vc_ref, v_ref, lq1, lk1, lq2, lk2, sg_ref, o_ref, kt_ref, v1_ref, *bufs,
                 lam_init, qk_dim, n_tiles):
    lam = (jnp.exp(jnp.sum(lq1[...] * lk1[...], axis=-1, keepdims=True))
           - jnp.exp(jnp.sum(lq2[...] * lk2[...], axis=-1, keepdims=True)) + lam_init)
    j = pl.program_id(2)
    even, odd = bufs[:4], bufs[4:]
    tq = q_ref.shape[1]
    chunks = _key_chunks(kt_ref.shape[1], ATT_KEY_CHUNKS)

    @pl.when(j == 0)
    def _():
        lc = kc_ref.shape[1]
        kt_ref[:, :lc] = kc_ref[0].T
        kt_ref[:, lc:] = k_ref[0].T
        v1_ref[:lc, :LANES] = vc_ref[0]
        v1_ref[lc:, :LANES] = v_ref[0]
        v1_ref[:, LANES:] = jnp.ones((v1_ref.shape[0], LANES), BF16)

    def step(cur, prev, score=True, attend=True):
        s1_w, s2_w, m1_w, m2_w = cur
        s1_r, s2_r, m1_r, m2_r = prev
        if score:
            q = q_ref[0]
            lane = lax.broadcasted_iota(I32, q.shape, 1)
            q1 = jnp.where(lane < qk_dim, q, jnp.zeros_like(q))
            q2 = jnp.where(lane >= qk_dim, q, jnp.zeros_like(q))
            m1 = m2 = jnp.full((tq, 1), -jnp.inf, F32)
        if attend:
            m1p, m2p = m1_r[:, :1], m2_r[:, :1]
            o1 = o2 = jnp.zeros((tq, 2 * LANES), F32)
        for c0, c1 in chunks:
            if score:
                s1 = _dot(q1, kt_ref[:, c0:c1])
                s2 = _dot(q2, kt_ref[:, c0:c1])
                s1_w[:, c0:c1] = s1
                s2_w[:, c0:c1] = s2
                m1 = jnp.maximum(m1, jnp.max(s1, axis=-1, keepdims=True))
                m2 = jnp.maximum(m2, jnp.max(s2, axis=-1, keepdims=True))
            if attend:
                e1 = jnp.exp2(s1_r[:, c0:c1] - m1p)
                e2 = jnp.exp2(s2_r[:, c0:c1] - m2p)
                o1 = o1 + _dot(e1.astype(BF16), v1_ref[c0:c1, :])
                o2 = o2 + _dot(e2.astype(BF16), v1_ref[c0:c1, :])
        if score:
            m1_w[...] = jnp.broadcast_to(m1, m1_w.shape)
            m2_w[...] = jnp.broadcast_to(m2, m2_w.shape)
        if attend:
            o = o1[:, :LANES] / o1[:, LANES:] - o2[:, :LANES] * (lam / o2[:, LANES:])
            ms = jnp.mean(o * o, axis=-1, keepdims=True)
            o_ref[0] = ((o * lax.rsqrt(ms + EPS) * sg_ref[...]) * (1.0 - lam_init)).astype(BF16)

    by_parity = (lambda **kw: step(even, odd, **kw)), (lambda **kw: step(odd, even, **kw))

    @pl.when(j == 0)
    def _():
        by_parity[0](attend=False)

    for parity in range(2):
        @pl.when((j > 0) & (j < n_tiles) & ((j & 1) == parity))
        def _():
            by_parity[parity]()

    @pl.when(j == n_tiles)
    def _():
        by_parity[n_tiles % 2](score=False)


def _diff_attention(q, k_c, k, v_c, v, lq1, lk1, lq2, lk2, subln_g, lam_init, qk_dim):
    b, s, w = q.shape
    lc = k_c.shape[1]
    assert lc % LANES == 0
    kk = lc + s
    tq = min(ATT_Q_TILE, s)
    nq = s // tq
    kv = lambda bi, h, i: (bi, 0, h)
    const = lambda bi, h, i: (0, 0)
    vec = pl.BlockSpec((1, qk_dim), const)
    wide = pltpu.VMEM((tq, kk), F32)
    stat = pltpu.VMEM((tq, LANES), F32)
    per_parity = [wide, wide, stat, stat]
    return pl.pallas_call(
        functools.partial(_attn_kernel, lam_init=lam_init, qk_dim=qk_dim, n_tiles=nq),
        grid=(b, N_HEADS, nq + 1),
        in_specs=[pl.BlockSpec((1, tq, LANES), lambda bi, h, i: (bi, jnp.minimum(i, nq - 1), h)),
                  pl.BlockSpec((1, lc, LANES), kv), pl.BlockSpec((1, s, LANES), kv),
                  pl.BlockSpec((1, lc, LANES), kv), pl.BlockSpec((1, s, LANES), kv), vec, vec, vec, vec,
                  pl.BlockSpec((1, LANES), const)],
        out_specs=pl.BlockSpec((1, tq, LANES), lambda bi, h, i: (bi, jnp.maximum(i - 1, 0), h)),
        out_shape=jax.ShapeDtypeStruct((b, s, w), BF16),
        scratch_shapes=[pltpu.VMEM((LANES, kk), BF16), pltpu.VMEM((kk, 2 * LANES), BF16)] + per_parity + per_parity,
        compiler_params=_cparams(("parallel", "parallel", "arbitrary")),
        name="diff_attention",
    )(q, k_c, k, v_c, v, lq1, lk1, lq2, lk2, subln_g)


def _filter_kernel(w1_ref, b1_ref, f1_ref, w2_ref, b2_ref, f2_ref, w3_ref, o_ref, *, seq, hw):
    tl, n = o_ref.shape
    base = pl.program_id(0) * tl
    quarter = seq // RADIX

    def position(shape):
        p = lax.broadcasted_iota(I32, shape, 0) + base
        return (((p & (quarter - 1)) << _log2(RADIX)) | (p >> _log2(quarter))).astype(F32)

    pos = position((tl, LANES))
    lane = lax.broadcasted_iota(I32, (tl, LANES), 1)
    tn = pos / seq
    band_idx = jnp.where(lane <= N_BANDS, lane - 1, lane - 1 - N_BANDS).astype(F32)
    band = 1e-4 + band_idx * ((N_BANDS - 1 - 1e-4) / (N_BANDS - 1))
    ang = (2.0 * math.pi / seq) * pos * band
    feats = jnp.where(lane == 0, tn,
                      jnp.where(lane <= N_BANDS, jnp.sin(ang),
                                jnp.where(lane < FEAT_DIM, jnp.cos(ang), 0.0)))
    h = jnp.sin(f1_ref[...] * (_dot3(feats, w1_ref[...]) + b1_ref[...]))
    h = jnp.sin(f2_ref[...] * (_dot3(h, w2_ref[...]) + b2_ref[...]))
    h = _dot3(h, w3_ref[...])
    ch = (lax.broadcasted_iota(I32, (tl, n), 1) & ((1 << _log2(hw)) - 1)).astype(F32)
    lo = abs(math.log(DECAY_TARGET) / SLOW_DECAY_PCT)
    hi = abs(math.log(DECAY_TARGET) / FAST_DECAY_PCT)
    delta = lo + ch * ((hi - lo) / (hw - 1))
    o_ref[...] = (h * jnp.exp(-(position((tl, n)) / seq) * delta)).astype(BF16)


def _hyena_filters(seq, hw, w1, b1, f1, w2, b2, f2, w3):
    fh = w2.shape[0]
    n = w3.shape[1]
    w1p = jnp.zeros((LANES, fh), F32).at[:FEAT_DIM].set(w1)
    tl = min(ROW_TILE, seq)
    const = lambda i: (0, 0)
    return pl.pallas_call(
        functools.partial(_filter_kernel, seq=seq, hw=hw),
        grid=(seq // tl,),
        in_specs=[pl.BlockSpec((LANES, fh), const), pl.BlockSpec((1, fh), const), pl.BlockSpec((1, fh), const),
                  pl.BlockSpec((fh, fh), const), pl.BlockSpec((1, fh), const), pl.BlockSpec((1, fh), const),
                  pl.BlockSpec((fh, n), const)],
        out_specs=pl.BlockSpec((tl, n), lambda i: (i, 0)),
        out_shape=jax.ShapeDtypeStruct((seq, n), BF16),
        compiler_params=_cparams(("parallel",)),
        name="hyena_filters",
    )(w1p, b1.reshape(1, fh), f1.reshape(1, fh), w2, b2.reshape(1, fh), f2.reshape(1, fh), w3)


def _dft_kernel(mf_ref, mi_ref, tfc, tfs, tic, tis, *, seq):
    rows, q = tic.shape
    mask = (1 << _log2(4 * seq)) - 1
    unit = math.pi / (2 * seq)
    i_row = lax.broadcasted_iota(I32, (rows, q), 0)
    col = lax.broadcasted_iota(I32, (rows, q), 1)

    @pl.when(pl.program_id(0) == 0)
    def _():
        for r in range(RADIX):
            af = ((2 * i_row * (RADIX * col + r)) & mask).astype(F32) * unit
            tfc[r] = jnp.cos(af)
            tfs[r] = jnp.sin(af)
        ai = (((2 * col + 1) * (RADIX * i_row)) & mask).astype(F32) * unit
        tic[...] = jnp.cos(ai)
        tis[...] = jnp.sin(ai)

    r0 = pl.program_id(0) * rows
    c1 = lax.broadcasted_iota(I32, (1, q), 1)
    for r in range(RADIX):
        bf = (((2 * r0 + 1) * (RADIX * c1 + r)) & mask).astype(F32) * unit
        bi = (((2 * c1 + 1) * (RADIX * r0 + r)) & mask).astype(F32) * unit
        cbf, sbf = jnp.cos(bf), jnp.sin(bf)
        cbi, sbi = jnp.cos(bi), jnp.sin(bi)
        mf_ref[r, 0] = (cbf * tfc[r] - sbf * tfs[r]).astype(BF16)
        mf_ref[r, 1] = (sbf * tfc[r] + cbf * tfs[r]).astype(BF16)
        mi_ref[r, :, :q] = (cbi * tic[...] - sbi * tis[...]).astype(BF16)
        mi_ref[r, :, q:] = (sbi * tic[...] + cbi * tis[...]).astype(BF16)


def _dft_matrices(seq):
    q = seq // RADIX
    rows = min(DFT_TILE, q)
    return pl.pallas_call(
        functools.partial(_dft_kernel, seq=seq),
        grid=(q // rows,),
        out_specs=[pl.BlockSpec((RADIX, 2, rows, q), lambda i: (0, 0, i, 0)),
                   pl.BlockSpec((RADIX, rows, 2 * q), lambda i: (0, i, 0))],
        out_shape=[jax.ShapeDtypeStruct((RADIX, 2, q, q), BF16), jax.ShapeDtypeStruct((RADIX, q, 2 * q), BF16)],
        scratch_shapes=[pltpu.VMEM((RADIX, rows, q), F32), pltpu.VMEM((RADIX, rows, q), F32),
                        pltpu.VMEM((rows, q), F32), pltpu.VMEM((rows, q), F32)],
        compiler_params=_cparams(("arbitrary",)),
        name="dft_matrices",
    )()


def _class_transform(mf_ref, x_of_class):
    tc, ts = [], []
    for r in range(RADIX):
        x = x_of_class(r)
        tc.append(_dot(mf_ref[r, 0], x))
        ts.append(_dot(mf_ref[r, 1], x))
    return [(tc[0] + tc[1] + tc[2] + tc[3], ts[0] + ts[1] + ts[2] + ts[3]),
            (tc[0] - tc[1] + tc[2] - tc[3], ts[1] - ts[0] + ts[3] - ts[2]),
            (tc[0] - ts[1] - tc[2] + ts[3], ts[0] + tc[1] - ts[2] - tc[3]),
            (tc[0] + ts[1] - tc[2] - ts[3], tc[1] - ts[0] + ts[2] - tc[3])]


def _spectrum_kernel(mf_ref, h_ref, g_ref, *, seq, hw):
    q = seq // RADIX
    groups = _class_transform(mf_ref, lambda r: h_ref[r * q:(r + 1) * q, :])
    scale = 1.0 / seq
    for x, (hc, hs) in enumerate(groups):
        g_ref[0, x, 0] = ((hc[:, :hw] + hc[:, hw:]) * scale).astype(g_ref.dtype)
        g_ref[0, x, 1] = ((hs[:, :hw] - hs[:, hw:]) * scale).astype(g_ref.dtype)


def _filter_spectra(mf, hfilt, hw):
    q = mf.shape[2]
    seq = q * RADIX
    rows = min(DFT_TILE, q)
    return pl.pallas_call(
        functools.partial(_spectrum_kernel, seq=seq, hw=hw),
        grid=(HYENA_ORDER, q // rows),
        in_specs=[pl.BlockSpec((RADIX, 2, rows, q), lambda n, i: (0, 0, i, 0)),
                  pl.BlockSpec((seq, 2 * hw), lambda n, i: (0, n))],
        out_specs=pl.BlockSpec((1, RADIX, 2, rows, hw), lambda n, i: (n, 0, 0, i, 0)),
        out_shape=jax.ShapeDtypeStruct((HYENA_ORDER, RADIX, 2, q, hw), BF16),
        compiler_params=_cparams(("parallel", "parallel")),
        name="filter_spectra",
    )(mf, hfilt)


def _short_conv_kernel(u_ref, w_ref, b_ref, o_ref, y_ref):
    s, tc = u_ref.shape[1], u_ref.shape[2]
    q = s // RADIX
    t = lax.broadcasted_iota(I32, (s, LANES), 0)
    for c in range(tc // LANES):
        cs = slice(c * LANES, (c + 1) * LANES)
        u = u_ref[0, :, cs].astype(F32)
        prev = jnp.where(t == 0, 0.0, pltpu.roll(u, 1, axis=0))
        nxt = jnp.where(t == s - 1, 0.0, pltpu.roll(u, s - 1, axis=0))
        y_ref[...] = b_ref[:, cs] + prev * w_ref[0:1, cs] + u * w_ref[1:2, cs] + nxt * w_ref[2:3, cs]
        for r in range(RADIX):
            o_ref[0, r * q:(r + 1) * q, cs] = y_ref[pl.ds(r, q, stride=RADIX), :].astype(o_ref.dtype)


def _short_conv(u, w, bias):
    b, s, c = u.shape
    tc = min(512, c)
    return pl.pallas_call(
        _short_conv_kernel,
        grid=(b, c // tc),
        in_specs=[pl.BlockSpec((1, s, tc), lambda bi, j: (bi, 0, j)),
                  pl.BlockSpec((SHORT_CONV, tc), lambda bi, j: (0, j)),
                  pl.BlockSpec((1, tc), lambda bi, j: (0, j))],
        out_specs=pl.BlockSpec((1, s, tc), lambda bi, j: (bi, 0, j)),
        out_shape=jax.ShapeDtypeStruct((b, s, c), BF16),
        scratch_shapes=[pltpu.VMEM((s, LANES), F32)],
        compiler_params=_cparams(("parallel", "parallel")),
        name="short_conv",
    )(u, w, bias.reshape(1, c))


def _fwd_dft_kernel(mf_ref, z_ref, g_ref, y_ref, zb):
    q = zb.shape[0] // RADIX

    @pl.when(pl.program_id(1) == 0)
    def _():
        zb[...] = z_ref[0].astype(BF16)

    groups = _class_transform(mf_ref, lambda r: zb[r * q:(r + 1) * q, :])
    yc, ys = [], []
    for x, (uc, us) in enumerate(groups):
        gc, gs = g_ref[0, x, 0].astype(F32), g_ref[0, x, 1].astype(F32)
        yc.append(uc * gc - us * gs)
        ys.append(uc * gs + us * gc)
    a, b, c, d = range(RADIX)
    z = [(yc[a] + yc[c] + yc[b] + yc[d], ys[a] + ys[c] - ys[b] - ys[d]),
         (yc[a] + ys[c] - yc[b] + ys[d], ys[a] - yc[c] + ys[b] + yc[d]),
         (yc[a] - yc[c] + yc[b] - yc[d], ys[a] - ys[c] - ys[b] + ys[d]),
         (yc[a] - ys[c] - yc[b] - ys[d], ys[a] + yc[c] + ys[b] - yc[d])]
    for r, (zc, zs) in enumerate(z):
        y_ref[0, r, 0] = zc.astype(BF16)
        y_ref[0, r, 1] = zs.astype(BF16)


def _fwd_dft(mf, z, z_col, g, order, hw):
    b, seq = z.shape[0], z.shape[1]
    q = seq // RADIX
    rows = min(DFT_TILE, q)
    return pl.pallas_call(
        _fwd_dft_kernel,
        grid=(b, q // rows),
        in_specs=[pl.BlockSpec((RADIX, 2, rows, q), lambda bi, i: (0, 0, i, 0)),
                  pl.BlockSpec((1, seq, hw), lambda bi, i: (bi, 0, z_col)),
                  pl.BlockSpec((1, RADIX, 2, rows, hw), lambda bi, i: (order, 0, 0, i, 0))],
        out_specs=pl.BlockSpec((1, RADIX, 2, rows, hw), lambda bi, i: (bi, 0, 0, i, 0)),
        out_shape=jax.ShapeDtypeStruct((b, RADIX, 2, q, hw), BF16),
        scratch_shapes=[pltpu.VMEM((seq, hw), BF16)],
        compiler_params=_cparams(("parallel", "arbitrary")),
        name="hyena_fwd_dft",
    )(mf, z, g)


def _inv_dft_kernel(mi_ref, y_ref, z_ref, gate_ref, skip_ref, *rest, final):
    if final:
        og_ref, o_ref = rest
    else:
        (o_ref,) = rest
    rows = mi_ref.shape[1]
    for r in range(RADIX):
        conv = _dot(mi_ref[r], y_ref[0, r])
        z = gate_ref[0, r].astype(F32) * (conv + z_ref[0, r].astype(F32) * skip_ref[0])
        if final:
            ms = jnp.mean(z * z, axis=-1, keepdims=True)
            zn = z * lax.rsqrt(ms + EPS) * og_ref[...]
            for c in range(zn.shape[1] // LANES):
                o_ref[0, c, pl.ds(r, rows, stride=RADIX), :] = zn[:, c * LANES:(c + 1) * LANES]
        else:
            o_ref[0, r] = z.astype(o_ref.dtype)


def _inv_dft(mi, y, z, z_col, gates, gate_col, skip, order, out_g, hw):
    b, seq = z.shape[0], z.shape[1]
    q = seq // RADIX
    rows = min(DFT_TILE, q)
    final = out_g is not None
    by_class = lambda a: a.reshape(b, RADIX, q, a.shape[2])
    in_specs = [pl.BlockSpec((RADIX, rows, 2 * q), lambda bi, i: (0, i, 0)),
                pl.BlockSpec((1, RADIX, 2 * q, hw), lambda bi, i: (bi, 0, 0, 0)),
                pl.BlockSpec((1, RADIX, rows, hw), lambda bi, i: (bi, 0, i, z_col)),
                pl.BlockSpec((1, RADIX, rows, hw), lambda bi, i: (bi, 0, i, gate_col)),
                pl.BlockSpec((1, 1, hw), lambda bi, i: (order, 0, 0))]
    args = [mi, y.reshape(b, RADIX, 2 * q, hw), by_class(z), by_class(gates), skip.reshape(HYENA_ORDER, 1, hw)]
    if final:
        in_specs.append(pl.BlockSpec((1, hw), lambda bi, i: (0, 0)))
        args.append(out_g.reshape(1, hw))
        out_spec = pl.BlockSpec((1, hw // LANES, RADIX * rows, LANES), lambda bi, i: (bi, 0, i, 0))
        out_shape = jax.ShapeDtypeStruct((b, hw // LANES, seq, LANES), F32)
    else:
        out_spec = pl.BlockSpec((1, RADIX, rows, hw), lambda bi, i: (bi, 0, i, 0))
        out_shape = jax.ShapeDtypeStruct((b, RADIX, q, hw), BF16)
    out = pl.pallas_call(
        functools.partial(_inv_dft_kernel, final=final),
        grid=(b, q // rows),
        in_specs=in_specs,
        out_specs=out_spec,
        out_shape=out_shape,
        compiler_params=_cparams(("parallel", "parallel")),
        name="hyena_inv_dft",
    )(*args)
    return out if final else out.reshape(b, seq, hw)


def _out_kernel(a_ref, hy_ref, x_ref, wo_ref, g1_ref, sh_ref, sc_ref, n2_ref, rw_ref, rb_ref,
                xn_ref, h2_ref, lg_ref):
    aw = a_ref.shape[2]
    hy = jnp.concatenate([hy_ref[0, c] for c in range(hy_ref.shape[1])], axis=1).astype(BF16)
    mix = _dot(a_ref[0], wo_ref[:aw, :]) + _dot(hy, wo_ref[aw:, :])
    xn = x_ref[0] + g1_ref[0] * mix
    xn_ref[0] = xn
    ms = jnp.mean(xn * xn, axis=-1, keepdims=True)
    h2 = (xn * lax.rsqrt(ms + EPS) * n2_ref[...]) * (1.0 + sc_ref[0]) + sh_ref[0]
    _store_row_tiles(h2_ref, h2)
    hh, hl = _split_bf16(h2)
    wh, wl = _split_bf16(rw_ref[...])
    lg_ref[...] = _dot_nt(wh, hh) + (_dot_nt(wh, hl) + _dot_nt(wl, hh)) + rb_ref[...]


def _out_project(attn, hyn, x, wo_bf, g1, sh2, sc2, n2g, rw_t, rb):
    b, s, d = x.shape
    aw, hw = attn.shape[2], hyn.shape[1] * hyn.shape[3]
    ne = rw_t.shape[0]
    tm = min(ROW_TILE, s)
    nt = s // tm
    row = lambda bi, i: (bi, i, 0)
    per_b = lambda bi, i: (bi, 0, 0)
    const = lambda bi, i: (0, 0)
    return pl.pallas_call(
        _out_kernel,
        grid=(b, nt),
        in_specs=[pl.BlockSpec((1, tm, aw), row),
                  pl.BlockSpec((1, hw // LANES, tm, LANES), lambda bi, i: (bi, 0, i, 0)),
                  pl.BlockSpec((1, tm, d), row),
                  pl.BlockSpec((aw + hw, d), const),
                  pl.BlockSpec((1, 1, d), per_b), pl.BlockSpec((1, 1, d), per_b), pl.BlockSpec((1, 1, d), per_b),
                  pl.BlockSpec((1, d), const), pl.BlockSpec((ne, d), const), pl.BlockSpec((ne, 1), const)],
        out_specs=[pl.BlockSpec((1, tm, d), row),
                   pl.BlockSpec((tm * (d // LANES), LANES), lambda bi, i: (bi * nt + i, 0)),
                   pl.BlockSpec((ne, tm), lambda bi, i: (0, bi * nt + i))],
        out_shape=[jax.ShapeDtypeStruct((b, s, d), F32), jax.ShapeDtypeStruct((b * s * (d // LANES), LANES), F32),
                   jax.ShapeDtypeStruct((ne, b * s), F32)],
        compiler_params=_cparams(("parallel", "parallel")),
        name="out_project",
    )(attn, hyn, x, wo_bf, g1, sh2, sc2, n2g, rw_t, rb)


def _route_kernel(lg_ref, gate_ref, lpos_ref, tstart_ref, tcnt_ref, cnt_ref, carry, *, rt):
    ne, tl = lg_ref.shape

    @pl.when(pl.program_id(0) == 0)
    def _():
        carry[...] = jnp.zeros_like(carry)

    l = lg_ref[...]
    rows = lax.broadcasted_iota(I32, (ne, tl), 0).astype(F32)
    vals, sels = [], []
    for k in range(TOP_K):
        m = jnp.max(l, axis=0, keepdims=True)
        ik = jnp.min(jnp.where(l == m, rows, float(ne)), axis=0, keepdims=True)
        sel = rows == ik
        vals.append(m)
        sels.append(sel)
        l = jnp.where(sel, -jnp.inf, l)
    exps = [jnp.exp(v - vals[0]) for v in vals]
    denom = exps[0] + exps[1] + exps[2] + exps[3]
    for k in range(TOP_K):
        gate_ref[k:k + 1, :] = exps[k] / denom
    oh = jnp.zeros((ne, tl), F32)
    for sel in sels:
        oh = oh + jnp.where(sel, 1.0, 0.0)
    r = lax.broadcasted_iota(I32, (tl, tl), 0)
    c = lax.broadcasted_iota(I32, (tl, tl), 1)
    tri = jnp.where(r <= c, 1.0, 0.0).astype(BF16)
    cum = _dot(oh.astype(BF16), tri)
    n_col = jnp.sum(oh, axis=1, keepdims=True)
    er = lax.broadcasted_iota(I32, (ne, LANES), 0)
    ec = lax.broadcasted_iota(I32, (ne, LANES), 1)
    to_lane = lambda col: jnp.sum(jnp.where(er == ec, jnp.broadcast_to(col, (ne, LANES)), 0.0),
                                  axis=0, keepdims=True)
    n_lane = to_lane(n_col)
    off_col = jnp.sum(jnp.where(ec < er, jnp.broadcast_to(n_lane, (ne, LANES)), 0.0), axis=1, keepdims=True)
    slab_pos = cum - oh + off_col
    for k in range(TOP_K):
        pos = jnp.sum(jnp.where(sels[k], slab_pos, 0.0), axis=0, keepdims=True)
        lpos_ref[k:k + 1, :] = (pos * rt).astype(I32)
    tstart_ref[0] = to_lane(carry[:, 0:1]).astype(I32)
    tcnt_ref[0] = n_lane.astype(I32)
    carry[...] = carry[...] + n_col
    cnt_ref[...] = carry[...]


def _route(logits_t, tl, rt):
    ne, t = logits_t.shape
    nt = t // tl
    blk = lambda i: (0, i)
    per_tile = pl.BlockSpec((1, 1, LANES), lambda i: (i, 0, 0))
    return pl.pallas_call(
        functools.partial(_route_kernel, rt=rt),
        grid=(nt,),
        in_specs=[pl.BlockSpec((ne, tl), blk)],
        out_specs=[pl.BlockSpec((TOP_K, tl), blk), pl.BlockSpec((TOP_K, tl), blk), per_tile, per_tile,
                   pl.BlockSpec((ne, LANES), lambda i: (0, 0))],
        out_shape=[jax.ShapeDtypeStruct((TOP_K, t), F32), jax.ShapeDtypeStruct((TOP_K, t), I32),
                   jax.ShapeDtypeStruct((nt, 1, LANES), I32), jax.ShapeDtypeStruct((nt, 1, LANES), I32),
                   jax.ShapeDtypeStruct((ne, LANES), F32)],
        scratch_shapes=[pltpu.VMEM((ne, LANES), F32)],
        compiler_params=_cparams(("arbitrary",)),
        name="moe_route",
    )(logits_t)


def _slots_kernel(cnt_ref, tstart_ref, run_ref, blk_ref, meta_ref, *, rows_per_block):
    ne = cnt_ref.shape[0]
    shift = _log2(rows_per_block)
    cnt = cnt_ref[...].astype(I32)
    padded = ((cnt + (rows_per_block - 1)) >> shift) << shift
    r = lax.broadcasted_iota(I32, (ne, LANES), 0)
    c = lax.broadcasted_iota(I32, (ne, LANES), 1)
    padded_lane = jnp.sum(jnp.where(r == c, padded, 0), axis=0, keepdims=True)
    cnt_lane = jnp.sum(jnp.where(r == c, cnt, 0), axis=0, keepdims=True)
    pend_lane = jnp.sum(jnp.where(r <= c, padded, 0), axis=0, keepdims=True)
    pend_col = jnp.sum(jnp.where(c <= r, jnp.broadcast_to(padded_lane, (ne, LANES)), 0),
                       axis=1, keepdims=True)
    run_ref[...] = tstart_ref[...] + (pend_lane - padded_lane)
    nbp = blk_ref.shape[1]
    j0 = lax.broadcasted_iota(I32, (ne, nbp), 1) * rows_per_block
    be = jnp.sum(jnp.where(jnp.broadcast_to(pend_col, (ne, nbp)) <= j0, 1, 0), axis=0, keepdims=True)
    blk_ref[...] = jnp.minimum(be, ne - 1)
    total = jnp.max(pend_col, axis=0, keepdims=True)
    meta_ref[0:1, :] = pend_lane - padded_lane + cnt_lane
    meta_ref[1:2, :] = padded_lane - cnt_lane
    meta_ref[2:3, :] = jnp.broadcast_to(total >> shift, (1, LANES))
    meta_ref[3:8, :] = jnp.zeros((5, LANES), I32)


def _slots(cnt, tstart, n_blocks, rows_per_block):
    nbp = -(-n_blocks // LANES) * LANES
    return pl.pallas_call(
        functools.partial(_slots_kernel, rows_per_block=rows_per_block),
        out_shape=[jax.ShapeDtypeStruct(tstart.shape, I32), jax.ShapeDtypeStruct((1, nbp), I32),
                   jax.ShapeDtypeStruct((8, LANES), I32)],
        compiler_params=pltpu.CompilerParams(vmem_limit_bytes=V7X_VMEM_LIMIT),
        name="moe_slots",
    )(cnt, tstart)


def _pad_chunks(rows_per_block):
    sizes, s = [], rows_per_block // 2
    while s >= 1:
        sizes.append(s)
        s //= 2
    return sizes


def _rows(start, size, rt):
    return pl.ds(pl.multiple_of(start * rt, rt), size * rt)


def _for_each_run_chunk(run_ref, cnt_ref, tile, ne, max_rows, act):
    sizes = _pad_chunks(2 * max_rows)

    def each(e, off):
        left = cnt_ref[tile * ne + e]
        pos, start = off, run_ref[tile * ne + e]
        for n, size in enumerate(sizes):
            hit = (left & size) != 0

            @pl.when(hit)
            def _():
                act(pos, start, size, n % 2)

            inc = jnp.where(hit, size, 0)
            pos, start = pos + inc, start + inc
        return off + left

    lax.fori_loop(0, ne, each, 0)


def _dispatch_kernel(run_ref, cnt_ref, padlo_ref, npad_ref, nused_ref, lpos_ref, h_ref, buf_ref,
                     slab, zeros, sems, zsem, *, rows_per_block, rt, n_tiles):
    tl = lpos_ref.shape[1]
    ne = padlo_ref.shape[0]
    sizes = _pad_chunks(rows_per_block)
    half = rows_per_block // 2
    i = pl.program_id(0)
    slot = i & 1

    def pad_copy(start, size):
        return pltpu.make_async_copy(zeros.at[_rows(0, size, rt)], buf_ref.at[_rows(start, size, rt)], zsem)

    def start_runs(tile, sl):
        def act(pos, start, size, priority):
            pltpu.make_async_copy(slab.at[sl, _rows(pos, size, rt)], buf_ref.at[_rows(start, size, rt)],
                                  sems.at[sl]).start(priority=priority)

        _for_each_run_chunk(run_ref, cnt_ref, tile, ne, tl, act)

    def wait_runs(sl):
        pltpu.make_async_copy(slab.at[sl], buf_ref.at[_rows(0, TOP_K * tl, rt)], sems.at[sl]).wait()

    @pl.when(i == 0)
    def _():
        zeros[...] = jnp.zeros_like(zeros)

        first, last = 2 * nused_ref[0], buf_ref.shape[0] // (half * rt)
        lax.fori_loop(first, last, lambda j, c: (pad_copy(j * half, half).start(), c)[1], 0)
        lax.fori_loop(first, last, lambda j, c: (pad_copy(j * half, half).wait(), c)[1], 0)

        def each(e, wait):
            start = padlo_ref[e]
            left = npad_ref[e]
            for size in sizes:
                hit = (left & size) != 0

                @pl.when(hit)
                def _():
                    cp = pad_copy(start, size)
                    if wait:
                        cp.wait()
                    else:
                        cp.start()

                start = start + jnp.where(hit, size, 0)

        lax.fori_loop(0, ne, lambda e, c: (each(e, False), c)[1], 0)
        lax.fori_loop(0, ne, lambda e, c: (each(e, True), c)[1], 0)

    def step(sl):
        @pl.when(i >= 2)
        def _():
            wait_runs(sl)

        def fill(t, c):
            row = h_ref[_rows(t, 1, rt), :]
            for k in range(TOP_K):
                slab[sl, pl.ds(pl.multiple_of(lpos_ref[k, t], rt), rt), :] = row
            return c

        lax.fori_loop(0, tl, fill, 0, unroll=8)
        start_runs(i, sl)

        @pl.when(i == n_tiles - 1)
        def _():
            if n_tiles >= 2:
                wait_runs(1 - sl)
            wait_runs(sl)

    for sl in range(2):
        pl.when(slot == sl)(functools.partial(step, sl))


def _dispatch(h2r, lpos, run_start, run_cnt, pad_lo, n_pad, n_used, n_rows, rows_per_block, rt, tl):
    n_tiles = lpos.shape[1] // tl
    return pl.pallas_call(
        functools.partial(_dispatch_kernel, rows_per_block=rows_per_block, rt=rt, n_tiles=n_tiles),
        grid_spec=pltpu.PrefetchScalarGridSpec(
            num_scalar_prefetch=5,
            grid=(n_tiles,),
            in_specs=[pl.BlockSpec((TOP_K, tl), lambda i, *_: (0, i), memory_space=pltpu.SMEM),
                      pl.BlockSpec((tl * rt, LANES), lambda i, *_: (i, 0))],
            out_specs=pl.BlockSpec(memory_space=pl.ANY),
            scratch_shapes=[pltpu.VMEM((2, TOP_K * tl * rt, LANES), F32),
                            pltpu.VMEM((rows_per_block // 2 * rt, LANES), F32),
                            pltpu.SemaphoreType.DMA((2,)), pltpu.SemaphoreType.DMA(())]),
        out_shape=jax.ShapeDtypeStruct((n_rows * rt, LANES), F32),
        compiler_params=_cparams(("arbitrary",)),
        name="moe_dispatch",
    )(run_start, run_cnt, pad_lo, n_pad, n_used, lpos, h2r)


def _expert_kernel(be_ref, nu_ref, x_ref, w1_ref, b1_ref, w2_ref, b2_ref, o_ref, w1b, w2b):
    j = pl.program_id(0)
    active = j < nu_ref[0]

    @pl.when(active & ((j == 0) | (be_ref[j] != be_ref[jnp.maximum(j - 1, 0)])))
    def _():
        w1b[...] = w1_ref[0].astype(BF16)
        w2b[...] = w2_ref[0].astype(BF16)

    @pl.when(active)
    def _():
        de = w2b.shape[0]
        gl = _dot(_load_row_tiles(x_ref, w1b.shape[0] // LANES).astype(BF16), w1b[...]) + b1_ref[0]
        g = jnp.minimum(gl[:, :de], SWIGLU_LIMIT)
        lin = jnp.clip(gl[:, de:], -SWIGLU_LIMIT, SWIGLU_LIMIT)
        glu = g * jax.nn.sigmoid(SWIGLU_ALPHA * g)
        _store_row_tiles(o_ref, _dot(((lin + 1.0) * glu).astype(BF16), w2b[...]) + b2_ref[0])

    @pl.when(pl.program_id(0) >= nu_ref[0])
    def _():
        o_ref[...] = jnp.zeros_like(o_ref)


def _experts(buf, block_e, n_used, w1, b1, w2, b2, rows_per_block):
    ne, d, d2 = w1.shape
    de = w2.shape[1]
    blk_shape = (rows_per_block * (d // LANES), LANES)
    nb = buf.shape[0] // blk_shape[0]
    rowblk = lambda j, be, nu: (jnp.minimum(j, nu[0] - 1), 0)
    by_e = lambda j, be, nu: (be[j], 0, 0)
    return pl.pallas_call(
        _expert_kernel,
        grid_spec=pltpu.PrefetchScalarGridSpec(
            num_scalar_prefetch=2,
            grid=(nb,),
            in_specs=[pl.BlockSpec(blk_shape, rowblk),
                      pl.BlockSpec((1, d, d2), by_e), pl.BlockSpec((1, 1, d2), by_e),
                      pl.BlockSpec((1, de, d), by_e), pl.BlockSpec((1, 1, d), by_e)],
            out_specs=pl.BlockSpec(blk_shape, lambda j, be, nu: (j, 0)),
            scratch_shapes=[pltpu.VMEM((d, d2), BF16), pltpu.VMEM((de, d), BF16)]),
        out_shape=jax.ShapeDtypeStruct(buf.shape, F32),
        compiler_params=_cparams(("arbitrary",)),
        name="moe_experts",
    )(block_e, n_used, buf, w1, b1.reshape(ne, 1, d2), w2, b2.reshape(ne, 1, d))


def _combine_kernel(run_ref, cnt_ref, lpos_ref, gate_ref, xn_ref, g2_ref, ob_ref, o_ref, slab, acc, sems,
                    *, rt, ne, n_tiles):
    tl = lpos_ref.shape[1]
    i = pl.program_id(0) * pl.num_programs(1) + pl.program_id(1)
    slot = i & 1

    def start_runs(tile, sl):
        def act(pos, start, size, priority):
            pltpu.make_async_copy(ob_ref.at[_rows(start, size, rt)], slab.at[sl, _rows(pos, size, rt)],
                                  sems.at[sl]).start(priority=priority)

        _for_each_run_chunk(run_ref, cnt_ref, tile, ne, tl, act)

    def step(sl):
        @pl.when(i == 0)
        def _():
            start_runs(i, sl)

        @pl.when(i + 1 < n_tiles)
        def _():
            start_runs(i + 1, 1 - sl)

        pltpu.make_async_copy(ob_ref.at[_rows(0, TOP_K * tl, rt)], slab.at[sl], sems.at[sl]).wait()

        def row(k, t):
            return slab[sl, pl.ds(pl.multiple_of(lpos_ref[k, t], rt), rt), :]

        def token(t, c):
            a = gate_ref[0, t] * row(0, t)
            for k in range(1, TOP_K):
                a = a + gate_ref[k, t] * row(k, t)
            acc[_rows(t, 1, rt), :] = a
            return c

        lax.fori_loop(0, tl, token, 0, unroll=8)
        o_ref[0] = xn_ref[0] + g2_ref[0] * _load_row_tiles(acc, rt)

    for sl in range(2):
        pl.when(slot == sl)(functools.partial(step, sl))


def _combine(out_buf, lpos, gates, run_start, run_cnt, xn, g2, rt, tl):
    b, s, d = xn.shape
    nt = s // tl
    tok = lambda bi, i, *_: (0, bi * nt + i)
    return pl.pallas_call(
        functools.partial(_combine_kernel, rt=rt, ne=N_EXPERTS, n_tiles=b * nt),
        grid_spec=pltpu.PrefetchScalarGridSpec(
            num_scalar_prefetch=2,
            grid=(b, nt),
            in_specs=[pl.BlockSpec((TOP_K, tl), tok, memory_space=pltpu.SMEM),
                      pl.BlockSpec((TOP_K, tl), tok, memory_space=pltpu.SMEM),
                      pl.BlockSpec((1, tl, d), lambda bi, i, *_: (bi, i, 0)),
                      pl.BlockSpec((1, 1, d), lambda bi, i, *_: (bi, 0, 0)),
                      pl.BlockSpec(memory_space=pl.ANY)],
            out_specs=pl.BlockSpec((1, tl, d), lambda bi, i, *_: (bi, i, 0)),
            scratch_shapes=[pltpu.VMEM((2, TOP_K * tl * rt, LANES), F32), pltpu.VMEM((tl * rt, LANES), F32),
                            pltpu.SemaphoreType.DMA((2,))]),
        out_shape=jax.ShapeDtypeStruct((b, s, d), F32),
        compiler_params=_cparams(("arbitrary", "arbitrary")),
        name="moe_combine",
    )(run_start, run_cnt, lpos, gates, xn, g2, out_buf)


def _moe(h2r, logits_t, xn, g2, w1, b1, w2, b2):
    b, s, d = xn.shape
    t = b * s
    rt = d // LANES
    tl = min(TOKEN_TILE, s)
    assert s % tl == 0
    n_blocks = (t * TOP_K) // EXPERT_ROWS + N_EXPERTS
    n_rows = n_blocks * EXPERT_ROWS
    gates, lpos, tstart, tcnt, cnt = _route(logits_t, tl, rt)
    run, blk, meta = _slots(cnt, tstart, n_blocks, EXPERT_ROWS)
    run_start = run[:, 0, :N_EXPERTS].reshape(-1)
    run_cnt = tcnt[:, 0, :N_EXPERTS].reshape(-1)
    buf = _dispatch(h2r, lpos, run_start, run_cnt, meta[0, :N_EXPERTS], meta[1, :N_EXPERTS], meta[2, :1],
                    n_rows, EXPERT_ROWS, rt, tl)
    out_buf = _experts(buf, blk[0, :n_blocks], meta[2, :1], w1, b1, w2, b2, EXPERT_ROWS)
    return _combine(out_buf, lpos, gates, run_start, run_cnt, xn, g2, rt, tl)


def _layer(x, ctx, c, c_ctx, p, lam_init):
    b, s, d = x.shape
    attn_w = d // 2
    hw = d - attn_w
    v_dim = attn_w // N_HEADS
    qk_dim = v_dim // 2
    qk_cols = N_HEADS * 2 * qk_dim
    v_cols = N_HEADS * v_dim
    assert 2 * qk_dim == LANES and v_dim == LANES and s % GRID_W == 0

    rows = -(-(b + 1) // 8) * 8
    cc = jnp.zeros((rows, d), F32).at[:b].set(c).at[b].set(c_ctx)
    mod = _modulation(cc, p['w_mod'], p['b_mod'])
    mod_x = mod[:b].reshape(b, N_MOD, 1, d)
    sh1, sc1, g1, sh2, sc2, g2 = [mod_x[:, i] for i in range(N_MOD)]
    mod_c = mod[b:b + 1].reshape(1, N_MOD, 1, d)
    csh1, csc1 = mod_c[:, 0], mod_c[:, 1]

    w_in_bf = p['w_in'].astype(BF16)
    qg = jnp.tile(p['q_norm_g'], qk_cols // qk_dim).reshape(1, qk_cols)
    kg = jnp.tile(p['k_norm_g'], qk_cols // qk_dim).reshape(1, qk_cols)
    n1g = p['norm1_g'].reshape(1, d)
    cos_t, sin_t = _rope_tables(s, qk_dim)
    q, k, v, u_hy = _project_latent(x, sh1, sc1, n1g, w_in_bf, qg, kg, cos_t, sin_t, qk_cols, v_cols, qk_dim)
    k_c, v_c = _project_context(ctx, csh1, csc1, n1g, w_in_bf[:, qk_cols:2 * qk_cols + v_cols], kg,
                                qk_cols, v_cols, qk_dim)
    vec = lambda a: a.reshape(1, qk_dim)
    attn = _diff_attention(q, k_c, k, v_c, v, vec(p['lam_q1']), vec(p['lam_k1']), vec(p['lam_q2']),
                           vec(p['lam_k2']), p['subln_g'].reshape(1, v_dim), lam_init, qk_dim)

    hfilt = _hyena_filters(s, hw, p['hy_w1'], p['hy_b1'], p['hy_f1'], p['hy_w2'], p['hy_b2'], p['hy_f2'], p['hy_w3'])
    mf, mi = _dft_matrices(s)
    g_spec = _filter_spectra(mf, hfilt, hw)
    uc = _short_conv(u_hy, p['hy_conv_w'], p['hy_conv_b'])
    y1 = _fwd_dft(mf, uc, 0, g_spec, 0, hw)
    z1 = _inv_dft(mi, y1, uc, 0, uc, 1, p['hy_skip'], 0, None, hw)
    y2 = _fwd_dft(mf, z1, 0, g_spec, 1, hw)
    hyn = _inv_dft(mi, y2, z1, 0, uc, 2, p['hy_skip'], 1, p['hy_out_g'], hw)

    xn, h2, logits_t = _out_project(attn, hyn, x, p['w_out'].astype(BF16), g1, sh2, sc2,
                                    p['norm2_g'].reshape(1, d), p['router_w'].T,
                                    p['router_b'].reshape(N_EXPERTS, 1))
    return _moe(h2, logits_t, xn, g2, p['exp_w1'], p['exp_b1'], p['exp_w2'], p['exp_b2'])


def kernel(x, c, ctx, c_ctx, w_mod, b_mod, norm1_g, norm2_g, w_in, q_norm_g, k_norm_g, lam_q1, lam_k1, lam_q2, lam_k2, subln_g, hy_conv_w, hy_conv_b, hy_w1, hy_b1, hy_f1, hy_w2, hy_b2, hy_f2, hy_w3, hy_skip, hy_out_g, w_out, router_w, router_b, exp_w1, exp_b1, exp_w2, exp_b2):
    depth = w_mod.shape[0]
    assert depth == 1, "context-token update between layers is not implemented"
    p = {
        'w_mod': w_mod[0], 'b_mod': b_mod[0], 'norm1_g': norm1_g[0], 'norm2_g': norm2_g[0],
        'w_in': w_in[0], 'q_norm_g': q_norm_g[0], 'k_norm_g': k_norm_g[0],
        'lam_q1': lam_q1[0], 'lam_k1': lam_k1[0], 'lam_q2': lam_q2[0], 'lam_k2': lam_k2[0],
        'subln_g': subln_g[0], 'hy_conv_w': hy_conv_w[0], 'hy_conv_b': hy_conv_b[0],
        'hy_w1': hy_w1[0], 'hy_b1': hy_b1[0], 'hy_f1': hy_f1[0], 'hy_w2': hy_w2[0],
        'hy_b2': hy_b2[0], 'hy_f2': hy_f2[0], 'hy_w3': hy_w3[0], 'hy_skip': hy_skip[0],
        'hy_out_g': hy_out_g[0], 'w_out': w_out[0], 'router_w': router_w[0],
        'router_b': router_b[0], 'exp_w1': exp_w1[0], 'exp_b1': exp_b1[0],
        'exp_w2': exp_w2[0], 'exp_b2': exp_b2[0],
    }
    lam_init = 0.8 - 0.6 * math.exp(-0.3 * 0)
    return _layer(x, ctx, c, c_ctx, p, lam_init)
```
